```python
import functools
import jax, jax.numpy as jnp
from jax import lax
import numpy as np

D_MODEL = 1024
BATCH = 8
SEQ = 4096
DEPTH = 1
DEC_BATCH = 128
DEC_SEQ = 8
PAST_LEN = 16384
PAGE_SIZE = 128

POOL_WIDTH = D_MODEL // 2
POOL_WINDOWS = (2, 4, 8, 16)
N_POOL_GROUPS = len(POOL_WINDOWS)
POOL_GROUP = POOL_WIDTH // N_POOL_GROUPS
POOL_STATE = max(POOL_WINDOWS) - 1
HEAD_DIM = 64
N_HEADS = (D_MODEL // 2) // HEAD_DIM
N_KV_HEADS = 2
GQA_GROUP = N_HEADS // N_KV_HEADS
WINDOW = 128
BLOCK = WINDOW
KV_BUF = min(WINDOW, PAST_LEN)
ATTN_WIDTH = N_HEADS * HEAD_DIM
KV_WIDTH = N_KV_HEADS * HEAD_DIM
ATTN_SCALE = HEAD_DIM ** -0.5
N_BRANCHES = 2
IN_WIDTH = POOL_WIDTH + ATTN_WIDTH + 2 * KV_WIDTH + N_BRANCHES * D_MODEL
SPLITS = (POOL_WIDTH, POOL_WIDTH + ATTN_WIDTH, POOL_WIDTH + ATTN_WIDTH + KV_WIDTH,
          POOL_WIDTH + ATTN_WIDTH + 2 * KV_WIDTH)
D_FF = 2816
CONV_WIDTH = 3
CONV_STATE = CONV_WIDTH - 1
EPS = 1e-6

kernel_name = "hybrid_pool_swa_sink_convffn_step"


def rmsnorm(x, g):
    xf = x.astype(jnp.float32)
    inv = lax.rsqrt(jnp.mean(xf * xf, axis=-1, keepdims=True) + EPS)
    return (xf * inv).astype(x.dtype) * g


def pool_mix(u, prefix, pos0, w_pool_grp, pool_scale):
    B, T, _ = u.shape
    ext = jnp.concatenate([prefix, u], axis=1).astype(jnp.float32)
    cs = jnp.concatenate([jnp.zeros((B, 1, POOL_WIDTH), jnp.float32),
                          jnp.cumsum(ext, axis=1)], axis=1)
    end = cs[:, POOL_STATE + 1:]
    pos = pos0 + jnp.arange(T)
    outs = []
    for g, w in enumerate(POOL_WINDOWS):
        sl = slice(g * POOL_GROUP, (g + 1) * POOL_GROUP)
        start = cs[:, POOL_STATE + 1 - w:POOL_STATE + 1 - w + T, sl]
        cnt = jnp.minimum(pos + 1, w).astype(jnp.float32)[None, :, None]
        outs.append((end[..., sl] - start) / cnt - ext[:, POOL_STATE:, sl])
    d = jnp.stack(outs, axis=2).astype(u.dtype)
    y = jnp.einsum('btgc,gcd->btgd', d, w_pool_grp).reshape(B, T, POOL_WIDTH)
    return y * pool_scale


def sink_softmax(s, mask, sinks):
    s = jnp.where(mask, s, -jnp.inf)
    sk = jnp.broadcast_to(sinks.astype(jnp.float32).reshape(N_KV_HEADS, GQA_GROUP, 1, 1),
                          s.shape[:-1] + (1,))
    p = jax.nn.softmax(jnp.concatenate([s, sk], axis=-1), axis=-1)
    return p[..., :-1]


def attn_prompt(q, k, v, sinks):
    B, T = q.shape[:2]
    NB = T // BLOCK
    qb = q.reshape(B, NB, BLOCK, N_KV_HEADS, GQA_GROUP, HEAD_DIM)
    kb = k.reshape(B, NB, BLOCK, N_KV_HEADS, HEAD_DIM)
    vb = v.reshape(B, NB, BLOCK, N_KV_HEADS, HEAD_DIM)
    kk = jnp.concatenate([jnp.concatenate([jnp.zeros_like(kb[:, :1]), kb[:, :-1]], axis=1), kb], axis=2)
    vv = jnp.concatenate([jnp.concatenate([jnp.zeros_like(vb[:, :1]), vb[:, :-1]], axis=1), vb], axis=2)
    s = jnp.einsum('bnqkgd,bnskd->bnkgqs', qb, kk).astype(jnp.float32) * ATTN_SCALE
    i = jnp.arange(BLOCK)[:, None]
    j = jnp.arange(2 * BLOCK)[None, :]
    diff = i + BLOCK - j
    band = (diff >= 0) & (diff <= WINDOW)
    key_ok = (jnp.arange(NB)[:, None, None] * BLOCK - BLOCK + j[None]) >= 0
    mask = (band[None] & key_ok)[None, :, None, None]
    p = sink_softmax(s, mask, sinks).astype(v.dtype)
    o = jnp.einsum('bnkgqs,bnskd->bnqkgd', p, vv).reshape(B, T, ATTN_WIDTH)
    return o, k[:, -KV_BUF:], v[:, -KV_BUF:]


def attn_sample(q, k, v, sinks, cache_k, cache_v):
    B, T = q.shape[:2]
    kk = jnp.concatenate([cache_k, k], axis=1)
    vv = jnp.concatenate([cache_v, v], axis=1)
    qg = q.reshape(B, T, N_KV_HEADS, GQA_GROUP, HEAD_DIM)
    s = jnp.einsum('bqkgd,bskd->bkgqs', qg, kk).astype(jnp.float32) * ATTN_SCALE
    diff = jnp.arange(T)[:, None] + KV_BUF - jnp.arange(KV_BUF + T)[None, :]
    mask = ((diff >= 0) & (diff <= WINDOW))[None, None, None]
    p = sink_softmax(s, mask, sinks).astype(v.dtype)
    o = jnp.einsum('bkgqs,bskd->bqkgd', p, vv).reshape(B, T, ATTN_WIDTH)
    return o, kk[:, -KV_BUF:], vv[:, -KV_BUF:]


def conv_ffn(x, prefix, g_ffn, w_up, conv_w, conv_b, w_down):
    h = rmsnorm(x, g_ffn)
    up = h @ w_up
    ext = jnp.concatenate([prefix, up], axis=1)
    c = lax.conv_general_dilated(ext, conv_w[:, None, :].astype(ext.dtype), window_strides=(1,),
                                 padding='VALID', dimension_numbers=('NWC', 'WIO', 'NWC'),
                                 feature_group_count=2 * D_FF) + conv_b
    gate, val = jnp.split(c, 2, axis=-1)
    return (jax.nn.silu(gate) * val) @ w_down, ext[:, -CONV_STATE:]


def layer(x, pool_prefix, pos0, attend, conv_prefix, g_mix, w_in, w_pool_grp, pool_scale, sinks,
          w_branch_pool, w_branch_attn, w_out, g_ffn, w_up, conv_w, conv_b, w_down):
    B, T, _ = x.shape
    h = rmsnorm(x, g_mix)
    z = h @ w_in
    u, q, k, v, gl = jnp.split(z, SPLITS, axis=-1)
    a = pool_mix(u, pool_prefix, pos0, w_pool_grp, pool_scale)
    o, k_state, v_state = attend(q.reshape(B, T, N_HEADS, HEAD_DIM),
                                 k.reshape(B, T, N_KV_HEADS, HEAD_DIM),
                                 v.reshape(B, T, N_KV_HEADS, HEAD_DIM), sinks)
    ga, gb = jnp.split(jax.nn.sigmoid(gl), N_BRANCHES, axis=-1)
    mixed = ga * (a @ w_branch_pool) + gb * (o @ w_branch_attn)
    x = x + mixed @ w_out
    f, conv_state = conv_ffn(x, conv_prefix, g_ffn, w_up, conv_w, conv_b, w_down)
    x = x + f
    pool_state = jnp.concatenate([pool_prefix, u], axis=1)[:, -POOL_STATE:]
    return x, pool_state, k_state, v_state, conv_state


def setup_inputs(seed: int = 0) -> dict:
    key = jax.random.key(seed)
    ks = jax.random.split(key, 20)

    def nrm(k, shape, scale):
        return jax.random.normal(k, shape, jnp.float32) * scale

    return {
        "x_prompt": nrm(ks[0], (BATCH, SEQ, D_MODEL), 1.0),
        "x_sample": nrm(ks[1], (DEC_BATCH, DEC_SEQ, D_MODEL), 1.0),
        "state_pool": nrm(ks[2], (DEPTH, DEC_BATCH, POOL_STATE, POOL_WIDTH), 1.0),
        "cache_k": nrm(ks[3], (DEPTH, DEC_BATCH, KV_BUF, N_KV_HEADS, HEAD_DIM), 1.0),
        "cache_v": nrm(ks[4], (DEPTH, DEC_BATCH, KV_BUF, N_KV_HEADS, HEAD_DIM), 1.0),
        "state_conv": nrm(ks[5], (DEPTH, DEC_BATCH, CONV_STATE, 2 * D_FF), 1.0),
        "g_mix": 1.0 + nrm(ks[6], (DEPTH, D_MODEL), 0.02),
        "w_in": nrm(ks[7], (DEPTH, D_MODEL, IN_WIDTH), D_MODEL ** -0.5),
        "w_pool_grp": nrm(ks[8], (DEPTH, N_POOL_GROUPS, POOL_GROUP, POOL_GROUP), POOL_GROUP ** -0.5),
        "pool_scale": 1.0 + nrm(ks[9], (DEPTH, POOL_WIDTH), 0.02),
        "sinks": nrm(ks[10], (DEPTH, N_HEADS), 0.5),
        "w_branch_pool": nrm(ks[11], (DEPTH, POOL_WIDTH, D_MODEL), POOL_WIDTH ** -0.5),
        "w_branch_attn": nrm(ks[12], (DEPTH, ATTN_WIDTH, D_MODEL), ATTN_WIDTH ** -0.5),
        "w_out": nrm(ks[13], (DEPTH, D_MODEL, D_MODEL), D_MODEL ** -0.5),
        "g_ffn": 1.0 + nrm(ks[14], (DEPTH, D_MODEL), 0.02),
        "w_up": nrm(ks[15], (DEPTH, D_MODEL, 2 * D_FF), D_MODEL ** -0.5),
        "conv_w": nrm(ks[16], (DEPTH, CONV_WIDTH, 2 * D_FF), CONV_WIDTH ** -0.5),
        "conv_b": nrm(ks[17], (DEPTH, 2 * D_FF), 0.01),
        "w_down": nrm(ks[18], (DEPTH, D_FF, D_MODEL), D_FF ** -0.5),
        "g_final": 1.0 + nrm(ks[19], (D_MODEL,), 0.02),
    }


def reference(x_prompt, x_sample, state_pool, cache_k, cache_v, state_conv, g_mix, w_in,
              w_pool_grp, pool_scale, sinks, w_branch_pool, w_branch_attn, w_out, g_ffn,
              w_up, conv_w, conv_b, w_down, g_final):
    yp, ys = x_prompt, x_sample
    pool_p, k_p, v_p, conv_p = [], [], [], []
    pool_s, k_s, v_s, conv_s = [], [], [], []
    for l in range(DEPTH):
        lw = (g_mix[l], w_in[l], w_pool_grp[l], pool_scale[l], sinks[l], w_branch_pool[l],
              w_branch_attn[l], w_out[l], g_ffn[l], w_up[l], conv_w[l], conv_b[l], w_down[l])
        zp = jnp.zeros((yp.shape[0], POOL_STATE, POOL_WIDTH), yp.dtype)
        zc = jnp.zeros((yp.shape[0], CONV_STATE, 2 * D_FF), yp.dtype)
        yp, a1, a2, a3, a4 = layer(yp, zp, 0, attn_prompt, zc, *lw)
        pool_p.append(a1); k_p.append(a2); v_p.append(a3); conv_p.append(a4)
        att = functools.partial(attn_sample, cache_k=cache_k[l], cache_v=cache_v[l])
        ys, b1, b2, b3, b4 = layer(ys, state_pool[l], PAST_LEN, att, state_conv[l], *lw)
        pool_s.append(b1); k_s.append(b2); v_s.append(b3); conv_s.append(b4)
    yp = rmsnorm(yp, g_final)
    ys = rmsnorm(ys, g_final)
    return (yp, ys, jnp.stack(pool_p), jnp.stack(k_p), jnp.stack(v_p), jnp.stack(conv_p),
            jnp.stack(pool_s), jnp.stack(k_s), jnp.stack(v_s), jnp.stack(conv_s))
```

```python
import functools

import jax
import jax.numpy as jnp
from jax import lax
from jax.experimental import pallas as pl
from jax.experimental.pallas import tpu as pltpu

D_MODEL = 1024
POOL_WIDTH = 512
POOL_WINDOWS = (2, 4, 8, 16)
POOL_GROUP = 128
POOL_STATE = 15
HEAD_DIM = 64
N_HEADS = 8
N_KV_HEADS = 2
GQA_GROUP = N_HEADS // N_KV_HEADS
WINDOW = 128
ATTN_WIDTH = N_HEADS * HEAD_DIM
KV_WIDTH = N_KV_HEADS * HEAD_DIM
ATTN_SCALE = HEAD_DIM ** -0.5
D_FF = 2816
CONV_STATE = 2
EPS = 1e-6
PAST_LEN = 16384

LANES = 128
SUBLANES = 8
VMEM_LIMIT = 56 * 1024 * 1024

F32 = jnp.float32
BF16 = jnp.bfloat16


def _rmsnorm(x, g):
    inv = lax.rsqrt(jnp.mean(x * x, axis=-1, keepdims=True) + EPS)
    return (x * inv) * g


def _dot(a, b):
    return jnp.dot(a, b, preferred_element_type=F32)


def _const_spec(shape):
    nd = len(shape)
    return pl.BlockSpec(shape, lambda *_: (0,) * nd, pipeline_mode=pl.Buffered(1))


def _params(n_axes):
    return pltpu.CompilerParams(dimension_semantics=("arbitrary",) * n_axes,
                                vmem_limit_bytes=VMEM_LIMIT)


def _in_proj_kernel(x_ref, g_ref, w_ref, u_ref, q_ref, k_ref, v_ref, gl_ref):
    h = _rmsnorm(x_ref[...], g_ref[...]).astype(BF16)
    o_q = POOL_WIDTH
    o_k = o_q + ATTN_WIDTH
    o_gl = o_k + 2 * KV_WIDTH
    u_ref[...] = _dot(h, w_ref[:, 0:o_q])
    q_ref[...] = _dot(h, w_ref[:, o_q:o_k])
    kv = _dot(h, w_ref[:, o_k:o_gl])
    k_ref[...] = kv[:, :KV_WIDTH]
    v_ref[...] = kv[:, KV_WIDTH:]
    gl_ref[...] = _dot(h, w_ref[:, o_gl:])


def _in_proj(x, g_mix, w_in_bf, tm):
    n = x.shape[0]
    in_width = w_in_bf.shape[1]
    row = lambda c: pl.BlockSpec((tm, c), lambda i: (i, 0))
    return pl.pallas_call(
        _in_proj_kernel,
        grid=(n // tm,),
        in_specs=[row(D_MODEL), _const_spec((1, D_MODEL)), _const_spec((D_MODEL, in_width))],
        out_specs=[row(POOL_WIDTH), row(ATTN_WIDTH), row(KV_WIDTH), row(KV_WIDTH), row(2 * D_MODEL)],
        out_shape=[jax.ShapeDtypeStruct((n, POOL_WIDTH), F32),
                   jax.ShapeDtypeStruct((n, ATTN_WIDTH), F32),
                   jax.ShapeDtypeStruct((n, KV_WIDTH), F32),
                   jax.ShapeDtypeStruct((n, KV_WIDTH), F32),
                   jax.ShapeDtypeStruct((n, 2 * D_MODEL), F32)],
        compiler_params=_params(1),
        name="in_proj",
    )(x, g_mix, w_in_bf)


def _dup_kv_head(x, g):
    lane = lax.broadcasted_iota(jnp.int32, x.shape, 1)
    r = pltpu.roll(x, HEAD_DIM, axis=1)
    lo = lane < HEAD_DIM
    out = jnp.where(lo, x, r) if g == 0 else jnp.where(lo, r, x)
    return out.astype(BF16)


def _attend(q, kk, vv, valid4, sinks_ref):
    r = q.shape[0]
    lane = lax.broadcasted_iota(jnp.int32, (r, LANES), 1)
    lo = lane < HEAD_DIM
    scale_lo = jnp.where(lo, ATTN_SCALE, 0.0).astype(F32)
    scale_hi = jnp.where(lo, 0.0, ATTN_SCALE).astype(F32)
    row4 =lax.broadcasted_iota(jnp.int32, (GQA_GROUP * r, 1), 0)
    outs = []
    for g in range(N_KV_HEADS):
        kg = _dup_kv_head(kk, g)
        vg = _dup_kv_head(vv, g)
        base = g * GQA_GROUP * HEAD_DIM
        p0 = q[:, base:base + LANES]
        p1 = q[:, base + LANES:base + 2 * LANES]
        qs = jnp.concatenate([p0 * scale_lo, p0 * scale_hi, p1 * scale_lo, p1 * scale_hi],
                             axis=0).astype(BF16)
        s = lax.dot_general(qs, kg, (((1,), (1,)), ((), ())), preferred_element_type=F32)
        s = jnp.where(valid4, s, -jnp.inf)
        sk = jnp.full((GQA_GROUP * r, 1), sinks_ref[g * GQA_GROUP + GQA_GROUP - 1], F32)
        for hh in range(GQA_GROUP - 2, -1, -1):
            sk = jnp.where(row4 < (hh + 1) * r, sinks_ref[g * GQA_GROUP + hh], sk)
        m = jnp.maximum(jnp.max(s, axis=-1, keepdims=True), sk)
        e = jnp.exp(s - m)
        den = jnp.sum(e, axis=-1, keepdims=True) + jnp.exp(sk - m)
        p = (e * (1.0 / den)).astype(BF16)
        o = _dot(p, vg)
        outs.append(jnp.where(lo, o[0:r], o[r:2 * r]))
        outs.append(jnp.where(lo, o[2 * r:3 * r], o[3 * r:4 * r]))
    return jnp.concatenate(outs, axis=1)


def _band4(r, s, first_key=0):
    i = lax.broadcasted_iota(jnp.int32, (GQA_GROUP * r, s), 0) & (r - 1)
    j = lax.broadcasted_iota(jnp.int32, (GQA_GROUP * r, s), 1)
    return (j >= jnp.maximum(i, first_key)) & (j <= i + WINDOW)


def _merge(x, pool_a, attn_o, gl, wbp_ref, wba_ref, wout_ref):
    pa = _dot(pool_a.astype(BF16), wbp_ref[...])
    po = _dot(attn_o.astype(BF16), wba_ref[...])
    ga = jax.nn.sigmoid(gl[:, :D_MODEL])
    gb = jax.nn.sigmoid(gl[:, D_MODEL:])
    mixed = (ga * pa + gb * po).astype(BF16)
    return x + _dot(mixed, wout_ref[...])


def _mix_prompt_kernel(sinks_ref, x_ref, u_ref, q_ref, k_ref, v_ref, gl_ref, wpool_ref, pscale_ref,
                       wbp_ref, wba_ref, wout_ref, o_ref, uext, kext, vext, oscr, *, tq):
    i = pl.program_id(1)
    pre = POOL_STATE + 1

    @pl.when(i == 0)
    def _():
        uext[0:pre, :] = jnp.zeros((pre, POOL_WIDTH), F32)
        kext[0:WINDOW, :] = jnp.zeros((WINDOW, KV_WIDTH), F32)
        vext[0:WINDOW, :] = jnp.zeros((WINDOW, KV_WIDTH), F32)

    uext[pre:pre + tq, :] = u_ref[...]
    kext[WINDOW:WINDOW + tq, :] = k_ref[...]
    vext[WINDOW:WINDOW + tq, :] = v_ref[...]

    pos1 = i * tq + lax.broadcasted_iota(jnp.int32, (tq, 1), 0) + 1
    a_parts = []
    for g, w in enumerate(POOL_WINDOWS):
        cols = slice(g * POOL_GROUP, (g + 1) * POOL_GROUP)
        xt = uext[pre:pre + tq, cols]
        acc = xt
        for j in range(1, w):
            acc = acc + uext[pre - j:pre - j + tq, cols]
        cnt = jnp.minimum(pos1, w).astype(F32)
        d = acc / cnt - xt
        y = _dot(d.astype(BF16), wpool_ref[g])
        a_parts.append(y * pscale_ref[:, cols])
    pool_a = jnp.concatenate(a_parts, axis=1)

    for nb in range(tq // WINDOW):
        first_key = jnp.where(i * (tq // WINDOW) + nb > 0, 0, WINDOW)
        valid = _band4(WINDOW, 2 * WINDOW, first_key)
        rows = slice(nb * WINDOW, (nb + 1) * WINDOW)
        keys = slice(nb * WINDOW, (nb + 2) * WINDOW)
        oscr[rows, :] = _attend(q_ref[rows, :], kext[keys, :], vext[keys, :], valid, sinks_ref)

    o_ref[...] = _merge(x_ref[...], pool_a, oscr[...], gl_ref[...], wbp_ref, wba_ref, wout_ref)

    uext[0:pre, :] = uext[tq:tq + pre, :]
    kext[0:WINDOW, :] = kext[tq:tq + WINDOW, :]
    vext[0:WINDOW, :] = vext[tq:tq + WINDOW, :]


def _mix_prompt(x, u, q, k, v, gl, sinks, wpool_bf, pscale, wbp_bf, wba_bf, wout_bf, batch, tq):
    n = x.shape[0]
    nt = n // batch // tq
    row = lambda c: pl.BlockSpec((tq, c), lambda b, i: (b * nt + i, 0))
    return pl.pallas_call(
        functools.partial(_mix_prompt_kernel, tq=tq),
        grid=(batch, nt),
        in_specs=[pl.BlockSpec(memory_space=pltpu.SMEM),
                  row(D_MODEL), row(POOL_WIDTH), row(ATTN_WIDTH), row(KV_WIDTH), row(KV_WIDTH),
                  row(2 * D_MODEL),
                  _const_spec(wpool_bf.shape), _const_spec(pscale.shape),
                  _const_spec(wbp_bf.shape), _const_spec(wba_bf.shape), _const_spec(wout_bf.shape)],
        out_specs=row(D_MODEL),
        out_shape=jax.ShapeDtypeStruct((n, D_MODEL), F32),
        scratch_shapes=[pltpu.VMEM((POOL_STATE + 1 + tq, POOL_WIDTH), F32),
                        pltpu.VMEM((WINDOW + tq, KV_WIDTH), F32),
                        pltpu.VMEM((WINDOW + tq, KV_WIDTH), F32),
                        pltpu.VMEM((tq, ATTN_WIDTH), F32)],
        compiler_params=_params(2),
        name="mix_prompt",
    )(sinks, x, u, q, k, v, gl, wpool_bf, pscale, wbp_bf, wba_bf, wout_bf)


def _tile_shift(cur, prev, j, row):
    return jnp.where(row < j, pltpu.roll(prev, j, axis=1), pltpu.roll(cur, j, axis=1))


def _mix_sample_kernel(sinks_ref, x_ref, u_ref, st_ref, q_ref, k_ref, v_ref, ck_ref, cv_ref, gl_ref,
                       wpool_ref, pscale_ref, wbp_ref, wba_ref, wout_ref, o_ref, oscr, *, gs, pos0):
    t = SUBLANES
    rows = gs * t

    row = lax.broadcasted_iota(jnp.int32, (gs, t, POOL_GROUP), 1)
    cnt_pos = pos0 + lax.broadcasted_iota(jnp.int32, (gs, t, 1), 1) + 1
    a_parts = []
    for g, w in enumerate(POOL_WINDOWS):
        cols = slice(g * POOL_GROUP, (g + 1) * POOL_GROUP)
        tiles = [st_ref[:, 0:t, cols], st_ref[:, t:2 * t, cols], u_ref[:, :, cols]]
        xt = tiles[2]
        step = 1
        while step < w and step < t:
            tiles = [tiles[n] + _tile_shift(tiles[n], tiles[max(n - 1, 0)], step, row)
                     for n in range(3)]
            step *= 2
        acc = tiles[2] + tiles[1] if w == 2 * t else tiles[2]
        cnt = jnp.minimum(cnt_pos, w).astype(F32)
        d = (acc / cnt - xt).reshape(rows, POOL_GROUP)
        y = _dot(d.astype(BF16), wpool_ref[g])
        a_parts.append(y * pscale_ref[:, cols])
    pool_a = jnp.concatenate(a_parts, axis=1)

    s_len = 2 * WINDOW
    band = _band4(t, s_len)
    pad = jnp.zeros((s_len - WINDOW - t, KV_WIDTH), F32)

    def seq_body(s, carry):
        r0 = pl.multiple_of(s * t, t)
        kk = jnp.concatenate([ck_ref[s], k_ref[pl.ds(r0, t), :], pad], axis=0)
        vv = jnp.concatenate([cv_ref[s], v_ref[pl.ds(r0, t), :], pad], axis=0)
        oscr[pl.ds(r0, t), :] = _attend(q_ref[pl.ds(r0, t), :], kk, vv, band, sinks_ref)
        return carry

    lax.fori_loop(0, gs, seq_body, 0)

    o_ref[...] = _merge(x_ref[...], pool_a, oscr[...], gl_ref[...], wbp_ref, wba_ref, wout_ref)


def _mix_sample(x, u3, stpad, q, k, v, ck, cv, gl, sinks, wpool_bf, pscale, wbp_bf, wba_bf, wout_bf,
                gs, pos0):
    nseq, t = u3.shape[0], u3.shape[1]
    rows = gs * t
    row = lambda c: pl.BlockSpec((rows, c), lambda i: (i, 0))
    seq = lambda a, b: pl.BlockSpec((gs, a, b), lambda i: (i, 0, 0))
    return pl.pallas_call(
        functools.partial(_mix_sample_kernel, gs=gs, pos0=pos0),
        grid=(nseq // gs,),
        in_specs=[pl.BlockSpec(memory_space=pltpu.SMEM),
                  row(D_MODEL), seq(t, POOL_WIDTH), seq(2 * t, POOL_WIDTH),
                  row(ATTN_WIDTH), row(KV_WIDTH), row(KV_WIDTH),
                  seq(WINDOW, KV_WIDTH), seq(WINDOW, KV_WIDTH), row(2 * D_MODEL),
                  _const_spec(wpool_bf.shape), _const_spec(pscale.shape),
                  _const_spec(wbp_bf.shape), _const_spec(wba_bf.shape), _const_spec(wout_bf.shape)],
        out_specs=row(D_MODEL),
        out_shape=jax.ShapeDtypeStruct((nseq * t, D_MODEL), F32),
        scratch_shapes=[pltpu.VMEM((rows, ATTN_WIDTH), F32)],
        compiler_params=_params(1),
        name="mix_sample",
    )(sinks, x, u3, stpad, q, k, v, ck, cv, gl, wpool_bf, pscale, wbp_bf, wba_bf, wout_bf)


FF_CHUNK = 256


def _gated_act(taps, cw_ref, cb_ref, c0):
    def conv(cols):
        x0, x1, x2 = taps(cols)
        return (x0 * cw_ref[0:1, cols] + x1 * cw_ref[1:2, cols] + x2 * cw_ref[2:3, cols]
                + cb_ref[:, cols])
    gate = conv(slice(c0, c0 + FF_CHUNK))
    val = conv(slice(D_FF + c0, D_FF + c0 + FF_CHUNK))
    return (jax.nn.silu(gate) * val).astype(BF16)


def _ffn_prompt_kernel(x_ref, g_ref, wup_ref, cw_ref, cb_ref, wdn_ref, gf_ref, o_ref, cs_ref,
                       upx, act, *, tm):
    i = pl.program_id(1)
    pre = SUBLANES

    @pl.when(i == 0)
    def _():
        upx[0:pre, :] = jnp.zeros((pre, 2 * D_FF), F32)

    x = x_ref[...]
    h = _rmsnorm(x, g_ref[...]).astype(BF16)
    upx[pre:pre + tm, :] = _dot(h, wup_ref[...])

    def taps(cols):
        return (upx[pre - 2:pre - 2 + tm, cols], upx[pre - 1:pre - 1 + tm, cols],
                upx[pre:pre + tm, cols])

    for c0 in range(0, D_FF, FF_CHUNK):
        act[:, c0:c0 + FF_CHUNK] = _gated_act(taps, cw_ref, cb_ref, c0)

    y = x + _dot(act[...], wdn_ref[...])
    o_ref[...] = _rmsnorm(y, gf_ref[...])
    cs_ref[0] = upx[pre + tm - CONV_STATE:pre + tm, :]
    upx[0:pre, :] = upx[tm:tm + pre, :]


def _ffn_prompt(x, g_ffn, wup_bf, conv_w, conv_b, wdn_bf, g_final, batch, tm):
    n = x.shape[0]
    nt = n // batch // tm
    row = pl.BlockSpec((tm, D_MODEL), lambda b, i: (b * nt + i, 0))
    return pl.pallas_call(
        functools.partial(_ffn_prompt_kernel, tm=tm),
        grid=(batch, nt),
        in_specs=[row, _const_spec((1, D_MODEL)), _const_spec(wup_bf.shape),
                  _const_spec(conv_w.shape), _const_spec(conv_b.shape), _const_spec(wdn_bf.shape),
                  _const_spec((1, D_MODEL))],
        out_specs=[row, pl.BlockSpec((1, CONV_STATE, 2 * D_FF), lambda b, i: (b, 0, 0))],
        out_shape=[jax.ShapeDtypeStruct((n, D_MODEL), F32),
                   jax.ShapeDtypeStruct((batch, CONV_STATE, 2 * D_FF), F32)],
        scratch_shapes=[pltpu.VMEM((SUBLANES + tm, 2 * D_FF), F32),
                        pltpu.VMEM((tm, D_FF), BF16)],
        compiler_params=_params(2),
        name="ffn_prompt",
    )(x, g_ffn, wup_bf, conv_w, conv_b, wdn_bf, g_final)


def _ffn_sample_kernel(x_ref, st_ref, g_ref, wup_ref, cw_ref, cb_ref, wdn_ref, gf_ref,
                       o_ref, cs_ref, up, act, *, gs):
    t = SUBLANES
    rows = gs * t
    x = x_ref[...]
    h = _rmsnorm(x, g_ref[...]).astype(BF16)
    up[...] = _dot(h, wup_ref[...])
    row = lax.broadcasted_iota(jnp.int32, (gs, t, FF_CHUNK), 1)

    def taps(cols):
        x2 = up[:, cols].reshape(gs, t, FF_CHUNK)
        s0 = jnp.broadcast_to(st_ref[:, 0:1, cols], x2.shape)
        s1 = jnp.broadcast_to(st_ref[:, 1:2, cols], x2.shape)
        x1 = jnp.where(row < 1, s1, pltpu.roll(x2, 1, axis=1))
        x0 = jnp.where(row < 1, s0, jnp.where(row < 2, s1, pltpu.roll(x2, 2, axis=1)))
        flat = lambda a: a.reshape(rows, FF_CHUNK)
        return flat(x0), flat(x1), flat(x2)

    for c0 in range(0, D_FF, FF_CHUNK):
        act[:, c0:c0 + FF_CHUNK] = _gated_act(taps, cw_ref, cb_ref, c0)

    y = x + _dot(act[...], wdn_ref[...])
    o_ref[...] = _rmsnorm(y, gf_ref[...])
    for c0 in range(0, 2 * D_FF, 4 * FF_CHUNK):
        cols = slice(c0, min(c0 + 4 * FF_CHUNK, 2 * D_FF))
        cs_ref[:, :, cols] = up[:, cols].reshape(gs, t, -1)[:, t - CONV_STATE:, :]


def _ffn_sample(x, st, g_ffn, wup_bf, conv_w, conv_b, wdn_bf, g_final, gs):
    nseq = st.shape[0]
    t = x.shape[0] // nseq
    rows = gs * t
    row = pl.BlockSpec((rows, D_MODEL), lambda i: (i, 0))
    st_spec = pl.BlockSpec((gs, CONV_STATE, 2 * D_FF), lambda i: (i, 0, 0))
    return pl.pallas_call(
        functools.partial(_ffn_sample_kernel, gs=gs),
        grid=(nseq // gs,),
        in_specs=[row, st_spec,
                  _const_spec((1, D_MODEL)), _const_spec(wup_bf.shape),
                  _const_spec(conv_w.shape), _const_spec(conv_b.shape), _const_spec(wdn_bf.shape),
                  _const_spec((1, D_MODEL))],
        out_specs=[row, st_spec],
        out_shape=[jax.ShapeDtypeStruct((nseq * t, D_MODEL), F32),
                   jax.ShapeDtypeStruct((nseq, CONV_STATE, 2 * D_FF), F32)],
        scratch_shapes=[pltpu.VMEM((rows, 2 * D_FF), F32), pltpu.VMEM((rows, D_FF), BF16)],
        compiler_params=_params(1),
        name="ffn_sample",
    )(x, st, g_ffn, wup_bf, conv_w, conv_b, wdn_bf, g_final)


TM_IN = 512
TQ_MIX = 256
TM_FFN = 256
GS_MIX = 16
GS_FFN = 32


def kernel(x_prompt, x_sample, state_pool, cache_k, cache_v, state_conv, g_mix, w_in, w_pool_grp,
           pool_scale, sinks, w_branch_pool, w_branch_attn, w_out, g_ffn, w_up, conv_w, conv_b,
           w_down, g_final):
    depth = g_mix.shape[0]
    batch, seq, _ = x_prompt.shape
    dec_batch, dec_seq, _ = x_sample.shape
    assert dec_seq == SUBLANES and seq % TQ_MIX == 0 and seq % TM_FFN == 0
    assert depth == 1

    yp = x_prompt.reshape(batch * seq, D_MODEL)
    ys = x_sample.reshape(dec_batch * dec_seq, D_MODEL)
    outs = [[] for _ in range(8)]
    gfin = g_final.reshape(1, D_MODEL)
    for l in range(depth):
        gmix = g_mix[l].reshape(1, D_MODEL)
        gffn = g_ffn[l].reshape(1, D_MODEL)
        w_in_bf = w_in[l].astype(BF16)
        wpool_bf = w_pool_grp[l].astype(BF16)
        pscale = pool_scale[l].reshape(1, POOL_WIDTH)
        wbp_bf = w_branch_pool[l].astype(BF16)
        wba_bf = w_branch_attn[l].astype(BF16)
        wout_bf = w_out[l].astype(BF16)
        wup_bf = w_up[l].astype(BF16)
        wdn_bf = w_down[l].astype(BF16)
        cb = conv_b[l].reshape(1, 2 * D_FF)

        u, q, k, v, gl = _in_proj(yp, gmix, w_in_bf, TM_IN)
        x1 = _mix_prompt(yp, u, q, k, v, gl, sinks[l], wpool_bf, pscale, wbp_bf, wba_bf, wout_bf,
                         batch, TQ_MIX)
        yp, conv_p = _ffn_prompt(x1, gffn, wup_bf, conv_w[l], cb, wdn_bf, gfin, batch, TM_FFN)
        outs[0].append(u.reshape(batch, seq, POOL_WIDTH)[:, seq - POOL_STATE:])
        outs[1].append(k.reshape(batch, seq, N_KV_HEADS, HEAD_DIM)[:, seq - WINDOW:])
        outs[2].append(v.reshape(batch, seq, N_KV_HEADS, HEAD_DIM)[:, seq - WINDOW:])
        outs[3].append(conv_p)

        us, qs, ks, vs, gls = _in_proj(ys, gmix, w_in_bf, TM_IN)
        u3 = us.reshape(dec_batch, dec_seq, POOL_WIDTH)
        stpad = jnp.pad(state_pool[l], ((0, 0), (1, 0), (0, 0)))
        ck = cache_k[l].reshape(dec_batch, WINDOW, KV_WIDTH)
        cv = cache_v[l].reshape(dec_batch, WINDOW, KV_WIDTH)
        x1s = _mix_sample(ys, u3, stpad, qs, ks, vs, ck, cv, gls, sinks[l], wpool_bf, pscale,
                          wbp_bf, wba_bf, wout_bf, GS_MIX, PAST_LEN)
        ys, conv_s = _ffn_sample(x1s, state_conv[l], gffn, wup_bf, conv_w[l], cb, wdn_bf, gfin,
                                   GS_FFN)
        outs[4].append(jnp.concatenate([state_pool[l], u3], axis=1)[:, dec_seq:])
        k3 = ks.reshape(dec_batch, dec_seq, N_KV_HEADS, HEAD_DIM)
        v3 = vs.reshape(dec_batch, dec_seq, N_KV_HEADS, HEAD_DIM)
        outs[5].append(jnp.concatenate([cache_k[l], k3], axis=1)[:, dec_seq:])
        outs[6].append(jnp.concatenate([cache_v[l], v3], axis=1)[:, dec_seq:])
        outs[7].append(conv_s)
    return (yp.reshape(batch, seq, D_MODEL), ys.reshape(dec_batch, dec_seq, D_MODEL),
            *[jnp.stack(o) for o in outs])
```

```python
import functools

import jax
import jax.numpy as jnp
from jax import lax
from jax.experimental import pallas as pl
from jax.experimental.pallas import tpu as pltpu

D_MODEL = 1024
POOL_WIDTH = 512
POOL_WINDOWS = (2, 4, 8, 16)
POOL_GROUP = 128
POOL_STATE = 15
HEAD_DIM = 64
N_HEADS = 8
N_KV_HEADS = 2
GQA_GROUP = N_HEADS // N_KV_HEADS
WINDOW = 128
ATTN_WIDTH = N_HEADS * HEAD_DIM
KV_WIDTH = N_KV_HEADS * HEAD_DIM
ATTN_SCALE = HEAD_DIM ** -0.5
D_FF = 2816
CONV_STATE = 2
EPS = 1e-6
PAST_LEN = 16384

LANES = 128
SUBLANES = 8
VMEM_LIMIT = 56 * 1024 * 1024

F32 = jnp.float32
BF16 = jnp.bfloat16


def _rmsnorm(x, g):
    inv = lax.rsqrt(jnp.mean(x * x, axis=-1, keepdims=True) + EPS)
    return (x * inv) * g


def _dot(a, b):
    return jnp.dot(a, b, preferred_element_type=F32)


def _const_spec(shape):
    nd = len(shape)
    return pl.BlockSpec(shape, lambda *_: (0,) * nd, pipeline_mode=pl.Buffered(1))


def _params(n_axes):
    return pltpu.CompilerParams(dimension_semantics=("arbitrary",) * n_axes,
                                vmem_limit_bytes=VMEM_LIMIT)


def _in_proj_kernel(x_ref, g_ref, w_ref, u_ref, q_ref, k_ref, v_ref, gl_ref):
    h = _rmsnorm(x_ref[...], g_ref[...]).astype(BF16)
    o_q = POOL_WIDTH
    o_k = o_q + ATTN_WIDTH
    o_gl = o_k + 2 * KV_WIDTH
    u_ref[...] = _dot(h, w_ref[:, 0:o_q])
    q_ref[...] = _dot(h, w_ref[:, o_q:o_k])
    kv = _dot(h, w_ref[:, o_k:o_gl])
    k_ref[...] = kv[:, :KV_WIDTH]
    v_ref[...] = kv[:, KV_WIDTH:]
    gl_ref[...] = _dot(h, w_ref[:, o_gl:])


def _in_proj(x, g_mix, w_in_bf, tm):
    n = x.shape[0]
    in_width = w_in_bf.shape[1]
    row = lambda c: pl.BlockSpec((tm, c), lambda i: (i, 0))
    return pl.pallas_call(
        _in_proj_kernel,
        grid=(n // tm,),
        in_specs=[row(D_MODEL), _const_spec((1, D_MODEL)), _const_spec((D_MODEL, in_width))],
        out_specs=[row(POOL_WIDTH), row(ATTN_WIDTH), row(KV_WIDTH), row(KV_WIDTH), row(2 * D_MODEL)],
        out_shape=[jax.ShapeDtypeStruct((n, POOL_WIDTH), F32),
                   jax.ShapeDtypeStruct((n, ATTN_WIDTH), F32),
                   jax.ShapeDtypeStruct((n, KV_WIDTH), F32),
                   jax.ShapeDtypeStruct((n, KV_WIDTH), F32),
                   jax.ShapeDtypeStruct((n, 2 * D_MODEL), F32)],
        compiler_params=_params(1),
        name="in_proj",
    )(x, g_mix, w_in_bf)


def _dup_kv_head(x, g):
    lane = lax.broadcasted_iota(jnp.int32, x.shape, 1)
    r = pltpu.roll(x, HEAD_DIM, axis=1)
    lo = lane < HEAD_DIM
    out = jnp.where(lo, x, r) if g == 0 else jnp.where(lo, r, x)
    return out.astype(BF16)


def _attend(q, kk, vv, bias4, sinks_ref):
    r = q.shape[0]
    lane = lax.broadcasted_iota(jnp.int32, (r, LANES), 1)
    lo = lane < HEAD_DIM
    scale_lo = jnp.where(lo, ATTN_SCALE, 0.0).astype(F32)
    scale_hi = jnp.where(lo, 0.0, ATTN_SCALE).astype(F32)
    row4 = lax.broadcasted_iota(jnp.int32, (GQA_GROUP * r, 1), 0)
    outs = []
    for g in range(N_KV_HEADS):
        kg = _dup_kv_head(kk, g)
        vg = _dup_kv_head(vv, g)
        base = g * GQA_GROUP * HEAD_DIM
        p0 = q[:, base:base + LANES]
        p1 = q[:, base + LANES:base + 2 * LANES]
        qs = jnp.concatenate([p0 * scale_lo, p0 * scale_hi, p1 * scale_lo, p1 * scale_hi],
                             axis=0).astype(BF16)
        s = lax.dot_general(qs, kg, (((1,), (1,)), ((), ())), preferred_element_type=F32) + bias4
        sk = jnp.full((GQA_GROUP * r, 1), sinks_ref[g * GQA_GROUP + GQA_GROUP - 1], F32)
        for hh in range(GQA_GROUP - 2, -1, -1):
            sk = jnp.where(row4 < (hh + 1) * r, sinks_ref[g * GQA_GROUP + hh], sk)
        m = jnp.maximum(jnp.max(s, axis=-1, keepdims=True), sk)
        e = jnp.exp(s - m)
        den = jnp.sum(e, axis=-1, keepdims=True) + jnp.exp(sk - m)
        p = (e * (1.0 / den)).astype(BF16)
        o = _dot(p, vg)
        outs.append(jnp.where(lo, o[0:r], o[r:2 * r]))
        outs.append(jnp.where(lo, o[2 * r:3 * r], o[3 * r:4 * r]))
    return jnp.concatenate(outs, axis=1)


def _band_bias4(r, s, first_key=0):
    i = lax.broadcasted_iota(jnp.int32, (GQA_GROUP * r, s), 0) & (r - 1)
    j = lax.broadcasted_iota(jnp.int32, (GQA_GROUP * r, s), 1)
    valid = (j >= jnp.maximum(i, first_key)) & (j <= i + WINDOW)
    return jnp.where(valid, 0.0, -jnp.inf).astype(F32)


def _window_sums(e, w):
    step = 1
    while step < w:
        e = e + pltpu.roll(e, step, axis=0)
        step *= 2
    return e


def _merge(x, pool_a, attn_o, gl, wbp_ref, wba_ref, wout_ref):
    pa = _dot(pool_a.astype(BF16), wbp_ref[...])
    po = _dot(attn_o.astype(BF16), wba_ref[...])
    ga = jax.nn.sigmoid(gl[:, :D_MODEL])
    gb = jax.nn.sigmoid(gl[:, D_MODEL:])
    mixed = (ga * pa + gb * po).astype(BF16)
    return x + _dot(mixed, wout_ref[...])


def _mix_prompt_kernel(sinks_ref, x_ref, u_ref, q_ref, k_ref, v_ref, gl_ref, wpool_ref, pscale_ref,
                       wbp_ref, wba_ref, wout_ref, o_ref, kt_ref, vt_ref, uext, kext, vext, oscr,
                       bias, *, tq):
    i = pl.program_id(1)
    pre = POOL_STATE + 1

    @pl.when(i == 0)
    def _():
        uext[0:pre, :] = jnp.zeros((pre, POOL_WIDTH), F32)
        kext[0:WINDOW, :] = jnp.zeros((WINDOW, KV_WIDTH), F32)
        vext[0:WINDOW, :] = jnp.zeros((WINDOW, KV_WIDTH), F32)
        bias[0] = _band_bias4(WINDOW, 2 * WINDOW, WINDOW)
        bias[1] = _band_bias4(WINDOW, 2 * WINDOW)

    uext[pre:pre + tq, :] = u_ref[...]
    kext[WINDOW:WINDOW + tq, :] = k_ref[...]
    vext[WINDOW:WINDOW + tq, :] = v_ref[...]

    pos1 = i * tq + lax.broadcasted_iota(jnp.int32, (tq, 1), 0) + 1
    a_parts = []
    for g, w in enumerate(POOL_WINDOWS):
        cols = slice(g * POOL_GROUP, (g + 1) * POOL_GROUP)
        e = uext[:, cols]
        inv_cnt = 1.0 / jnp.minimum(pos1, w).astype(F32)
        d = _window_sums(e, w)[pre:] * inv_cnt - e[pre:]
        y = _dot(d.astype(BF16), wpool_ref[g])
        a_parts.append(y * pscale_ref[:, cols])
    pool_a = jnp.concatenate(a_parts, axis=1)

    for nb in range(tq // WINDOW):
        slot = jnp.minimum(i * (tq // WINDOW) + nb, 1)
        rows = slice(nb * WINDOW, (nb + 1) * WINDOW)
        keys = slice(nb * WINDOW, (nb + 2) * WINDOW)
        oscr[rows, :] = _attend(q_ref[rows, :], kext[keys, :], vext[keys, :], bias[slot], sinks_ref)

    o_ref[...] = _merge(x_ref[...], pool_a, oscr[...], gl_ref[...], wbp_ref, wba_ref, wout_ref)

    @pl.when(i == pl.num_programs(1) - 1)
    def _():
        kt_ref[0] = kext[tq:tq + WINDOW, :].T
        vt_ref[0] = vext[tq:tq + WINDOW, :].T

    uext[0:pre, :] = uext[tq:tq + pre, :]
    kext[0:WINDOW, :] = kext[tq:tq + WINDOW, :]
    vext[0:WINDOW, :] = vext[tq:tq + WINDOW, :]


def _mix_prompt(x, u, q, k, v, gl, sinks, wpool_bf, pscale, wbp_bf, wba_bf, wout_bf, batch, tq):
    n = x.shape[0]
    nt = n // batch // tq
    row = lambda c: pl.BlockSpec((tq, c), lambda b, i: (b * nt + i, 0))
    last_t = pl.BlockSpec((1, KV_WIDTH, WINDOW), lambda b, i: (b, 0, 0))
    return pl.pallas_call(
        functools.partial(_mix_prompt_kernel, tq=tq),
        grid=(batch, nt),
        in_specs=[pl.BlockSpec(memory_space=pltpu.SMEM),
                  row(D_MODEL), row(POOL_WIDTH), row(ATTN_WIDTH), row(KV_WIDTH), row(KV_WIDTH),
                  row(2 * D_MODEL),
                  _const_spec(wpool_bf.shape), _const_spec(pscale.shape),
                  _const_spec(wbp_bf.shape), _const_spec(wba_bf.shape), _const_spec(wout_bf.shape)],
        out_specs=[row(D_MODEL), last_t, last_t],
        out_shape=[jax.ShapeDtypeStruct((n, D_MODEL), F32),
                   jax.ShapeDtypeStruct((batch, KV_WIDTH, WINDOW), F32),
                   jax.ShapeDtypeStruct((batch, KV_WIDTH, WINDOW), F32)],
        scratch_shapes=[pltpu.VMEM((POOL_STATE + 1 + tq, POOL_WIDTH), F32),
                        pltpu.VMEM((WINDOW + tq, KV_WIDTH), F32),
                        pltpu.VMEM((WINDOW + tq, KV_WIDTH), F32),
                        pltpu.VMEM((tq, ATTN_WIDTH), F32),
                        pltpu.VMEM((2, GQA_GROUP * WINDOW, 2 * WINDOW), F32)],
        compiler_params=_params(2),
        name="mix_prompt",
    )(sinks, x, u, q, k, v, gl, wpool_bf, pscale, wbp_bf, wba_bf, wout_bf)


def _tile_shift(cur, prev, j, row):
    return jnp.where(row < j, pltpu.roll(prev, j, axis=1), pltpu.roll(cur, j, axis=1))


SEQ_BLOCK = WINDOW // SUBLANES


def _sample_bias4():
    shape = (GQA_GROUP * WINDOW, 2 * WINDOW)
    i = lax.broadcasted_iota(jnp.int32, shape, 0) & (WINDOW - 1)
    j = lax.broadcasted_iota(jnp.int32, shape, 1)
    b, t = i >> 3, i & (SUBLANES - 1)
    new = j - WINDOW
    valid = ((j < WINDOW) & (j >= t)) | ((new >> 3 == b) & ((new & (SUBLANES - 1)) <= t) & (new >= 0))
    return jnp.where(valid, 0.0, -jnp.inf).astype(F32)


def _attend_cached(q, knew, vnew, ckt_ref, cvt_ref, b0, bias4, sinks_ref):
    r, t = WINDOW, SUBLANES
    lane = lax.broadcasted_iota(jnp.int32, (r, LANES), 1)
    lo = lane < HEAD_DIM
    scale_lo = jnp.where(lo, ATTN_SCALE, 0.0).astype(F32)
    scale_hi = jnp.where(lo, 0.0, ATTN_SCALE).astype(F32)
    row4 = lax.broadcasted_iota(jnp.int32, (GQA_GROUP * r, 1), 0)
    nt_dims = (((1,), (1,)), ((), ()))

    def per_seq(x):
        return [jnp.concatenate([x[j * r + b * t:j * r + (b + 1) * t] for j in range(GQA_GROUP)],
                                axis=0) for b in range(SEQ_BLOCK)]

    def stacked(xs):
        return jnp.concatenate([xs[b][j * t:(j + 1) * t] for j in range(GQA_GROUP)
                                for b in range(SEQ_BLOCK)], axis=0)

    outs = []
    for g in range(N_KV_HEADS):
        base = g * GQA_GROUP * HEAD_DIM
        kv_rows = slice(g * HEAD_DIM, (g + 1) * HEAD_DIM)
        p0 = q[:, base:base + LANES]
        p1 = q[:, base + LANES:base + 2 * LANES]
        qs = jnp.concatenate([p0 * scale_lo, p0 * scale_hi, p1 * scale_lo, p1 * scale_hi], axis=0)
        s_new = lax.dot_general(qs.astype(BF16), _dup_kv_head(knew, g), nt_dims,
                                preferred_element_type=F32)
        s_cache = []
        for b, qb in enumerate(per_seq(qs)):
            kt = ckt_ref[b0 + b, kv_rows, :]
            s_cache.append(_dot(qb.astype(BF16), jnp.concatenate([kt, kt], axis=0).astype(BF16)))
        s = jnp.concatenate([stacked(s_cache), s_new], axis=1) + bias4
        sk = jnp.full((GQA_GROUP * r, 1), sinks_ref[g * GQA_GROUP + GQA_GROUP - 1], F32)
        for hh in range(GQA_GROUP - 2, -1, -1):
            sk = jnp.where(row4 < (hh + 1) * r, sinks_ref[g * GQA_GROUP + hh], sk)
        m = jnp.maximum(jnp.max(s, axis=-1, keepdims=True), sk)
        e = jnp.exp(s - m)
        den = jnp.sum(e, axis=-1, keepdims=True) + jnp.exp(sk - m)
        p = e * (1.0 / den)
        o_cache = []
        for b, pb in enumerate(per_seq(p[:, :r])):
            vt = cvt_ref[b0 + b, kv_rows, :]
            o_cache.append(lax.dot_general(pb.astype(BF16),
                                           jnp.concatenate([vt, vt], axis=0).astype(BF16),
                                           nt_dims, preferred_element_type=F32))
        o = stacked(o_cache) + _dot(p[:, r:].astype(BF16), _dup_kv_head(vnew, g))
        outs.append(jnp.where(lo, o[0:r], o[r:2 * r]))
        outs.append(jnp.where(lo, o[2 * r:3 * r], o[3 * r:4 * r]))
    return jnp.concatenate(outs, axis=1)


def _mix_sample_kernel(sinks_ref, x_ref, u_ref, st_ref, q_ref, k_ref, v_ref, ckt_ref, cvt_ref, gl_ref,
                       wpool_ref, pscale_ref, wbp_ref, wba_ref, wout_ref, o_ref, oscr, *, gs, pos0):
    t = SUBLANES
    rows = gs * t

    row = lax.broadcasted_iota(jnp.int32, (gs, t, POOL_GROUP), 1)
    cnt_pos = pos0 + lax.broadcasted_iota(jnp.int32, (gs, t, 1), 1) + 1
    a_parts = []
    for g, w in enumerate(POOL_WINDOWS):
        cols = slice(g * POOL_GROUP, (g + 1) * POOL_GROUP)
        tiles = [st_ref[:, 0:t, cols], st_ref[:, t:2 * t, cols], u_ref[:, :, cols]]
        xt = tiles[2]
        step = 1
        while step < w and step < t:
            tiles = [tiles[n] + _tile_shift(tiles[n], tiles[max(n - 1, 0)], step, row)
                     for n in range(3)]
            step *= 2
        acc = tiles[2] + tiles[1] if w == 2 * t else tiles[2]
        cnt = jnp.minimum(cnt_pos, w).astype(F32)
        d = (acc / cnt - xt).reshape(rows, POOL_GROUP)
        y = _dot(d.astype(BF16), wpool_ref[g])
        a_parts.append(y * pscale_ref[:, cols])
    pool_a = jnp.concatenate(a_parts, axis=1)

    bias = _sample_bias4()
    for blk in range(gs // SEQ_BLOCK):
        rs = slice(blk * WINDOW, (blk + 1) * WINDOW)
        oscr[rs, :] = _attend_cached(q_ref[rs, :], k_ref[rs, :], v_ref[rs, :], ckt_ref, cvt_ref,
                                     blk * SEQ_BLOCK, bias, sinks_ref)

    o_ref[...] = _merge(x_ref[...], pool_a, oscr[...], gl_ref[...], wbp_ref, wba_ref, wout_ref)


def _mix_sample(x, u3, stpad, q, k, v, ck, cv, gl, sinks, wpool_bf, pscale, wbp_bf, wba_bf, wout_bf,
                gs, pos0):
    nseq, t = u3.shape[0], u3.shape[1]
    rows = gs * t
    row = lambda c: pl.BlockSpec((rows, c), lambda i: (i, 0))
    seq = lambda a, b: pl.BlockSpec((gs, a, b), lambda i: (i, 0, 0))
    return pl.pallas_call(
        functools.partial(_mix_sample_kernel, gs=gs, pos0=pos0),
        grid=(nseq // gs,),
        in_specs=[pl.BlockSpec(memory_space=pltpu.SMEM),
                  row(D_MODEL), seq(t, POOL_WIDTH), seq(2 * t, POOL_WIDTH),
                  row(ATTN_WIDTH), row(KV_WIDTH), row(KV_WIDTH),
                  seq(WINDOW, KV_WIDTH), seq(WINDOW, KV_WIDTH), row(2 * D_MODEL),
                  _const_spec(wpool_bf.shape), _const_spec(pscale.shape),
                  _const_spec(wbp_bf.shape), _const_spec(wba_bf.shape), _const_spec(wout_bf.shape)],
        out_specs=row(D_MODEL),
        out_shape=jax.ShapeDtypeStruct((nseq * t, D_MODEL), F32),
        scratch_shapes=[pltpu.VMEM((rows, ATTN_WIDTH), F32)],
        compiler_params=_params(1),
        name="mix_sample",
    )(sinks, x, u3, stpad, q, k, v, ck, cv, gl, wpool_bf, pscale, wbp_bf, wba_bf, wout_bf)


FF_CHUNK = 256


def _gated_act(taps, cw_ref, cb_ref, c0):
    def conv(cols):
        x0, x1, x2 = taps(cols)
        return (x0 * cw_ref[0:1, cols] + x1 * cw_ref[1:2, cols] + x2 * cw_ref[2:3, cols]
                + cb_ref[:, cols])
    gate = conv(slice(c0, c0 + FF_CHUNK))
    val = conv(slice(D_FF + c0, D_FF + c0 + FF_CHUNK))
    return (jax.nn.silu(gate) * val).astype(BF16)


def _ffn_prompt_kernel(x_ref, g_ref, wup_ref, cw_ref, cb_ref, wdn_ref, gf_ref, o_ref, cs_ref,
                       upx, act, *, tm):
    i = pl.program_id(1)
    pre = SUBLANES

    @pl.when(i == 0)
    def _():
        upx[0:pre, :] = jnp.zeros((pre, 2 * D_FF), F32)

    x = x_ref[...]
    h = _rmsnorm(x, g_ref[...]).astype(BF16)
    upx[pre:pre + tm, :] = _dot(h, wup_ref[...])

    def taps(cols):
        e = upx[:, cols]
        return pltpu.roll(e, 2, axis=0)[pre:], pltpu.roll(e, 1, axis=0)[pre:], e[pre:]

    for c0 in range(0, D_FF, FF_CHUNK):
        act[:, c0:c0 + FF_CHUNK] = _gated_act(taps, cw_ref, cb_ref, c0)

    y = x + _dot(act[...], wdn_ref[...])
    o_ref[...] = _rmsnorm(y, gf_ref[...])
    cs_ref[0] = upx[pre + tm - CONV_STATE:pre + tm, :]
    upx[0:pre, :] = upx[tm:tm + pre, :]


def _ffn_prompt(x, g_ffn, wup_bf, conv_w, conv_b, wdn_bf, g_final, batch, tm):
    n = x.shape[0]
    nt = n // batch // tm
    row = pl.BlockSpec((tm, D_MODEL), lambda b, i: (b * nt + i, 0))
    return pl.pallas_call(
        functools.partial(_ffn_prompt_kernel, tm=tm),
        grid=(batch, nt),
        in_specs=[row, _const_spec((1, D_MODEL)), _const_spec(wup_bf.shape),
                  _const_spec(conv_w.shape), _const_spec(conv_b.shape), _const_spec(wdn_bf.shape),
                  _const_spec((1, D_MODEL))],
        out_specs=[row, pl.BlockSpec((1, CONV_STATE, 2 * D_FF), lambda b, i: (b, 0, 0))],
        out_shape=[jax.ShapeDtypeStruct((n, D_MODEL), F32),
                   jax.ShapeDtypeStruct((batch, CONV_STATE, 2 * D_FF), F32)],
        scratch_shapes=[pltpu.VMEM((SUBLANES + tm, 2 * D_FF), F32),
                        pltpu.VMEM((tm, D_FF), BF16)],
        compiler_params=_params(2),
        name="ffn_prompt",
    )(x, g_ffn, wup_bf, conv_w, conv_b, wdn_bf, g_final)


def _ffn_sample_kernel(x_ref, st_ref, g_ref, wup_ref, cw_ref, cb_ref, wdn_ref, gf_ref,
                       o_ref, cs_ref, up, act, *, gs):
    t = SUBLANES
    rows = gs * t
    x = x_ref[...]
    h = _rmsnorm(x, g_ref[...]).astype(BF16)
    up[...] = _dot(h, wup_ref[...])
    row = lax.broadcasted_iota(jnp.int32, (gs, t, FF_CHUNK), 1)

    def taps(cols):
        x2 = up[:, cols].reshape(gs, t, FF_CHUNK)
        s0 = jnp.broadcast_to(st_ref[:, 0:1, cols], x2.shape)
        s1 = jnp.broadcast_to(st_ref[:, 1:2, cols], x2.shape)
        x1 = jnp.where(row < 1, s1, pltpu.roll(x2, 1, axis=1))
        x0 = jnp.where(row < 1, s0, jnp.where(row < 2, s1, pltpu.roll(x2, 2, axis=1)))
        flat = lambda a: a.reshape(rows, FF_CHUNK)
        return flat(x0), flat(x1), flat(x2)

    for c0 in range(0, D_FF, FF_CHUNK):
        act[:, c0:c0 + FF_CHUNK] = _gated_act(taps, cw_ref, cb_ref, c0)

    y = x + _dot(act[...], wdn_ref[...])
    o_ref[...] = _rmsnorm(y, gf_ref[...])
    for c0 in range(0, 2 * D_FF, 4 * FF_CHUNK):
        cols = slice(c0, min(c0 + 4 * FF_CHUNK, 2 * D_FF))
        cs_ref[:, :, cols] = up[:, cols].reshape(gs, t, -1)[:, t - CONV_STATE:, :]


def _ffn_sample(x, st, g_ffn, wup_bf, conv_w, conv_b, wdn_bf, g_final, gs):
    nseq = st.shape[0]
    t = x.shape[0] // nseq
    rows = gs * t
    row = pl.BlockSpec((rows, D_MODEL), lambda i: (i, 0))
    st_spec = pl.BlockSpec((gs, CONV_STATE, 2 * D_FF), lambda i: (i, 0, 0))
    return pl.pallas_call(
        functools.partial(_ffn_sample_kernel, gs=gs),
        grid=(nseq // gs,),
        in_specs=[row, st_spec,
                  _const_spec((1, D_MODEL)), _const_spec(wup_bf.shape),
                  _const_spec(conv_w.shape), _const_spec(conv_b.shape), _const_spec(wdn_bf.shape),
                  _const_spec((1, D_MODEL))],
        out_specs=[row, st_spec],
        out_shape=[jax.ShapeDtypeStruct((nseq * t, D_MODEL), F32),
                   jax.ShapeDtypeStruct((nseq, CONV_STATE, 2 * D_FF), F32)],
        scratch_shapes=[pltpu.VMEM((rows, 2 * D_FF), F32), pltpu.VMEM((rows, D_FF), BF16)],
        compiler_params=_params(1),
        name="ffn_sample",
    )(x, st, g_ffn, wup_bf, conv_w, conv_b, wdn_bf, g_final)


TM_IN = 1024
TQ_MIX = 512
TM_FFN = 512
GS_MIX = 32
GS_FFN = 64


def kernel(x_prompt, x_sample, state_pool, cache_k, cache_v, state_conv, g_mix, w_in, w_pool_grp,
           pool_scale, sinks, w_branch_pool, w_branch_attn, w_out, g_ffn, w_up, conv_w, conv_b,
           w_down, g_final):
    depth = g_mix.shape[0]
    batch, seq, _ = x_prompt.shape
    dec_batch, dec_seq, _ = x_sample.shape
    assert dec_seq == SUBLANES and seq % TQ_MIX == 0 and seq % TM_FFN == 0
    assert depth == 1

    yp = x_prompt.reshape(batch * seq, D_MODEL)
    ys = x_sample.reshape(dec_batch * dec_seq, D_MODEL)
    outs = [[] for _ in range(8)]
    gfin = g_final.reshape(1, D_MODEL)
    for l in range(depth):
        gmix = g_mix[l].reshape(1, D_MODEL)
        gffn = g_ffn[l].reshape(1, D_MODEL)
        w_in_bf = w_in[l].astype(BF16)
        wpool_bf = w_pool_grp[l].astype(BF16)
        pscale = pool_scale[l].reshape(1, POOL_WIDTH)
        wbp_bf = w_branch_pool[l].astype(BF16)
        wba_bf = w_branch_attn[l].astype(BF16)
        wout_bf = w_out[l].astype(BF16)
        wup_bf = w_up[l].astype(BF16)
        wdn_bf = w_down[l].astype(BF16)
        cb = conv_b[l].reshape(1, 2 * D_FF)

        u, q, k, v, gl = _in_proj(yp, gmix, w_in_bf, TM_IN)
        x1, kt, vt = _mix_prompt(yp, u, q, k, v, gl, sinks[l], wpool_bf, pscale, wbp_bf, wba_bf,
                                 wout_bf, batch, TQ_MIX)
        yp, conv_p = _ffn_prompt(x1, gffn, wup_bf, conv_w[l], cb, wdn_bf, gfin, batch, TM_FFN)
        outs[0].append(u.reshape(batch, seq, POOL_WIDTH)[:, seq - POOL_STATE:])
        untranspose = lambda t: t.reshape(-1, N_KV_HEADS, HEAD_DIM, WINDOW).transpose(0, 3, 1, 2)
        outs[1].append(untranspose(kt))
        outs[2].append(untranspose(vt))
        outs[3].append(conv_p)

        us, qs, ks, vs, gls = _in_proj(ys, gmix, w_in_bf, TM_IN)
        u3 = us.reshape(dec_batch, dec_seq, POOL_WIDTH)
        stpad = jnp.pad(state_pool[l], ((0, 0), (1, 0), (0, 0)))
        transposed = lambda c: c.transpose(0, 2, 3, 1).reshape(dec_batch, KV_WIDTH, WINDOW)
        ck, cv = transposed(cache_k[l]), transposed(cache_v[l])
        x1s = _mix_sample(ys, u3, stpad, qs, ks, vs, ck, cv, gls, sinks[l], wpool_bf, pscale,
                          wbp_bf, wba_bf, wout_bf, GS_MIX, PAST_LEN)
        ys, conv_s = _ffn_sample(x1s, state_conv[l], gffn, wup_bf, conv_w[l], cb, wdn_bf, gfin,
                                   GS_FFN)
        outs[4].append(jnp.concatenate([state_pool[l], u3], axis=1)[:, dec_seq:])
        k3 = ks.reshape(dec_batch, dec_seq, N_KV_HEADS, HEAD_DIM)
        v3 = vs.reshape(dec_batch, dec_seq, N_KV_HEADS, HEAD_DIM)
        outs[5].append(jnp.concatenate([cache_k[l], k3], axis=1)[:, dec_seq:])
        outs[6].append(jnp.concatenate([cache_v[l], v3], axis=1)[:, dec_seq:])
        outs[7].append(conv_s)
    return (yp.reshape(batch, seq, D_MODEL), ys.reshape(dec_batch, dec_seq, D_MODEL),
            *[jnp.stack(o) for o in outs])
```

```python
import functools

import jax
import jax.numpy as jnp
from jax import lax
from jax.experimental import pallas as pl
from jax.experimental.pallas import tpu as pltpu

D_MODEL = 1024
POOL_WIDTH = 512
POOL_WINDOWS = (2, 4, 8, 16)
POOL_GROUP = 128
POOL_STATE = 15
HEAD_DIM = 64
N_HEADS = 8
N_KV_HEADS = 2
GQA_GROUP = N_HEADS // N_KV_HEADS
WINDOW = 128
ATTN_WIDTH = N_HEADS * HEAD_DIM
KV_WIDTH = N_KV_HEADS * HEAD_DIM
ATTN_SCALE = HEAD_DIM ** -0.5
D_FF = 2816
CONV_STATE = 2
EPS = 1e-6
PAST_LEN = 16384

LANES = 128
SUBLANES = 8
VMEM_LIMIT = 56 * 1024 * 1024

F32 = jnp.float32
BF16 = jnp.bfloat16


def _rmsnorm(x, g):
    inv = lax.rsqrt(jnp.mean(x * x, axis=-1, keepdims=True) + EPS)
    return (x * inv) * g


def _dot(a, b):
    return jnp.dot(a, b, preferred_element_type=F32)


def _const_spec(shape):
    nd = len(shape)
    return pl.BlockSpec(shape, lambda *_: (0,) * nd, pipeline_mode=pl.Buffered(1))


def _params(n_axes):
    return pltpu.CompilerParams(dimension_semantics=("arbitrary",) * n_axes,
                                vmem_limit_bytes=VMEM_LIMIT)


def _in_proj_kernel(x_ref, g_ref, w_ref, u_ref, q_ref, k_ref, v_ref, gl_ref):
    h = _rmsnorm(x_ref[...], g_ref[...]).astype(BF16)
    o_q = POOL_WIDTH
    o_k = o_q + ATTN_WIDTH
    o_gl = o_k + 2 * KV_WIDTH
    u_ref[...] = _dot(h, w_ref[:, 0:o_q])
    q_ref[...] = _dot(h, w_ref[:, o_q:o_k])
    kv = _dot(h, w_ref[:, o_k:o_gl])
    k_ref[...] = kv[:, :KV_WIDTH]
    v_ref[...] = kv[:, KV_WIDTH:]
    gl_ref[...] = _dot(h, w_ref[:, o_gl:])


def _in_proj(x, g_mix, w_in_bf, tm):
    n = x.shape[0]
    in_width = w_in_bf.shape[1]
    row = lambda c: pl.BlockSpec((tm, c), lambda i: (i, 0))
    return pl.pallas_call(
        _in_proj_kernel,
        grid=(n // tm,),
        in_specs=[row(D_MODEL), _const_spec((1, D_MODEL)), _const_spec((D_MODEL, in_width))],
        out_specs=[row(POOL_WIDTH), row(ATTN_WIDTH), row(KV_WIDTH), row(KV_WIDTH), row(2 * D_MODEL)],
        out_shape=[jax.ShapeDtypeStruct((n, POOL_WIDTH), F32),
                   jax.ShapeDtypeStruct((n, ATTN_WIDTH), F32),
                   jax.ShapeDtypeStruct((n, KV_WIDTH), F32),
                   jax.ShapeDtypeStruct((n, KV_WIDTH), F32),
                   jax.ShapeDtypeStruct((n, 2 * D_MODEL), F32)],
        compiler_params=_params(1),
        name="in_proj",
    )(x, g_mix, w_in_bf)


def _dup_kv_heads(x):
    lo = lax.broadcasted_iota(jnp.int32, x.shape, 1) < HEAD_DIM
    r = pltpu.roll(x, HEAD_DIM, axis=1)
    return [jnp.where(lo, x, r).astype(BF16), jnp.where(lo, r, x).astype(BF16)]


def _stack_heads(q, g):
    lo = lax.broadcasted_iota(jnp.int32, (q.shape[0], LANES), 1) < HEAD_DIM
    scale_lo = jnp.where(lo, ATTN_SCALE, 0.0).astype(F32)
    scale_hi = jnp.where(lo, 0.0, ATTN_SCALE).astype(F32)
    base = g * GQA_GROUP * HEAD_DIM
    p0 = q[:, base:base + LANES]
    p1 = q[:, base + LANES:base + 2 * LANES]
    return jnp.concatenate([p0 * scale_lo, p0 * scale_hi, p1 * scale_lo, p1 * scale_hi], axis=0)


def _attn_scores(q, kdup, g, bias4):
    return lax.dot_general(_stack_heads(q, g).astype(BF16), kdup, (((1,), (1,)), ((), ())),
                           preferred_element_type=F32) + bias4


def _sink_column(r, g, sinks_ref):
    row4 = lax.broadcasted_iota(jnp.int32, (GQA_GROUP * r, 1), 0)
    sk = jnp.full((GQA_GROUP * r, 1), sinks_ref[g * GQA_GROUP + GQA_GROUP - 1], F32)
    for hh in range(GQA_GROUP - 2, -1, -1):
        sk = jnp.where(row4 < (hh + 1) * r, sinks_ref[g * GQA_GROUP + hh], sk)
    return sk


def _sink_softmax(s, sk):
    m = jnp.maximum(jnp.max(s, axis=-1, keepdims=True), sk)
    e = jnp.exp(s - m)
    den = jnp.sum(e, axis=-1, keepdims=True) + jnp.exp(sk - m)
    return e * (1.0 / den)


def _unstack_heads(o):
    r = o.shape[0] // GQA_GROUP
    lo = lax.broadcasted_iota(jnp.int32, (r, LANES), 1) < HEAD_DIM
    return jnp.concatenate([jnp.where(lo, o[0:r], o[r:2 * r]),
                            jnp.where(lo, o[2 * r:3 * r], o[3 * r:4 * r])], axis=1)


def _software_pipeline(n_units, stages, after_tick=()):
    vals = {}
    for tick in range(n_units + len(stages) - 1):
        for k in reversed(range(len(stages))):
            u = tick - k
            if 0 <= u < n_units:
                vals[u] = stages[k](u, vals.get(u))
        if tick < len(after_tick) and after_tick[tick] is not None:
            after_tick[tick]()


def _band_bias4(r, s, first_key=0):
    i = lax.broadcasted_iota(jnp.int32, (GQA_GROUP * r, s), 0) & (r - 1)
    j = lax.broadcasted_iota(jnp.int32, (GQA_GROUP * r, s), 1)
    valid = (j >= jnp.maximum(i, first_key)) & (j <= i + WINDOW)
    return jnp.where(valid, 0.0, -jnp.inf).astype(F32)


def _window_sums(e, w):
    step = 1
    while step < w:
        e = e + pltpu.roll(e, step, axis=0)
        step *= 2
    return e


def _gate_merge(pa, po, gl):
    ga = jax.nn.sigmoid(gl[:, :D_MODEL])
    gb = jax.nn.sigmoid(gl[:, D_MODEL:])
    return (ga * pa + gb * po).astype(BF16)


def _merge(x, pool_a, attn_o, gl, wbp_ref, wba_ref, wout_ref):
    pa = _dot(pool_a.astype(BF16), wbp_ref[...])
    po = _dot(attn_o.astype(BF16), wba_ref[...])
    return x + _dot(_gate_merge(pa, po, gl), wout_ref[...])


def _mix_prompt_kernel(sinks_ref, u_ref, q_ref, k_ref, v_ref, gl_ref, wpool_ref, pscale_ref,
                       wbp_ref, wba_ref, wout_ref, xo_ref, o_ref, kt_ref, vt_ref, uext, kdup, vdup,
                       oscr, bias, pa_scr, mixed, *, tq, nt):
    s = pl.program_id(0)
    i = s % nt
    pre = POOL_STATE + 1

    @pl.when(s == 0)
    def _():
        mixed[...] = jnp.zeros(mixed.shape, BF16)

    @pl.when(i == 0)
    def _():
        uext[0:pre, :] = jnp.zeros((pre, POOL_WIDTH), F32)
        kdup[:, 0:WINDOW, :] = jnp.zeros((N_KV_HEADS, WINDOW, LANES), BF16)
        vdup[:, 0:WINDOW, :] = jnp.zeros((N_KV_HEADS, WINDOW, LANES), BF16)
        bias[0] = _band_bias4(WINDOW, 2 * WINDOW, WINDOW)
        bias[1] = _band_bias4(WINDOW, 2 * WINDOW)

    uext[pre:pre + tq, :] = u_ref[...]
    for dst, src in ((kdup, k_ref), (vdup, v_ref)):
        for g, d in enumerate(_dup_kv_heads(src[...])):
            dst[g, WINDOW:WINDOW + tq, :] = d
    sink_cols = [_sink_column(WINDOW, g, sinks_ref) for g in range(N_KV_HEADS)]

    piece = D_MODEL // 4

    def out_prev(n):
        cols = slice(n * piece, (n + 1) * piece)
        o_ref[:, cols] = xo_ref[:, cols] + _dot(mixed[...], wout_ref[:, cols])

    pos1 = i * tq + lax.broadcasted_iota(jnp.int32, (tq, 1), 0) + 1
    a_parts = []
    for g, w in enumerate(POOL_WINDOWS):
        cols = slice(g * POOL_GROUP, (g + 1) * POOL_GROUP)
        e = uext[:, cols]
        inv_cnt = 1.0 / jnp.minimum(pos1, w).astype(F32)
        d = _window_sums(e, w)[pre:] * inv_cnt - e[pre:]
        y = _dot(d.astype(BF16), wpool_ref[g])
        a_parts.append(y * pscale_ref[:, cols])
    pool_a = jnp.concatenate(a_parts, axis=1).astype(BF16)

    n_units = N_KV_HEADS * tq // WINDOW

    def pooled_branch(n):
        cols = slice(n * 2 * piece, (n + 1) * 2 * piece)
        pa_scr[:, cols] = _dot(pool_a, wbp_ref[:, cols])

    def merge_half(half):
        rows = slice(half * tq // 2, (half + 1) * tq // 2)
        po = _dot(oscr[rows, :].astype(BF16), wba_ref[...])
        mixed[rows, :] = _gate_merge(pa_scr[rows, :], po, gl_ref[rows, :])

    half_done = n_units // 2 + 1
    after_tick = [functools.partial(out_prev, n) for n in range(4)]
    after_tick += [functools.partial(pooled_branch, n) for n in range(2)]
    assert len(after_tick) <= half_done + 1
    after_tick += [None] * (half_done + 1 - len(after_tick)) + [functools.partial(merge_half, 0)]

    def unit(u):
        nb, g = divmod(u, N_KV_HEADS)
        return nb, g, slice(nb * WINDOW, (nb + 1) * WINDOW), slice(nb * WINDOW, (nb + 2) * WINDOW)

    def scores(u, _):
        nb, g, rows, keys = unit(u)
        slot = jnp.minimum(i * (tq // WINDOW) + nb, 1)
        return _attn_scores(q_ref[rows, :], kdup[g, keys, :], g, bias[slot])

    def softmax(u, s):
        return _sink_softmax(s, sink_cols[unit(u)[1]]).astype(BF16)

    def values(u, p):
        nb, g, rows, keys = unit(u)
        oscr[rows, g * 2 * LANES:(g + 1) * 2 * LANES] = _unstack_heads(_dot(p, vdup[g, keys, :]))

    _software_pipeline(n_units, [scores, softmax, values], after_tick)
    merge_half(1)

    @pl.when(i == nt - 1)
    def _():
        kt_ref[0] = k_ref[tq - WINDOW:tq, :].T
        vt_ref[0] = v_ref[tq - WINDOW:tq, :].T

    uext[0:pre, :] = uext[tq:tq + pre, :]
    kdup[:, 0:WINDOW, :] = kdup[:, tq:tq + WINDOW, :]
    vdup[:, 0:WINDOW, :] = vdup[:, tq:tq + WINDOW, :]


def _mix_prompt(x, u, q, k, v, gl, sinks, wpool_bf, pscale, wbp_bf, wba_bf, wout_bf, batch, tq):
    n = x.shape[0]
    n_tiles = n // tq
    nt = n_tiles // batch
    row = lambda c: pl.BlockSpec((tq, c), lambda s: (jnp.minimum(s, n_tiles - 1), 0))
    row_prev = pl.BlockSpec((tq, D_MODEL), lambda s: (jnp.maximum(s - 1, 0), 0))
    last_t = pl.BlockSpec((1, KV_WIDTH, WINDOW),
                          lambda s: (jnp.minimum(s, n_tiles - 1) // nt, 0, 0))
    return pl.pallas_call(
        functools.partial(_mix_prompt_kernel, tq=tq, nt=nt),
        grid=(n_tiles + 1,),
        in_specs=[pl.BlockSpec(memory_space=pltpu.SMEM),
                  row(POOL_WIDTH), row(ATTN_WIDTH), row(KV_WIDTH), row(KV_WIDTH), row(2 * D_MODEL),
                  _const_spec(wpool_bf.shape), _const_spec(pscale.shape),
                  _const_spec(wbp_bf.shape), _const_spec(wba_bf.shape), _const_spec(wout_bf.shape),
                  row_prev],
        out_specs=[row_prev, last_t, last_t],
        out_shape=[jax.ShapeDtypeStruct((n, D_MODEL), F32),
                   jax.ShapeDtypeStruct((batch, KV_WIDTH, WINDOW), F32),
                   jax.ShapeDtypeStruct((batch, KV_WIDTH, WINDOW), F32)],
        scratch_shapes=[pltpu.VMEM((POOL_STATE + 1 + tq, POOL_WIDTH), F32),
                        pltpu.VMEM((N_KV_HEADS, WINDOW + tq, LANES), BF16),
                        pltpu.VMEM((N_KV_HEADS, WINDOW + tq, LANES), BF16),
                        pltpu.VMEM((tq, ATTN_WIDTH), F32),
                        pltpu.VMEM((2, GQA_GROUP * WINDOW, 2 * WINDOW), F32),
                        pltpu.VMEM((tq, D_MODEL), F32),
                        pltpu.VMEM((tq, D_MODEL), BF16)],
        compiler_params=_params(1),
        name="mix_prompt",
    )(sinks, u, q, k, v, gl, wpool_bf, pscale, wbp_bf, wba_bf, wout_bf, x)


def _tile_shift(cur, prev, j, row):
    return jnp.where(row < j, pltpu.roll(prev, j, axis=1), pltpu.roll(cur, j, axis=1))


SEQ_BLOCK = WINDOW // SUBLANES


def _sample_bias4():
    shape = (GQA_GROUP * WINDOW, 2 * WINDOW)
    i = lax.broadcasted_iota(jnp.int32, shape, 0) & (WINDOW - 1)
    j = lax.broadcasted_iota(jnp.int32, shape, 1)
    b, t = i >> 3, i & (SUBLANES - 1)
    new = j - WINDOW
    valid = ((j < WINDOW) & (j >= t)) | ((new >> 3 == b) & ((new & (SUBLANES - 1)) <= t) & (new >= 0))
    return jnp.where(valid, 0.0, -jnp.inf).astype(F32)


def _attend_cached(q, knew, vnew, ckt_ref, cvt_ref, b0, bias4, sinks_ref):
    r, t = WINDOW, SUBLANES
    nt_dims = (((1,), (1,)), ((), ()))
    kn, vn = _dup_kv_heads(knew), _dup_kv_heads(vnew)

    def per_seq(x):
        return [jnp.concatenate([x[j * r + b * t:j * r + (b + 1) * t] for j in range(GQA_GROUP)],
                                axis=0) for b in range(SEQ_BLOCK)]

    def stacked(xs):
        return jnp.concatenate([xs[b][j * t:(j + 1) * t] for j in range(GQA_GROUP)
                                for b in range(SEQ_BLOCK)], axis=0)

    outs = []
    for g in range(N_KV_HEADS):
        kv_rows = slice(g * HEAD_DIM, (g + 1) * HEAD_DIM)
        qs = _stack_heads(q, g)
        s_new = lax.dot_general(qs.astype(BF16), kn[g], nt_dims, preferred_element_type=F32)
        s_cache = []
        for b, qb in enumerate(per_seq(qs)):
            kt = ckt_ref[b0 + b, kv_rows, :]
            s_cache.append(_dot(qb.astype(BF16), jnp.concatenate([kt, kt], axis=0).astype(BF16)))
        s = jnp.concatenate([stacked(s_cache), s_new], axis=1) + bias4
        p = _sink_softmax(s, _sink_column(r, g, sinks_ref))
        o_cache = []
        for b, pb in enumerate(per_seq(p[:, :r])):
            vt = cvt_ref[b0 + b, kv_rows, :]
            o_cache.append(lax.dot_general(pb.astype(BF16),
                                           jnp.concatenate([vt, vt], axis=0).astype(BF16),
                                           nt_dims, preferred_element_type=F32))
        o = stacked(o_cache) + _dot(p[:, r:].astype(BF16), vn[g])
        outs.append(_unstack_heads(o))
    return jnp.concatenate(outs, axis=1)


def _mix_sample_kernel(sinks_ref, x_ref, u_ref, st_ref, q_ref, k_ref, v_ref, ckt_ref, cvt_ref, gl_ref,
                       wpool_ref, pscale_ref, wbp_ref, wba_ref, wout_ref, o_ref, oscr, *, gs, pos0):
    t = SUBLANES
    rows = gs * t

    row = lax.broadcasted_iota(jnp.int32, (gs, t, POOL_GROUP), 1)
    cnt_pos = pos0 + lax.broadcasted_iota(jnp.int32, (gs, t, 1), 1) + 1
    a_parts = []
    for g, w in enumerate(POOL_WINDOWS):
        cols = slice(g * POOL_GROUP, (g + 1) * POOL_GROUP)
        tiles = [st_ref[:, 0:t, cols], st_ref[:, t:2 * t, cols], u_ref[:, :, cols]]
        xt = tiles[2]
        step = 1
        while step < w and step < t:
            tiles = [tiles[n] + _tile_shift(tiles[n], tiles[max(n - 1, 0)], step, row)
                     for n in range(3)]
            step *= 2
        acc = tiles[2] + tiles[1] if w == 2 * t else tiles[2]
        cnt = jnp.minimum(cnt_pos, w).astype(F32)
        d = (acc / cnt - xt).reshape(rows, POOL_GROUP)
        y = _dot(d.astype(BF16), wpool_ref[g])
        a_parts.append(y * pscale_ref[:, cols])
    pool_a = jnp.concatenate(a_parts, axis=1)

    bias = _sample_bias4()
    for blk in range(gs // SEQ_BLOCK):
        rs = slice(blk * WINDOW, (blk + 1) * WINDOW)
        oscr[rs, :] = _attend_cached(q_ref[rs, :], k_ref[rs, :], v_ref[rs, :], ckt_ref, cvt_ref,
                                     blk * SEQ_BLOCK, bias, sinks_ref)

    o_ref[...] = _merge(x_ref[...], pool_a, oscr[...], gl_ref[...], wbp_ref, wba_ref, wout_ref)


def _mix_sample(x, u3, stpad, q, k, v, ck, cv, gl, sinks, wpool_bf, pscale, wbp_bf, wba_bf, wout_bf,
                gs, pos0):
    nseq, t = u3.shape[0], u3.shape[1]
    rows = gs * t
    row = lambda c: pl.BlockSpec((rows, c), lambda i: (i, 0))
    seq = lambda a, b: pl.BlockSpec((gs, a, b), lambda i: (i, 0, 0))
    return pl.pallas_call(
        functools.partial(_mix_sample_kernel, gs=gs, pos0=pos0),
        grid=(nseq // gs,),
        in_specs=[pl.BlockSpec(memory_space=pltpu.SMEM),
                  row(D_MODEL), seq(t, POOL_WIDTH), seq(2 * t, POOL_WIDTH),
                  row(ATTN_WIDTH), row(KV_WIDTH), row(KV_WIDTH),
                  seq(WINDOW, KV_WIDTH), seq(WINDOW, KV_WIDTH), row(2 * D_MODEL),
                  _const_spec(wpool_bf.shape), _const_spec(pscale.shape),
                  _const_spec(wbp_bf.shape), _const_spec(wba_bf.shape), _const_spec(wout_bf.shape)],
        out_specs=row(D_MODEL),
        out_shape=jax.ShapeDtypeStruct((nseq * t, D_MODEL), F32),
        scratch_shapes=[pltpu.VMEM((rows, ATTN_WIDTH), F32)],
        compiler_params=_params(1),
        name="mix_sample",
    )(sinks, x, u3, stpad, q, k, v, ck, cv, gl, wpool_bf, pscale, wbp_bf, wba_bf, wout_bf)


FF_CHUNK = 128


def _gated_act(taps, cw_ref, cb_ref, c0):
    def conv(cols):
        x0, x1, x2 = taps(cols)
        return (x0 * cw_ref[0:1, cols] + x1 * cw_ref[1:2, cols] + x2 * cw_ref[2:3, cols]
                + cb_ref[:, cols])
    gate = conv(slice(c0, c0 + FF_CHUNK))
    val = conv(slice(D_FF + c0, D_FF + c0 + FF_CHUNK))
    return (jax.nn.silu(gate) * val).astype(BF16)


def _ffn_prompt_kernel(x_ref, g_ref, wup_ref, cw_ref, cb_ref, wdn_ref, gf_ref, o_ref, cs_ref,
                       upx, act, *, tm):
    i = pl.program_id(1)
    pre = SUBLANES

    @pl.when(i == 0)
    def _():
        upx[0:pre, :] = jnp.zeros((pre, 2 * D_FF), F32)

    x = x_ref[...]
    h = _rmsnorm(x, g_ref[...]).astype(BF16)
    upx[pre:pre + tm, :] = _dot(h, wup_ref[...])

    def taps(cols):
        e = upx[:, cols]
        return pltpu.roll(e, 2, axis=0)[pre:], pltpu.roll(e, 1, axis=0)[pre:], e[pre:]

    for c0 in range(0, D_FF, FF_CHUNK):
        act[:, c0:c0 + FF_CHUNK] = _gated_act(taps, cw_ref, cb_ref, c0)

    y = x + _dot(act[...], wdn_ref[...])
    o_ref[...] = _rmsnorm(y, gf_ref[...])
    cs_ref[0] = upx[pre + tm - CONV_STATE:pre + tm, :]
    upx[0:pre, :] = upx[tm:tm + pre, :]


def _ffn_prompt(x, g_ffn, wup_bf, conv_w, conv_b, wdn_bf, g_final, batch, tm):
    n = x.shape[0]
    nt = n // batch // tm
    row = pl.BlockSpec((tm, D_MODEL), lambda b, i: (b * nt + i, 0))
    return pl.pallas_call(
        functools.partial(_ffn_prompt_kernel, tm=tm),
        grid=(batch, nt),
        in_specs=[row, _const_spec((1, D_MODEL)), _const_spec(wup_bf.shape),
                  _const_spec(conv_w.shape), _const_spec(conv_b.shape), _const_spec(wdn_bf.shape),
                  _const_spec((1, D_MODEL))],
        out_specs=[row, pl.BlockSpec((1, CONV_STATE, 2 * D_FF), lambda b, i: (b, 0, 0))],
        out_shape=[jax.ShapeDtypeStruct((n, D_MODEL), F32),
                   jax.ShapeDtypeStruct((batch, CONV_STATE, 2 * D_FF), F32)],
        scratch_shapes=[pltpu.VMEM((SUBLANES + tm, 2 * D_FF), F32),
                        pltpu.VMEM((tm, D_FF), BF16)],
        compiler_params=_params(2),
        name="ffn_prompt",
    )(x, g_ffn, wup_bf, conv_w, conv_b, wdn_bf, g_final)


def _ffn_sample_kernel(x_ref, st_ref, g_ref, wup_ref, cw_ref, cb_ref, wdn_ref, gf_ref,
                       o_ref, cs_ref, up, act, *, gs):
    t = SUBLANES
    rows = gs * t
    x = x_ref[...]
    h = _rmsnorm(x, g_ref[...]).astype(BF16)
    up[...] = _dot(h, wup_ref[...])
    row = lax.broadcasted_iota(jnp.int32, (gs, t, FF_CHUNK), 1)

    def taps(cols):
        x2 = up[:, cols].reshape(gs, t, FF_CHUNK)
        s0 = jnp.broadcast_to(st_ref[:, 0:1, cols], x2.shape)
        s1 = jnp.broadcast_to(st_ref[:, 1:2, cols], x2.shape)
        x1 = jnp.where(row < 1, s1, pltpu.roll(x2, 1, axis=1))
        x0 = jnp.where(row < 1, s0, jnp.where(row < 2, s1, pltpu.roll(x2, 2, axis=1)))
        flat = lambda a: a.reshape(rows, FF_CHUNK)
        return flat(x0), flat(x1), flat(x2)

    for c0 in range(0, D_FF, FF_CHUNK):
        act[:, c0:c0 + FF_CHUNK] = _gated_act(taps, cw_ref, cb_ref, c0)

    y = x + _dot(act[...], wdn_ref[...])
    o_ref[...] = _rmsnorm(y, gf_ref[...])
    for c0 in range(0, 2 * D_FF, 4 * FF_CHUNK):
        cols = slice(c0, min(c0 + 4 * FF_CHUNK, 2 * D_FF))
        cs_ref[:, :, cols] = up[:, cols].reshape(gs, t, -1)[:, t - CONV_STATE:, :]


def _ffn_sample(x, st, g_ffn, wup_bf, conv_w, conv_b, wdn_bf, g_final, gs):
    nseq = st.shape[0]
    t = x.shape[0] // nseq
    rows = gs * t
    row = pl.BlockSpec((rows, D_MODEL), lambda i: (i, 0))
    st_spec = pl.BlockSpec((gs, CONV_STATE, 2 * D_FF), lambda i: (i, 0, 0))
    return pl.pallas_call(
        functools.partial(_ffn_sample_kernel, gs=gs),
        grid=(nseq // gs,),
        in_specs=[row, st_spec,
                  _const_spec((1, D_MODEL)), _const_spec(wup_bf.shape),
                  _const_spec(conv_w.shape), _const_spec(conv_b.shape), _const_spec(wdn_bf.shape),
                  _const_spec((1, D_MODEL))],
        out_specs=[row, st_spec],
        out_shape=[jax.ShapeDtypeStruct((nseq * t, D_MODEL), F32),
                   jax.ShapeDtypeStruct((nseq, CONV_STATE, 2 * D_FF), F32)],
        scratch_shapes=[pltpu.VMEM((rows, 2 * D_FF), F32), pltpu.VMEM((rows, D_FF), BF16)],
        compiler_params=_params(1),
        name="ffn_sample",
    )(x, st, g_ffn, wup_bf, conv_w, conv_b, wdn_bf, g_final)


TM_IN = 1024
TQ_MIX = 512
TM_FFN = 512
GS_MIX = 32
GS_FFN = 64


def kernel(x_prompt, x_sample, state_pool, cache_k, cache_v, state_conv, g_mix, w_in, w_pool_grp,
           pool_scale, sinks, w_branch_pool, w_branch_attn, w_out, g_ffn, w_up, conv_w, conv_b,
           w_down, g_final):
    depth = g_mix.shape[0]
    batch, seq, _ = x_prompt.shape
    dec_batch, dec_seq, _ = x_sample.shape
    assert dec_seq == SUBLANES and seq % TQ_MIX == 0 and seq % TM_FFN == 0
    assert depth == 1

    yp = x_prompt.reshape(batch * seq, D_MODEL)
    ys = x_sample.reshape(dec_batch * dec_seq, D_MODEL)
    outs = [[] for _ in range(8)]
    gfin = g_final.reshape(1, D_MODEL)
    for l in range(depth):
        gmix = g_mix[l].reshape(1, D_MODEL)
        gffn = g_ffn[l].reshape(1, D_MODEL)
        w_in_bf = w_in[l].astype(BF16)
        wpool_bf = w_pool_grp[l].astype(BF16)
        pscale = pool_scale[l].reshape(1, POOL_WIDTH)
        wbp_bf = w_branch_pool[l].astype(BF16)
        wba_bf = w_branch_attn[l].astype(BF16)
        wout_bf = w_out[l].astype(BF16)
        wup_bf = w_up[l].astype(BF16)
        wdn_bf = w_down[l].astype(BF16)
        cb = conv_b[l].reshape(1, 2 * D_FF)

        u, q, k, v, gl = _in_proj(yp, gmix, w_in_bf, TM_IN)
        x1, kt, vt = _mix_prompt(yp, u, q, k, v, gl, sinks[l], wpool_bf, pscale, wbp_bf, wba_bf,
                                 wout_bf, batch, TQ_MIX)
        yp, conv_p = _ffn_prompt(x1, gffn, wup_bf, conv_w[l], cb, wdn_bf, gfin, batch, TM_FFN)
        outs[0].append(u.reshape(batch, seq, POOL_WIDTH)[:, seq - POOL_STATE:])
        untranspose = lambda t: t.reshape(-1, N_KV_HEADS, HEAD_DIM, WINDOW).transpose(0, 3, 1, 2)
        outs[1].append(untranspose(kt))
        outs[2].append(untranspose(vt))
        outs[3].append(conv_p)

        us, qs, ks, vs, gls = _in_proj(ys, gmix, w_in_bf, TM_IN)
        u3 = us.reshape(dec_batch, dec_seq, POOL_WIDTH)
        stpad = jnp.pad(state_pool[l], ((0, 0), (1, 0), (0, 0)))
        transposed = lambda c: c.transpose(0, 2, 3, 1).reshape(dec_batch, KV_WIDTH, WINDOW)
        ck, cv = transposed(cache_k[l]), transposed(cache_v[l])
        x1s = _mix_sample(ys, u3, stpad, qs, ks, vs, ck, cv, gls, sinks[l], wpool_bf, pscale,
                          wbp_bf, wba_bf, wout_bf, GS_MIX, PAST_LEN)
        ys, conv_s = _ffn_sample(x1s, state_conv[l], gffn, wup_bf, conv_w[l], cb, wdn_bf, gfin,
                                   GS_FFN)
        outs[4].append(jnp.concatenate([state_pool[l], u3], axis=1)[:, dec_seq:])
        k3 = ks.reshape(dec_batch, dec_seq, N_KV_HEADS, HEAD_DIM)
        v3 = vs.reshape(dec_batch, dec_seq, N_KV_HEADS, HEAD_DIM)
        outs[5].append(jnp.concatenate([cache_k[l], k3], axis=1)[:, dec_seq:])
        outs[6].append(jnp.concatenate([cache_v[l], v3], axis=1)[:, dec_seq:])
        outs[7].append(conv_s)
    return (yp.reshape(batch, seq, D_MODEL), ys.reshape(dec_batch, dec_seq, D_MODEL),
            *[jnp.stack(o) for o in outs])
```

```python
import functools

import jax
import jax.numpy as jnp
from jax import lax
from jax.experimental import pallas as pl
from jax.experimental.pallas import tpu as pltpu

D_MODEL = 1024
POOL_WIDTH = 512
POOL_WINDOWS = (2, 4, 8, 16)
POOL_GROUP = 128
POOL_STATE = 15
HEAD_DIM = 64
N_HEADS = 8
N_KV_HEADS = 2
GQA_GROUP = N_HEADS // N_KV_HEADS
WINDOW = 128
ATTN_WIDTH = N_HEADS * HEAD_DIM
KV_WIDTH = N_KV_HEADS * HEAD_DIM
ATTN_SCALE = HEAD_DIM ** -0.5
D_FF = 2816
CONV_STATE = 2
EPS = 1e-6
PAST_LEN = 16384

LANES = 128
SUBLANES = 8
VMEM_LIMIT = 56 * 1024 * 1024

F32 = jnp.float32
BF16 = jnp.bfloat16


def _rmsnorm(x, g):
    inv = lax.rsqrt(jnp.mean(x * x, axis=-1, keepdims=True) + EPS)
    return (x * inv) * g


def _dot(a, b):
    return jnp.dot(a, b, preferred_element_type=F32)


def _const_spec(shape):
    nd = len(shape)
    return pl.BlockSpec(shape, lambda *_: (0,) * nd, pipeline_mode=pl.Buffered(1))


def _params(n_axes):
    return pltpu.CompilerParams(dimension_semantics=("arbitrary",) * n_axes,
                                vmem_limit_bytes=VMEM_LIMIT)


def _in_proj_kernel(x_ref, g_ref, w_ref, u_ref, q_ref, k_ref, v_ref, gl_ref):
    h = _rmsnorm(x_ref[...], g_ref[...]).astype(BF16)
    o_q = POOL_WIDTH
    o_k = o_q + ATTN_WIDTH
    o_gl = o_k + 2 * KV_WIDTH
    u_ref[...] = _dot(h, w_ref[:, 0:o_q])
    q_ref[...] = _dot(h, w_ref[:, o_q:o_k])
    kv = _dot(h, w_ref[:, o_k:o_gl])
    k_ref[...] = kv[:, :KV_WIDTH]
    v_ref[...] = kv[:, KV_WIDTH:]
    gl_ref[...] = _dot(h, w_ref[:, o_gl:])


def _in_proj(x, g_mix, w_in_bf, tm):
    n = x.shape[0]
    in_width = w_in_bf.shape[1]
    row = lambda c: pl.BlockSpec((tm, c), lambda i: (i, 0))
    return pl.pallas_call(
        _in_proj_kernel,
        grid=(n // tm,),
        in_specs=[row(D_MODEL), _const_spec((1, D_MODEL)), _const_spec((D_MODEL, in_width))],
        out_specs=[row(POOL_WIDTH), row(ATTN_WIDTH), row(KV_WIDTH), row(KV_WIDTH), row(2 * D_MODEL)],
        out_shape=[jax.ShapeDtypeStruct((n, POOL_WIDTH), F32),
                   jax.ShapeDtypeStruct((n, ATTN_WIDTH), F32),
                   jax.ShapeDtypeStruct((n, KV_WIDTH), F32),
                   jax.ShapeDtypeStruct((n, KV_WIDTH), F32),
                   jax.ShapeDtypeStruct((n, 2 * D_MODEL), F32)],
        compiler_params=_params(1),
        name="in_proj",
    )(x, g_mix, w_in_bf)


def _dup_kv_heads(x):
    lo = lax.broadcasted_iota(jnp.int32, x.shape, 1) < HEAD_DIM
    r = pltpu.roll(x, HEAD_DIM, axis=1)
    return [jnp.where(lo, x, r).astype(BF16), jnp.where(lo, r, x).astype(BF16)]


def _stack_heads(q, g):
    lo = lax.broadcasted_iota(jnp.int32, (q.shape[0], LANES), 1) < HEAD_DIM
    scale_lo = jnp.where(lo, ATTN_SCALE, 0.0).astype(F32)
    scale_hi = jnp.where(lo, 0.0, ATTN_SCALE).astype(F32)
    base = g * GQA_GROUP * HEAD_DIM
    p0 = q[:, base:base + LANES]
    p1 = q[:, base + LANES:base + 2 * LANES]
    return jnp.concatenate([p0 * scale_lo, p0 * scale_hi, p1 * scale_lo, p1 * scale_hi], axis=0)


def _attn_scores(q, kdup, g, bias4):
    return lax.dot_general(_stack_heads(q, g).astype(BF16), kdup, (((1,), (1,)), ((), ())),
                           preferred_element_type=F32) + bias4


def _sink_column(r, g, sinks_ref):
    row4 = lax.broadcasted_iota(jnp.int32, (GQA_GROUP * r, 1), 0)
    sk = jnp.full((GQA_GROUP * r, 1), sinks_ref[g * GQA_GROUP + GQA_GROUP - 1], F32)
    for hh in range(GQA_GROUP - 2, -1, -1):
        sk = jnp.where(row4 < (hh + 1) * r, sinks_ref[g * GQA_GROUP + hh], sk)
    return sk


def _sink_softmax(s, sk):
    m = jnp.maximum(jnp.max(s, axis=-1, keepdims=True), sk)
    e = jnp.exp(s - m)
    den = jnp.sum(e, axis=-1, keepdims=True) + jnp.exp(sk - m)
    return e * (1.0 / den)


def _unstack_heads(o):
    r = o.shape[0] // GQA_GROUP
    lo = lax.broadcasted_iota(jnp.int32, (r, LANES), 1) < HEAD_DIM
    return jnp.concatenate([jnp.where(lo, o[0:r], o[r:2 * r]),
                            jnp.where(lo, o[2 * r:3 * r], o[3 * r:4 * r])], axis=1)


def _software_pipeline(n_units, stages, after_tick=()):
    vals = {}
    for tick in range(n_units + len(stages) - 1):
        for k in reversed(range(len(stages))):
            u = tick - k
            if 0 <= u < n_units:
                vals[u] = stages[k](u, vals.get(u))
        if tick < len(after_tick) and after_tick[tick] is not None:
            after_tick[tick]()


def _band_bias4(r, s, first_key=0):
    i = lax.broadcasted_iota(jnp.int32, (GQA_GROUP * r, s), 0) & (r - 1)
    j = lax.broadcasted_iota(jnp.int32, (GQA_GROUP * r, s), 1)
    valid = (j >= jnp.maximum(i, first_key)) & (j <= i + WINDOW)
    return jnp.where(valid, 0.0, -jnp.inf).astype(F32)


def _window_sums(e, w):
    step = 1
    while step < w:
        e = e + pltpu.roll(e, step, axis=0)
        step *= 2
    return e


def _gate_merge(pa, po, gl):
    ga = jax.nn.sigmoid(gl[:, :D_MODEL])
    gb = jax.nn.sigmoid(gl[:, D_MODEL:])
    return (ga * pa + gb * po).astype(BF16)


def _merge(x, pool_a, attn_o, gl, wbp_ref, wba_ref, wout_ref):
    pa = _dot(pool_a.astype(BF16), wbp_ref[...])
    po = _dot(attn_o.astype(BF16), wba_ref[...])
    return x + _dot(_gate_merge(pa, po, gl), wout_ref[...])


def _mix_prompt_kernel(sinks_ref, u_ref, q_ref, k_ref, v_ref, gl_ref, wpool_ref, pscale_ref,
                       wbp_ref, wba_ref, wout_ref, xo_ref, o_ref, kt_ref, vt_ref, uext, kdup, vdup,
                       oscr, bias, pa_scr, mixed, *, tq, nt):
    s = pl.program_id(0)
    i = s % nt
    pre = POOL_STATE + 1

    @pl.when(s == 0)
    def _():
        mixed[...] = jnp.zeros(mixed.shape, BF16)

    @pl.when(i == 0)
    def _():
        uext[0:pre, :] = jnp.zeros((pre, POOL_WIDTH), F32)
        kdup[:, 0:WINDOW, :] = jnp.zeros((N_KV_HEADS, WINDOW, LANES), BF16)
        vdup[:, 0:WINDOW, :] = jnp.zeros((N_KV_HEADS, WINDOW, LANES), BF16)
        bias[0] = _band_bias4(WINDOW, 2 * WINDOW, WINDOW)
        bias[1] = _band_bias4(WINDOW, 2 * WINDOW)

    uext[pre:pre + tq, :] = u_ref[...]
    for dst, src in ((kdup, k_ref), (vdup, v_ref)):
        for g, d in enumerate(_dup_kv_heads(src[...])):
            dst[g, WINDOW:WINDOW + tq, :] = d
    sink_cols = [_sink_column(WINDOW, g, sinks_ref) for g in range(N_KV_HEADS)]

    piece = D_MODEL // 4

    def out_prev(n):
        cols = slice(n * piece, (n + 1) * piece)
        o_ref[:, cols] = xo_ref[:, cols] + _dot(mixed[...], wout_ref[:, cols])

    pos1 = i * tq + lax.broadcasted_iota(jnp.int32, (tq, 1), 0) + 1
    a_parts = []
    for g, w in enumerate(POOL_WINDOWS):
        cols = slice(g * POOL_GROUP, (g + 1) * POOL_GROUP)
        e = uext[:, cols]
        inv_cnt = 1.0 / jnp.minimum(pos1, w).astype(F32)
        d = _window_sums(e, w)[pre:] * inv_cnt - e[pre:]
        y = _dot(d.astype(BF16), wpool_ref[g])
        a_parts.append(y * pscale_ref[:, cols])
    pool_a = jnp.concatenate(a_parts, axis=1).astype(BF16)

    n_units = N_KV_HEADS * tq // WINDOW

    def pooled_branch(n):
        cols = slice(n * 2 * piece, (n + 1) * 2 * piece)
        pa_scr[:, cols] = _dot(pool_a, wbp_ref[:, cols])

    def merge_half(half):
        rows = slice(half * tq // 2, (half + 1) * tq // 2)
        po = _dot(oscr[rows, :].astype(BF16), wba_ref[...])
        mixed[rows, :] = _gate_merge(pa_scr[rows, :], po, gl_ref[rows, :])

    half_done = n_units // 2 + 1
    after_tick = [functools.partial(out_prev, n) for n in range(4)]
    after_tick += [functools.partial(pooled_branch, n) for n in range(2)]
    assert len(after_tick) <= half_done + 1
    after_tick += [None] * (half_done + 1 - len(after_tick)) + [functools.partial(merge_half, 0)]

    def unit(u):
        nb, g = divmod(u, N_KV_HEADS)
        return nb, g, slice(nb * WINDOW, (nb + 1) * WINDOW), slice(nb * WINDOW, (nb + 2) * WINDOW)

    def scores(u, _):
        nb, g, rows, keys = unit(u)
        slot = jnp.minimum(i * (tq // WINDOW) + nb, 1)
        return _attn_scores(q_ref[rows, :], kdup[g, keys, :], g, bias[slot])

    def softmax(u, s):
        return _sink_softmax(s, sink_cols[unit(u)[1]]).astype(BF16)

    def values(u, p):
        nb, g, rows, keys = unit(u)
        oscr[rows, g * 2 * LANES:(g + 1) * 2 * LANES] = _unstack_heads(_dot(p, vdup[g, keys, :]))

    _software_pipeline(n_units, [scores, softmax, values], after_tick)
    merge_half(1)

    @pl.when(i == nt - 1)
    def _():
        kt_ref[0] = k_ref[tq - WINDOW:tq, :].T
        vt_ref[0] = v_ref[tq - WINDOW:tq, :].T

    uext[0:pre, :] = uext[tq:tq + pre, :]
    kdup[:, 0:WINDOW, :] = kdup[:, tq:tq + WINDOW, :]
    vdup[:, 0:WINDOW, :] = vdup[:, tq:tq + WINDOW, :]


def _mix_prompt(x, u, q, k, v, gl, sinks, wpool_bf, pscale, wbp_bf, wba_bf, wout_bf, batch, tq):
    n = x.shape[0]
    n_tiles = n // tq
    nt = n_tiles // batch
    row = lambda c: pl.BlockSpec((tq, c), lambda s: (jnp.minimum(s, n_tiles - 1), 0))
    row_prev = pl.BlockSpec((tq, D_MODEL), lambda s: (jnp.maximum(s - 1, 0), 0))
    last_t = pl.BlockSpec((1, KV_WIDTH, WINDOW),
                          lambda s: (jnp.minimum(s, n_tiles - 1) // nt, 0, 0))
    return pl.pallas_call(
        functools.partial(_mix_prompt_kernel, tq=tq, nt=nt),
        grid=(n_tiles + 1,),
        in_specs=[pl.BlockSpec(memory_space=pltpu.SMEM),
                  row(POOL_WIDTH), row(ATTN_WIDTH), row(KV_WIDTH), row(KV_WIDTH), row(2 * D_MODEL),
                  _const_spec(wpool_bf.shape), _const_spec(pscale.shape),
                  _const_spec(wbp_bf.shape), _const_spec(wba_bf.shape), _const_spec(wout_bf.shape),
                  row_prev],
        out_specs=[row_prev, last_t, last_t],
        out_shape=[jax.ShapeDtypeStruct((n, D_MODEL), F32),
                   jax.ShapeDtypeStruct((batch, KV_WIDTH, WINDOW), F32),
                   jax.ShapeDtypeStruct((batch, KV_WIDTH, WINDOW), F32)],
        scratch_shapes=[pltpu.VMEM((POOL_STATE + 1 + tq, POOL_WIDTH), F32),
                        pltpu.VMEM((N_KV_HEADS, WINDOW + tq, LANES), BF16),
                        pltpu.VMEM((N_KV_HEADS, WINDOW + tq, LANES), BF16),
                        pltpu.VMEM((tq, ATTN_WIDTH), F32),
                        pltpu.VMEM((2, GQA_GROUP * WINDOW, 2 * WINDOW), F32),
                        pltpu.VMEM((tq, D_MODEL), F32),
                        pltpu.VMEM((tq, D_MODEL), BF16)],
        compiler_params=_params(1),
        name="mix_prompt",
    )(sinks, u, q, k, v, gl, wpool_bf, pscale, wbp_bf, wba_bf, wout_bf, x)


def _tile_shift(cur, prev, j, row):
    return jnp.where(row < j, pltpu.roll(prev, j, axis=1), pltpu.roll(cur, j, axis=1))


SEQ_BLOCK = WINDOW // SUBLANES


def _sample_bias4():
    shape = (GQA_GROUP * WINDOW, 2 * WINDOW)
    i = lax.broadcasted_iota(jnp.int32, shape, 0) & (WINDOW - 1)
    j = lax.broadcasted_iota(jnp.int32, shape, 1)
    b, t = i >> 3, i & (SUBLANES - 1)
    new = j - WINDOW
    valid = ((j < WINDOW) & (j >= t)) | ((new >> 3 == b) & ((new & (SUBLANES - 1)) <= t) & (new >= 0))
    return jnp.where(valid, 0.0, -jnp.inf).astype(F32)


def _attend_cached(q, knew, vnew, ckt_ref, cvt_ref, b0, bias4, sinks_ref):
    r, t = WINDOW, SUBLANES
    nt_dims = (((1,), (1,)), ((), ()))
    kn, vn = _dup_kv_heads(knew), _dup_kv_heads(vnew)

    def per_seq(x):
        return [jnp.concatenate([x[j * r + b * t:j * r + (b + 1) * t] for j in range(GQA_GROUP)],
                                axis=0) for b in range(SEQ_BLOCK)]

    def stacked(xs):
        return jnp.concatenate([xs[b][j * t:(j + 1) * t] for j in range(GQA_GROUP)
                                for b in range(SEQ_BLOCK)], axis=0)

    outs = []
    for g in range(N_KV_HEADS):
        kv_rows = slice(g * HEAD_DIM, (g + 1) * HEAD_DIM)
        qs = _stack_heads(q, g)
        s_new = lax.dot_general(qs.astype(BF16), kn[g], nt_dims, preferred_element_type=F32)
        s_cache = []
        for b, qb in enumerate(per_seq(qs)):
            kt = ckt_ref[b0 + b, kv_rows, :]
            s_cache.append(_dot(qb.astype(BF16), jnp.concatenate([kt, kt], axis=0).astype(BF16)))
        s = jnp.concatenate([stacked(s_cache), s_new], axis=1) + bias4
        p = _sink_softmax(s, _sink_column(r, g, sinks_ref))
        o_cache = []
        for b, pb in enumerate(per_seq(p[:, :r])):
            vt = cvt_ref[b0 + b, kv_rows, :]
            o_cache.append(lax.dot_general(pb.astype(BF16),
                                           jnp.concatenate([vt, vt], axis=0).astype(BF16),
                                           nt_dims, preferred_element_type=F32))
        o = stacked(o_cache) + _dot(p[:, r:].astype(BF16), vn[g])
        outs.append(_unstack_heads(o))
    return jnp.concatenate(outs, axis=1)


def _mix_sample_kernel(sinks_ref, x_ref, u_ref, st_ref, q_ref, k_ref, v_ref, ckt_ref, cvt_ref, gl_ref,
                       wpool_ref, pscale_ref, wbp_ref, wba_ref, wout_ref, o_ref, oscr, *, gs, pos0):
    t = SUBLANES
    rows = gs * t

    row = lax.broadcasted_iota(jnp.int32, (gs, t, POOL_GROUP), 1)
    cnt_pos = pos0 + lax.broadcasted_iota(jnp.int32, (gs, t, 1), 1) + 1
    a_parts = []
    for g, w in enumerate(POOL_WINDOWS):
        cols = slice(g * POOL_GROUP, (g + 1) * POOL_GROUP)
        tiles = [st_ref[:, 0:t, cols], st_ref[:, t:2 * t, cols], u_ref[:, :, cols]]
        xt = tiles[2]
        step = 1
        while step < w and step < t:
            tiles = [tiles[n] + _tile_shift(tiles[n], tiles[max(n - 1, 0)], step, row)
                     for n in range(3)]
            step *= 2
        acc = tiles[2] + tiles[1] if w == 2 * t else tiles[2]
        cnt = jnp.minimum(cnt_pos, w).astype(F32)
        d = (acc / cnt - xt).reshape(rows, POOL_GROUP)
        y = _dot(d.astype(BF16), wpool_ref[g])
        a_parts.append(y * pscale_ref[:, cols])
    pool_a = jnp.concatenate(a_parts, axis=1)

    bias = _sample_bias4()
    for blk in range(gs // SEQ_BLOCK):
        rs = slice(blk * WINDOW, (blk + 1) * WINDOW)
        oscr[rs, :] = _attend_cached(q_ref[rs, :], k_ref[rs, :], v_ref[rs, :], ckt_ref, cvt_ref,
                                     blk * SEQ_BLOCK, bias, sinks_ref)

    o_ref[...] = _merge(x_ref[...], pool_a, oscr[...], gl_ref[...], wbp_ref, wba_ref, wout_ref)


def _mix_sample(x, u3, stpad, q, k, v, ck, cv, gl, sinks, wpool_bf, pscale, wbp_bf, wba_bf, wout_bf,
                gs, pos0):
    nseq, t = u3.shape[0], u3.shape[1]
    rows = gs * t
    row = lambda c: pl.BlockSpec((rows, c), lambda i: (i, 0))
    seq = lambda a, b: pl.BlockSpec((gs, a, b), lambda i: (i, 0, 0))
    return pl.pallas_call(
        functools.partial(_mix_sample_kernel, gs=gs, pos0=pos0),
        grid=(nseq // gs,),
        in_specs=[pl.BlockSpec(memory_space=pltpu.SMEM),
                  row(D_MODEL), seq(t, POOL_WIDTH), seq(2 * t, POOL_WIDTH),
                  row(ATTN_WIDTH), row(KV_WIDTH), row(KV_WIDTH),
                  seq(WINDOW, KV_WIDTH), seq(WINDOW, KV_WIDTH), row(2 * D_MODEL),
                  _const_spec(wpool_bf.shape), _const_spec(pscale.shape),
                  _const_spec(wbp_bf.shape), _const_spec(wba_bf.shape), _const_spec(wout_bf.shape)],
        out_specs=row(D_MODEL),
        out_shape=jax.ShapeDtypeStruct((nseq * t, D_MODEL), F32),
        scratch_shapes=[pltpu.VMEM((rows, ATTN_WIDTH), F32)],
        compiler_params=_params(1),
        name="mix_sample",
    )(sinks, x, u3, stpad, q, k, v, ck, cv, gl, wpool_bf, pscale, wbp_bf, wba_bf, wout_bf)


FF_CHUNK = 256


def _gated_act(taps, cw_ref, cb_ref, c0):
    def conv(cols):
        x0, x1, x2 = taps(cols)
        return (x0 * cw_ref[0:1, cols] + x1 * cw_ref[1:2, cols] + x2 * cw_ref[2:3, cols]
                + cb_ref[:, cols])
    gate = conv(slice(c0, c0 + FF_CHUNK))
    val = conv(slice(D_FF + c0, D_FF + c0 + FF_CHUNK))
    return (jax.nn.silu(gate) * val).astype(BF16)


def _ffn_prompt_kernel(x_ref, g_ref, wup_ref, cw_ref, cb_ref, wdn_ref, gf_ref, o_ref, cs_ref,
                       upx, act, *, tm):
    i = pl.program_id(1)
    pre = SUBLANES

    @pl.when(i == 0)
    def _():
        upx[0:pre, :] = jnp.zeros((pre, 2 * D_FF), F32)

    x = x_ref[...]
    h = _rmsnorm(x, g_ref[...]).astype(BF16)
    upx[pre:pre + tm, :] = _dot(h, wup_ref[...])

    def taps(cols):
        e = upx[:, cols]
        return pltpu.roll(e, 2, axis=0)[pre:], pltpu.roll(e, 1, axis=0)[pre:], e[pre:]

    for c0 in range(0, D_FF, FF_CHUNK):
        act[:, c0:c0 + FF_CHUNK] = _gated_act(taps, cw_ref, cb_ref, c0)

    y = x + _dot(act[...], wdn_ref[...])
    o_ref[...] = _rmsnorm(y, gf_ref[...])
    cs_ref[0] = upx[pre + tm - CONV_STATE:pre + tm, :]
    upx[0:pre, :] = upx[tm:tm + pre, :]


def _ffn_prompt(x, g_ffn, wup_bf, conv_w, conv_b, wdn_bf, g_final, batch, tm):
    n = x.shape[0]
    nt = n // batch // tm
    row = pl.BlockSpec((tm, D_MODEL), lambda b, i: (b * nt + i, 0))
    return pl.pallas_call(
        functools.partial(_ffn_prompt_kernel, tm=tm),
        grid=(batch, nt),
        in_specs=[row, _const_spec((1, D_MODEL)), _const_spec(wup_bf.shape),
                  _const_spec(conv_w.shape), _const_spec(conv_b.shape), _const_spec(wdn_bf.shape),
                  _const_spec((1, D_MODEL))],
        out_specs=[row, pl.BlockSpec((1, CONV_STATE, 2 * D_FF), lambda b, i: (b, 0, 0))],
        out_shape=[jax.ShapeDtypeStruct((n, D_MODEL), F32),
                   jax.ShapeDtypeStruct((batch, CONV_STATE, 2 * D_FF), F32)],
        scratch_shapes=[pltpu.VMEM((SUBLANES + tm, 2 * D_FF), F32),
                        pltpu.VMEM((tm, D_FF), BF16)],
        compiler_params=_params(2),
        name="ffn_prompt",
    )(x, g_ffn, wup_bf, conv_w, conv_b, wdn_bf, g_final)


def _ffn_sample_kernel(x_ref, st_ref, g_ref, wup_ref, cw_ref, cb_ref, wdn_ref, gf_ref,
                       o_ref, cs_ref, up, act, *, gs):
    t = SUBLANES
    rows = gs * t
    x = x_ref[...]
    h = _rmsnorm(x, g_ref[...]).astype(BF16)
    up[...] = _dot(h, wup_ref[...])
    row = lax.broadcasted_iota(jnp.int32, (gs, t, FF_CHUNK), 1)

    def taps(cols):
        x2 = up[:, cols].reshape(gs, t, FF_CHUNK)
        s0 = jnp.broadcast_to(st_ref[:, 0:1, cols], x2.shape)
        s1 = jnp.broadcast_to(st_ref[:, 1:2, cols], x2.shape)
        x1 = jnp.where(row < 1, s1, pltpu.roll(x2, 1, axis=1))
        x0 = jnp.where(row < 1, s0, jnp.where(row < 2, s1, pltpu.roll(x2, 2, axis=1)))
        flat = lambda a: a.reshape(rows, FF_CHUNK)
        return flat(x0), flat(x1), flat(x2)

    for c0 in range(0, D_FF, FF_CHUNK):
        act[:, c0:c0 + FF_CHUNK] = _gated_act(taps, cw_ref, cb_ref, c0)

    y = x + _dot(act[...], wdn_ref[...])
    o_ref[...] = _rmsnorm(y, gf_ref[...])
    for c0 in range(0, 2 * D_FF, 4 * FF_CHUNK):
        cols = slice(c0, min(c0 + 4 * FF_CHUNK, 2 * D_FF))
        cs_ref[:, :, cols] = up[:, cols].reshape(gs, t, -1)[:, t - CONV_STATE:, :]


def _ffn_sample(x, st, g_ffn, wup_bf, conv_w, conv_b, wdn_bf, g_final, gs):
    nseq = st.shape[0]
    t = x.shape[0] // nseq
    rows = gs * t
    row = pl.BlockSpec((rows, D_MODEL), lambda i: (i, 0))
    st_spec = pl.BlockSpec((gs, CONV_STATE, 2 * D_FF), lambda i: (i, 0, 0))
    return pl.pallas_call(
        functools.partial(_ffn_sample_kernel, gs=gs),
        grid=(nseq // gs,),
        in_specs=[row, st_spec,
                  _const_spec((1, D_MODEL)), _const_spec(wup_bf.shape),
                  _const_spec(conv_w.shape), _const_spec(conv_b.shape), _const_spec(wdn_bf.shape),
                  _const_spec((1, D_MODEL))],
        out_specs=[row, st_spec],
        out_shape=[jax.ShapeDtypeStruct((nseq * t, D_MODEL), F32),
                   jax.ShapeDtypeStruct((nseq, CONV_STATE, 2 * D_FF), F32)],
        scratch_shapes=[pltpu.VMEM((rows, 2 * D_FF), F32), pltpu.VMEM((rows, D_FF), BF16)],
        compiler_params=_params(1),
        name="ffn_sample",
    )(x, st, g_ffn, wup_bf, conv_w, conv_b, wdn_bf, g_final)


TM_IN = 1024
TQ_MIX = 512
TM_FFN = 512
GS_MIX = 32
GS_FFN = 64


def kernel(x_prompt, x_sample, state_pool, cache_k, cache_v, state_conv, g_mix, w_in, w_pool_grp,
           pool_scale, sinks, w_branch_pool, w_branch_attn, w_out, g_ffn, w_up, conv_w, conv_b,
           w_down, g_final):
    depth = g_mix.shape[0]
    batch, seq, _ = x_prompt.shape
    dec_batch, dec_seq, _ = x_sample.shape
    assert dec_seq == SUBLANES and seq % TQ_MIX == 0 and seq % TM_FFN == 0
    assert depth == 1

    yp = x_prompt.reshape(batch * seq, D_MODEL)
    ys = x_sample.reshape(dec_batch * dec_seq, D_MODEL)
    outs = [[] for _ in range(8)]
    gfin = g_final.reshape(1, D_MODEL)
    for l in range(depth):
        gmix = g_mix[l].reshape(1, D_MODEL)
        gffn = g_ffn[l].reshape(1, D_MODEL)
        w_in_bf = w_in[l].astype(BF16)
        wpool_bf = w_pool_grp[l].astype(BF16)
        pscale = pool_scale[l].reshape(1, POOL_WIDTH)
        wbp_bf = w_branch_pool[l].astype(BF16)
        wba_bf = w_branch_attn[l].astype(BF16)
        wout_bf = w_out[l].astype(BF16)
        wup_bf = w_up[l].astype(BF16)
        wdn_bf = w_down[l].astype(BF16)
        cb = conv_b[l].reshape(1, 2 * D_FF)

        u, q, k, v, gl = _in_proj(yp, gmix, w_in_bf, TM_IN)
        x1, kt, vt = _mix_prompt(yp, u, q, k, v, gl, sinks[l], wpool_bf, pscale, wbp_bf, wba_bf,
                                 wout_bf, batch, TQ_MIX)
        yp, conv_p = _ffn_prompt(x1, gffn, wup_bf, conv_w[l], cb, wdn_bf, gfin, batch, TM_FFN)
        outs[0].append(u.reshape(batch, seq, POOL_WIDTH)[:, seq - POOL_STATE:])
        untranspose = lambda t: t.reshape(-1, N_KV_HEADS, HEAD_DIM, WINDOW).transpose(0, 3, 1, 2)
        outs[1].append(untranspose(kt))
        outs[2].append(untranspose(vt))
        outs[3].append(conv_p)

        us, qs, ks, vs, gls = _in_proj(ys, gmix, w_in_bf, TM_IN)
        u3 = us.reshape(dec_batch, dec_seq, POOL_WIDTH)
        stpad = jnp.pad(state_pool[l], ((0, 0), (1, 0), (0, 0)))
        transposed = lambda c: c.transpose(0, 2, 3, 1).reshape(dec_batch, KV_WIDTH, WINDOW)
        ck, cv = transposed(cache_k[l]), transposed(cache_v[l])
        x1s = _mix_sample(ys, u3, stpad, qs, ks, vs, ck, cv, gls, sinks[l], wpool_bf, pscale,
                          wbp_bf, wba_bf, wout_bf, GS_MIX, PAST_LEN)
        ys, conv_s = _ffn_sample(x1s, state_conv[l], gffn, wup_bf, conv_w[l], cb, wdn_bf, gfin,
                                   GS_FFN)
        outs[4].append(jnp.concatenate([state_pool[l], u3], axis=1)[:, dec_seq:])
        k3 = ks.reshape(dec_batch, dec_seq, N_KV_HEADS, HEAD_DIM)
        v3 = vs.reshape(dec_batch, dec_seq, N_KV_HEADS, HEAD_DIM)
        outs[5].append(jnp.concatenate([cache_k[l], k3], axis=1)[:, dec_seq:])
        outs[6].append(jnp.concatenate([cache_v[l], v3], axis=1)[:, dec_seq:])
        outs[7].append(conv_s)
    return (yp.reshape(batch, seq, D_MODEL), ys.reshape(dec_batch, dec_seq, D_MODEL),
            *[jnp.stack(o) for o in outs])
```

```python
import functools

import jax
import jax.numpy as jnp
from jax import lax
from jax.experimental import pallas as pl
from jax.experimental.pallas import tpu as pltpu

D_MODEL = 1024
POOL_WIDTH = 512
POOL_WINDOWS = (2, 4, 8, 16)
POOL_GROUP = 128
POOL_STATE = 15
HEAD_DIM = 64
N_HEADS = 8
N_KV_HEADS = 2
GQA_GROUP = N_HEADS // N_KV_HEADS
WINDOW = 128
ATTN_WIDTH = N_HEADS * HEAD_DIM
KV_WIDTH = N_KV_HEADS * HEAD_DIM
ATTN_SCALE = HEAD_DIM ** -0.5
LOG2_E = 1.4426950408889634
LOGIT_SCALE = ATTN_SCALE * LOG2_E
D_FF = 2816
CONV_STATE = 2
EPS = 1e-6
PAST_LEN = 16384

LANES = 128
SUBLANES = 8
VMEM_LIMIT = 56 * 1024 * 1024

F32 = jnp.float32
BF16 = jnp.bfloat16


def _rmsnorm(x, g):
    inv = lax.rsqrt(jnp.mean(x * x, axis=-1, keepdims=True) + EPS)
    return (x * inv) * g


def _dot(a, b):
    return jnp.dot(a, b, preferred_element_type=F32)


def _const_spec(shape):
    nd = len(shape)
    return pl.BlockSpec(shape, lambda *_: (0,) * nd, pipeline_mode=pl.Buffered(1))


def _params(n_axes):
    return pltpu.CompilerParams(dimension_semantics=("arbitrary",) * n_axes,
                                vmem_limit_bytes=VMEM_LIMIT)


def _in_proj_kernel(x_ref, g_ref, w_ref, u_ref, q_ref, k_ref, v_ref, gl_ref):
    h = _rmsnorm(x_ref[...], g_ref[...]).astype(BF16)
    o_q = POOL_WIDTH
    o_k = o_q + ATTN_WIDTH
    o_gl = o_k + 2 * KV_WIDTH
    u_ref[...] = _dot(h, w_ref[:, 0:o_q])
    q_ref[...] = _dot(h, w_ref[:, o_q:o_k])
    kv = _dot(h, w_ref[:, o_k:o_gl])
    k_ref[...] = kv[:, :KV_WIDTH]
    v_ref[...] = kv[:, KV_WIDTH:]
    gl_ref[...] = _dot(h, w_ref[:, o_gl:])


def _in_proj(x, g_mix, w_in_bf, tm):
    n = x.shape[0]
    in_width = w_in_bf.shape[1]
    row = lambda c: pl.BlockSpec((tm, c), lambda i: (i, 0))
    return pl.pallas_call(
        _in_proj_kernel,
        grid=(n // tm,),
        in_specs=[row(D_MODEL), _const_spec((1, D_MODEL)), _const_spec((D_MODEL, in_width))],
        out_specs=[row(POOL_WIDTH), row(ATTN_WIDTH), row(KV_WIDTH), row(KV_WIDTH), row(2 * D_MODEL)],
        out_shape=[jax.ShapeDtypeStruct((n, POOL_WIDTH), F32),
                   jax.ShapeDtypeStruct((n, ATTN_WIDTH), F32),
                   jax.ShapeDtypeStruct((n, KV_WIDTH), F32),
                   jax.ShapeDtypeStruct((n, KV_WIDTH), F32),
                   jax.ShapeDtypeStruct((n, 2 * D_MODEL), F32)],
        compiler_params=_params(1),
        name="in_proj",
    )(x, g_mix, w_in_bf)


def _dup_kv_heads(x):
    lo = lax.broadcasted_iota(jnp.int32, x.shape, 1) < HEAD_DIM
    r = pltpu.roll(x, HEAD_DIM, axis=1)
    return [jnp.where(lo, x, r).astype(BF16), jnp.where(lo, r, x).astype(BF16)]


def _stack_heads(q, g):
    lo = lax.broadcasted_iota(jnp.int32, (q.shape[0], LANES), 1) < HEAD_DIM
    scale_lo = jnp.where(lo, LOGIT_SCALE, 0.0).astype(F32)
    scale_hi = jnp.where(lo, 0.0, LOGIT_SCALE).astype(F32)
    base = g * GQA_GROUP * HEAD_DIM
    p0 = q[:, base:base + LANES]
    p1 = q[:, base + LANES:base + 2 * LANES]
    return jnp.concatenate([p0 * scale_lo, p0 * scale_hi, p1 * scale_lo, p1 * scale_hi], axis=0)


def _attn_scores(q, kdup, g, bias4):
    return lax.dot_general(_stack_heads(q, g).astype(BF16), kdup, (((1,), (1,)), ((), ())),
                           preferred_element_type=F32) + bias4


def _sink_column(r, g, sinks_ref):
    row4 = lax.broadcasted_iota(jnp.int32, (GQA_GROUP * r, 1), 0)
    sk = jnp.full((GQA_GROUP * r, 1), sinks_ref[g * GQA_GROUP + GQA_GROUP - 1], F32)
    for hh in range(GQA_GROUP - 2, -1, -1):
        sk = jnp.where(row4 < (hh + 1) * r, sinks_ref[g * GQA_GROUP + hh], sk)
    return sk * LOG2_E


def _sink_softmax(s, sk):
    m = jnp.maximum(jnp.max(s, axis=-1, keepdims=True), sk)
    e = jnp.exp2(s - m)
    den = jnp.sum(e, axis=-1, keepdims=True) + jnp.exp2(sk - m)
    return e, 1.0 / den


def _unstack_heads(o):
    r = o.shape[0] // GQA_GROUP
    lo = lax.broadcasted_iota(jnp.int32, (r, LANES), 1) < HEAD_DIM
    return jnp.concatenate([jnp.where(lo, o[0:r], o[r:2 * r]),
                            jnp.where(lo, o[2 * r:3 * r], o[3 * r:4 * r])], axis=1)


def _software_pipeline(n_units, stages, after_tick=()):
    vals = {}
    for tick in range(n_units + len(stages) - 1):
        for k in reversed(range(len(stages))):
            u = tick - k
            if 0 <= u < n_units:
                vals[u] = stages[k](u, vals.get(u))
        if tick < len(after_tick) and after_tick[tick] is not None:
            after_tick[tick]()


def _band_bias4(r, s, first_key=0):
    i = lax.broadcasted_iota(jnp.int32, (GQA_GROUP * r, s), 0) & (r - 1)
    j = lax.broadcasted_iota(jnp.int32, (GQA_GROUP * r, s), 1)
    valid = (j >= jnp.maximum(i, first_key)) & (j <= i + WINDOW)
    return jnp.where(valid, 0.0, -jnp.inf).astype(F32)


def _window_sums(e, w):
    step = 1
    while step < w:
        e = e + pltpu.roll(e, step, axis=0)
        step *= 2
    return e


def _gate_merge(pa, po, gl):
    ga = jax.nn.sigmoid(gl[:, :D_MODEL])
    gb = jax.nn.sigmoid(gl[:, D_MODEL:])
    return (ga * pa + gb * po).astype(BF16)


def _merge(x, pool_a, attn_o, gl, wbp_ref, wba_ref, wout_ref):
    pa = _dot(pool_a.astype(BF16), wbp_ref[...])
    po = _dot(attn_o.astype(BF16), wba_ref[...])
    return x + _dot(_gate_merge(pa, po, gl), wout_ref[...])


def _mix_prompt_kernel(sinks_ref, u_ref, q_ref, k_ref, v_ref, gl_ref, wpool_ref, pscale_ref,
                       wbp_ref, wba_ref, wout_ref, xo_ref, o_ref, kt_ref, vt_ref, uext, kdup, vdup,
                       oscr, bias, pa_scr, mixed, *, tq, nt):
    s = pl.program_id(0)
    i = s % nt
    pre = POOL_STATE + 1

    @pl.when(s == 0)
    def _():
        mixed[...] = jnp.zeros(mixed.shape, BF16)

    @pl.when(i == 0)
    def _():
        uext[0:pre, :] = jnp.zeros((pre, POOL_WIDTH), F32)
        kdup[:, 0:WINDOW, :] = jnp.zeros((N_KV_HEADS, WINDOW, LANES), BF16)
        vdup[:, 0:WINDOW, :] = jnp.zeros((N_KV_HEADS, WINDOW, LANES), BF16)
        bias[0] = _band_bias4(WINDOW, 2 * WINDOW, WINDOW)
        bias[1] = _band_bias4(WINDOW, 2 * WINDOW)

    uext[pre:pre + tq, :] = u_ref[...]
    for dst, src in ((kdup, k_ref), (vdup, v_ref)):
        for g, d in enumerate(_dup_kv_heads(src[...])):
            dst[g, WINDOW:WINDOW + tq, :] = d
    sink_cols = [_sink_column(WINDOW, g, sinks_ref) for g in range(N_KV_HEADS)]

    piece = D_MODEL // 4

    def out_prev(n):
        cols = slice(n * piece, (n + 1) * piece)
        o_ref[:, cols] = xo_ref[:, cols] + _dot(mixed[...], wout_ref[:, cols])

    pos1 = i * tq + lax.broadcasted_iota(jnp.int32, (tq, 1), 0) + 1
    a_parts = []

    def pool_group(g):
        w = POOL_WINDOWS[g]
        cols = slice(g * POOL_GROUP, (g + 1) * POOL_GROUP)
        e = uext[:, cols]
        inv_cnt = 1.0 / jnp.minimum(pos1, w).astype(F32)
        d = _window_sums(e, w)[pre:] * inv_cnt - e[pre:]
        y = _dot(d.astype(BF16), wpool_ref[g])
        a_parts.append((y * pscale_ref[:, cols]).astype(BF16))

    n_units = N_KV_HEADS * tq // WINDOW

    def pooled_branch(n):
        cols = slice(n * 2 * piece, (n + 1) * 2 * piece)
        pa_scr[:, cols] = _dot(jnp.concatenate(a_parts, axis=1), wbp_ref[:, cols])

    def early(n):
        out_prev(n)
        pool_group(n)

    def merge_half(half):
        rows = slice(half * tq // 2, (half + 1) * tq // 2)
        po = _dot(oscr[rows, :].astype(BF16), wba_ref[...])
        mixed[rows, :] = _gate_merge(pa_scr[rows, :], po, gl_ref[rows, :])

    half_done = n_units // 2 + 1
    after_tick = [functools.partial(early, n) for n in range(4)]
    after_tick += [functools.partial(pooled_branch, n) for n in range(2)]
    assert len(after_tick) <= half_done + 1
    after_tick += [None] * (half_done + 1 - len(after_tick)) + [functools.partial(merge_half, 0)]

    def unit(u):
        nb, g = divmod(u, N_KV_HEADS)
        return nb, g, slice(nb * WINDOW, (nb + 1) * WINDOW), slice(nb * WINDOW, (nb + 2) * WINDOW)

    def scores(u, _):
        nb, g, rows, keys = unit(u)
        slot = jnp.minimum(i * (tq // WINDOW) + nb, 1)
        return _attn_scores(q_ref[rows, :], kdup[g, keys, :], g, bias[slot])

    def softmax(u, s):
        e, inv = _sink_softmax(s, sink_cols[unit(u)[1]])
        return e.astype(BF16), inv

    def values(u, weights):
        nb, g, rows, keys = unit(u)
        e, inv = weights
        oscr[rows, g * 2 * LANES:(g + 1) * 2 * LANES] = _unstack_heads(
            _dot(e, vdup[g, keys, :]) * inv)

    _software_pipeline(n_units, [scores, softmax, values], after_tick)
    merge_half(1)

    @pl.when(i == nt - 1)
    def _():
        kt_ref[0] = k_ref[tq - WINDOW:tq, :].T
        vt_ref[0] = v_ref[tq - WINDOW:tq, :].T

    uext[0:pre, :] = uext[tq:tq + pre, :]
    kdup[:, 0:WINDOW, :] = kdup[:, tq:tq + WINDOW, :]
    vdup[:, 0:WINDOW, :] = vdup[:, tq:tq + WINDOW, :]


def _mix_prompt(x, u, q, k, v, gl, sinks, wpool_bf, pscale, wbp_bf, wba_bf, wout_bf, batch, tq):
    n = x.shape[0]
    n_tiles = n // tq
    nt = n_tiles // batch
    row = lambda c: pl.BlockSpec((tq, c), lambda s: (jnp.minimum(s, n_tiles - 1), 0))
    row_prev = pl.BlockSpec((tq, D_MODEL), lambda s: (jnp.maximum(s - 1, 0), 0))
    last_t = pl.BlockSpec((1, KV_WIDTH, WINDOW),
                          lambda s: (jnp.minimum(s, n_tiles - 1) // nt, 0, 0))
    return pl.pallas_call(
        functools.partial(_mix_prompt_kernel, tq=tq, nt=nt),
        grid=(n_tiles + 1,),
        in_specs=[pl.BlockSpec(memory_space=pltpu.SMEM),
                  row(POOL_WIDTH), row(ATTN_WIDTH), row(KV_WIDTH), row(KV_WIDTH), row(2 * D_MODEL),
                  _const_spec(wpool_bf.shape), _const_spec(pscale.shape),
                  _const_spec(wbp_bf.shape), _const_spec(wba_bf.shape), _const_spec(wout_bf.shape),
                  row_prev],
        out_specs=[row_prev, last_t, last_t],
        out_shape=[jax.ShapeDtypeStruct((n, D_MODEL), F32),
                   jax.ShapeDtypeStruct((batch, KV_WIDTH, WINDOW), F32),
                   jax.ShapeDtypeStruct((batch, KV_WIDTH, WINDOW), F32)],
        scratch_shapes=[pltpu.VMEM((POOL_STATE + 1 + tq, POOL_WIDTH), F32),
                        pltpu.VMEM((N_KV_HEADS, WINDOW + tq, LANES), BF16),
                        pltpu.VMEM((N_KV_HEADS, WINDOW + tq, LANES), BF16),
                        pltpu.VMEM((tq, ATTN_WIDTH), F32),
                        pltpu.VMEM((2, GQA_GROUP * WINDOW, 2 * WINDOW), F32),
                        pltpu.VMEM((tq, D_MODEL), F32),
                        pltpu.VMEM((tq, D_MODEL), BF16)],
        compiler_params=_params(1),
        name="mix_prompt",
    )(sinks, u, q, k, v, gl, wpool_bf, pscale, wbp_bf, wba_bf, wout_bf, x)


def _tile_shift(cur, prev, j, row):
    return jnp.where(row < j, pltpu.roll(prev, j, axis=1), pltpu.roll(cur, j, axis=1))


SEQ_BLOCK = WINDOW // SUBLANES


def _sample_bias4():
    shape = (GQA_GROUP * WINDOW, 2 * WINDOW)
    i = lax.broadcasted_iota(jnp.int32, shape, 0) & (WINDOW - 1)
    j = lax.broadcasted_iota(jnp.int32, shape, 1)
    b, t = i >> 3, i & (SUBLANES - 1)
    new = j - WINDOW
    valid = ((j < WINDOW) & (j >= t)) | ((new >> 3 == b) & ((new & (SUBLANES - 1)) <= t) & (new >= 0))
    return jnp.where(valid, 0.0, -jnp.inf).astype(F32)


def _attend_cached(q, knew, vnew, ckt_ref, cvt_ref, b0, bias4, sinks_ref):
    r, t = WINDOW, SUBLANES
    nt_dims = (((1,), (1,)), ((), ()))
    kn, vn = _dup_kv_heads(knew), _dup_kv_heads(vnew)

    def per_seq(x):
        return [jnp.concatenate([x[j * r + b * t:j * r + (b + 1) * t] for j in range(GQA_GROUP)],
                                axis=0) for b in range(SEQ_BLOCK)]

    def stacked(xs):
        return jnp.concatenate([xs[b][j * t:(j + 1) * t] for j in range(GQA_GROUP)
                                for b in range(SEQ_BLOCK)], axis=0)

    outs = []
    for g in range(N_KV_HEADS):
        kv_rows = slice(g * HEAD_DIM, (g + 1) * HEAD_DIM)
        qs = _stack_heads(q, g)
        s_new = lax.dot_general(qs.astype(BF16), kn[g], nt_dims, preferred_element_type=F32)
        s_cache = []
        for b, qb in enumerate(per_seq(qs)):
            kt = ckt_ref[b0 + b, kv_rows, :]
            s_cache.append(_dot(qb.astype(BF16), jnp.concatenate([kt, kt], axis=0).astype(BF16)))
        s = jnp.concatenate([stacked(s_cache), s_new], axis=1) + bias4
        p, inv = _sink_softmax(s, _sink_column(r, g, sinks_ref))
        o_cache = []
        for b, pb in enumerate(per_seq(p[:, :r])):
            vt = cvt_ref[b0 + b, kv_rows, :]
            o_cache.append(lax.dot_general(pb.astype(BF16),
                                           jnp.concatenate([vt, vt], axis=0).astype(BF16),
                                           nt_dims, preferred_element_type=F32))
        o = stacked(o_cache) + _dot(p[:, r:].astype(BF16), vn[g])
        outs.append(_unstack_heads(o * inv))
    return jnp.concatenate(outs, axis=1)


def _slide_cache(new_rows, ct_ref, out_ref, b0):
    t = SUBLANES
    new_t = new_rows.T
    is_new = lax.broadcasted_iota(jnp.int32, (KV_WIDTH, WINDOW), 1) >= WINDOW - t
    for b in range(SEQ_BLOCK):
        kept = pltpu.roll(ct_ref[b0 + b], WINDOW - t, axis=1)
        fresh = pltpu.roll(new_t, (WINDOW - t - b * t) % WINDOW, axis=1)
        out_ref[b0 + b] = jnp.where(is_new, fresh, kept)


def _mix_sample_kernel(sinks_ref, x_ref, u_ref, st_ref, q_ref, k_ref, v_ref, ckt_ref, cvt_ref, gl_ref,
                       wpool_ref, pscale_ref, wbp_ref, wba_ref, wout_ref, o_ref, kst_ref, vst_ref,
                       oscr, *, gs, pos0):
    t = SUBLANES
    rows = gs * t

    row = lax.broadcasted_iota(jnp.int32, (gs, t, POOL_GROUP), 1)
    cnt_pos = pos0 + lax.broadcasted_iota(jnp.int32, (gs, t, 1), 1) + 1
    a_parts = []
    for g, w in enumerate(POOL_WINDOWS):
        cols = slice(g * POOL_GROUP, (g + 1) * POOL_GROUP)
        tiles = [st_ref[:, 0:t, cols], st_ref[:, t:2 * t, cols], u_ref[:, :, cols]]
        xt = tiles[2]
        step = 1
        while step < w and step < t:
            tiles = [tiles[n] + _tile_shift(tiles[n], tiles[max(n - 1, 0)], step, row)
                     for n in range(3)]
            step *= 2
        acc = tiles[2] + tiles[1] if w == 2 * t else tiles[2]
        cnt = jnp.minimum(cnt_pos, w).astype(F32)
        d = (acc / cnt - xt).reshape(rows, POOL_GROUP)
        y = _dot(d.astype(BF16), wpool_ref[g])
        a_parts.append(y * pscale_ref[:, cols])
    pool_a = jnp.concatenate(a_parts, axis=1)

    bias = _sample_bias4()
    for blk in range(gs // SEQ_BLOCK):
        rs = slice(blk * WINDOW, (blk + 1) * WINDOW)
        oscr[rs, :] = _attend_cached(q_ref[rs, :], k_ref[rs, :], v_ref[rs, :], ckt_ref, cvt_ref,
                                     blk * SEQ_BLOCK, bias, sinks_ref)
        _slide_cache(k_ref[rs, :], ckt_ref, kst_ref, blk * SEQ_BLOCK)
        _slide_cache(v_ref[rs, :], cvt_ref, vst_ref, blk * SEQ_BLOCK)

    o_ref[...] = _merge(x_ref[...], pool_a, oscr[...], gl_ref[...], wbp_ref, wba_ref, wout_ref)


def _mix_sample(x, u3, stpad, q, k, v, ck, cv, gl, sinks, wpool_bf, pscale, wbp_bf, wba_bf, wout_bf,
                gs, pos0):
    nseq, t = u3.shape[0], u3.shape[1]
    rows = gs * t
    row = lambda c: pl.BlockSpec((rows, c), lambda i: (i, 0))
    seq = lambda a, b: pl.BlockSpec((gs, a, b), lambda i: (i, 0, 0))
    return pl.pallas_call(
        functools.partial(_mix_sample_kernel, gs=gs, pos0=pos0),
        grid=(nseq // gs,),
        in_specs=[pl.BlockSpec(memory_space=pltpu.SMEM),
                  row(D_MODEL), seq(t, POOL_WIDTH), seq(2 * t, POOL_WIDTH),
                  row(ATTN_WIDTH), row(KV_WIDTH), row(KV_WIDTH),
                  seq(WINDOW, KV_WIDTH), seq(WINDOW, KV_WIDTH), row(2 * D_MODEL),
                  _const_spec(wpool_bf.shape), _const_spec(pscale.shape),
                  _const_spec(wbp_bf.shape), _const_spec(wba_bf.shape), _const_spec(wout_bf.shape)],
        out_specs=[row(D_MODEL), seq(KV_WIDTH, WINDOW), seq(KV_WIDTH, WINDOW)],
        out_shape=[jax.ShapeDtypeStruct((nseq * t, D_MODEL), F32),
                   jax.ShapeDtypeStruct((nseq, KV_WIDTH, WINDOW), F32),
                   jax.ShapeDtypeStruct((nseq, KV_WIDTH, WINDOW), F32)],
        scratch_shapes=[pltpu.VMEM((rows, ATTN_WIDTH), F32)],
        compiler_params=_params(1),
        name="mix_sample",
    )(sinks, x, u3, stpad, q, k, v, ck, cv, gl, wpool_bf, pscale, wbp_bf, wba_bf, wout_bf)


FF_CHUNK = 256


def _ff_cols(c, part):
    start = part * D_FF + c * FF_CHUNK
    return slice(start, start + FF_CHUNK)


def _gated_act(taps, cw_ref, cb_ref, c):
    def conv(part):
        x0, x1, x2 = taps(c, part)
        cols = _ff_cols(c, part)
        return (x0 * cw_ref[0:1, cols] + x1 * cw_ref[1:2, cols] + x2 * cw_ref[2:3, cols]
                + cb_ref[:, cols])
    return (jax.nn.silu(conv(0)) * conv(1)).astype(BF16)


def _ffn_prompt_kernel(x_ref, g_ref, wup_ref, cw_ref, cb_ref, wdn_ref, gf_ref, o_ref, cs_ref,
                       upx, act, *, tm):
    i = pl.program_id(1)
    pre = SUBLANES

    @pl.when(i == 0)
    def _():
        upx[0:pre, :] = jnp.zeros((pre, 2 * D_FF), F32)

    x = x_ref[...]
    h = _rmsnorm(x, g_ref[...]).astype(BF16)
    upx[pre:pre + tm, :] = _dot(h, wup_ref[...])

    def taps(c, part):
        e = upx[:, _ff_cols(c, part)]
        return pltpu.roll(e, 2, axis=0)[pre:], pltpu.roll(e, 1, axis=0)[pre:], e[pre:]

    for c in range(D_FF // FF_CHUNK):
        act[:, c * FF_CHUNK:(c + 1) * FF_CHUNK] = _gated_act(taps, cw_ref, cb_ref, c)

    y = x + _dot(act[...], wdn_ref[...])
    o_ref[...] = _rmsnorm(y, gf_ref[...])
    cs_ref[0] = upx[pre + tm - CONV_STATE:pre + tm, :]
    upx[0:pre, :] = upx[tm:tm + pre, :]


def _ffn_prompt(x, g_ffn, wup_bf, conv_w, conv_b, wdn_bf, g_final, batch, tm):
    n = x.shape[0]
    nt = n // batch // tm
    row = pl.BlockSpec((tm, D_MODEL), lambda b, i: (b * nt + i, 0))
    return pl.pallas_call(
        functools.partial(_ffn_prompt_kernel, tm=tm),
        grid=(batch, nt),
        in_specs=[row, _const_spec((1, D_MODEL)), _const_spec(wup_bf.shape),
                  _const_spec(conv_w.shape), _const_spec(conv_b.shape), _const_spec(wdn_bf.shape),
                  _const_spec((1, D_MODEL))],
        out_specs=[row, pl.BlockSpec((1, CONV_STATE, 2 * D_FF), lambda b, i: (b, 0, 0))],
        out_shape=[jax.ShapeDtypeStruct((n, D_MODEL), F32),
                   jax.ShapeDtypeStruct((batch, CONV_STATE, 2 * D_FF), F32)],
        scratch_shapes=[pltpu.VMEM((SUBLANES + tm, 2 * D_FF), F32),
                        pltpu.VMEM((tm, D_FF), BF16)],
        compiler_params=_params(2),
        name="ffn_prompt",
    )(x, g_ffn, wup_bf, conv_w, conv_b, wdn_bf, g_final)


def _ffn_sample_kernel(x_ref, st_ref, g_ref, wup_ref, cw_ref, cb_ref, wdn_ref, gf_ref,
                       o_ref, cs_ref, up, act, *, gs):
    t = SUBLANES
    rows = gs * t
    x = x_ref[...]
    h = _rmsnorm(x, g_ref[...]).astype(BF16)
    up[...] = _dot(h, wup_ref[...])
    row = lax.broadcasted_iota(jnp.int32, (gs, t, FF_CHUNK), 1)

    def taps(c, part):
        cols = _ff_cols(c, part)
        x2 = up[:, cols].reshape(gs, t, FF_CHUNK)
        s0 = jnp.broadcast_to(st_ref[:, 0:1, cols], x2.shape)
        s1 = jnp.broadcast_to(st_ref[:, 1:2, cols], x2.shape)
        x1 = jnp.where(row < 1, s1, pltpu.roll(x2, 1, axis=1))
        x0 = jnp.where(row < 1, s0, jnp.where(row < 2, s1, pltpu.roll(x2, 2, axis=1)))
        flat = lambda a: a.reshape(rows, FF_CHUNK)
        return flat(x0), flat(x1), flat(x2)

    for c in range(D_FF // FF_CHUNK):
        act[:, c * FF_CHUNK:(c + 1) * FF_CHUNK] = _gated_act(taps, cw_ref, cb_ref, c)

    y = x + _dot(act[...], wdn_ref[...])
    o_ref[...] = _rmsnorm(y, gf_ref[...])
    for c0 in range(0, 2 * D_FF, 4 * FF_CHUNK):
        cols = slice(c0, min(c0 + 4 * FF_CHUNK, 2 * D_FF))
        cs_ref[:, :, cols] = up[:, cols].reshape(gs, t, -1)[:, t - CONV_STATE:, :]


def _ffn_sample(x, st, g_ffn, wup_bf, conv_w, conv_b, wdn_bf, g_final, gs):
    nseq = st.shape[0]
    t = x.shape[0] // nseq
    rows = gs * t
    row = pl.BlockSpec((rows, D_MODEL), lambda i: (i, 0))
    st_spec = pl.BlockSpec((gs, CONV_STATE, 2 * D_FF), lambda i: (i, 0, 0))
    return pl.pallas_call(
        functools.partial(_ffn_sample_kernel, gs=gs),
        grid=(nseq // gs,),
        in_specs=[row, st_spec,
                  _const_spec((1, D_MODEL)), _const_spec(wup_bf.shape),
                  _const_spec(conv_w.shape), _const_spec(conv_b.shape), _const_spec(wdn_bf.shape),
                  _const_spec((1, D_MODEL))],
        out_specs=[row, st_spec],
        out_shape=[jax.ShapeDtypeStruct((nseq * t, D_MODEL), F32),
                   jax.ShapeDtypeStruct((nseq, CONV_STATE, 2 * D_FF), F32)],
        scratch_shapes=[pltpu.VMEM((rows, 2 * D_FF), F32), pltpu.VMEM((rows, D_FF), BF16)],
        compiler_params=_params(1),
        name="ffn_sample",
    )(x, st, g_ffn, wup_bf, conv_w, conv_b, wdn_bf, g_final)


TM_IN = 1024
TQ_MIX = 512
TM_FFN = 512
GS_MIX = 32
GS_FFN = 64


def kernel(x_prompt, x_sample, state_pool, cache_k, cache_v, state_conv, g_mix, w_in, w_pool_grp,
           pool_scale, sinks, w_branch_pool, w_branch_attn, w_out, g_ffn, w_up, conv_w, conv_b,
           w_down, g_final):
    depth = g_mix.shape[0]
    batch, seq, _ = x_prompt.shape
    dec_batch, dec_seq, _ = x_sample.shape
    assert dec_seq == SUBLANES and seq % TQ_MIX == 0 and seq % TM_FFN == 0
    assert depth == 1

    yp = x_prompt.reshape(batch * seq, D_MODEL)
    ys = x_sample.reshape(dec_batch * dec_seq, D_MODEL)
    outs = [[] for _ in range(8)]
    gfin = g_final.reshape(1, D_MODEL)
    for l in range(depth):
        gmix = g_mix[l].reshape(1, D_MODEL)
        gffn = g_ffn[l].reshape(1, D_MODEL)
        w_in_bf = w_in[l].astype(BF16)
        wpool_bf = w_pool_grp[l].astype(BF16)
        pscale = pool_scale[l].reshape(1, POOL_WIDTH)
        wbp_bf = w_branch_pool[l].astype(BF16)
        wba_bf = w_branch_attn[l].astype(BF16)
        wout_bf = w_out[l].astype(BF16)
        wup_bf = w_up[l].astype(BF16)
        wdn_bf = w_down[l].astype(BF16)
        cw = conv_w[l]
        cb = conv_b[l].reshape(1, 2 * D_FF)

        u, q, k, v, gl = _in_proj(yp, gmix, w_in_bf, TM_IN)
        x1, kt, vt = _mix_prompt(yp, u, q, k, v, gl, sinks[l], wpool_bf, pscale, wbp_bf, wba_bf,
                                 wout_bf, batch, TQ_MIX)
        yp, conv_p = _ffn_prompt(x1, gffn, wup_bf, cw, cb, wdn_bf, gfin, batch, TM_FFN)
        outs[0].append(u.reshape(batch, seq, POOL_WIDTH)[:, seq - POOL_STATE:])
        untranspose = lambda t: t.reshape(-1, N_KV_HEADS, HEAD_DIM, WINDOW).transpose(0, 3, 1, 2)
        outs[1].append(untranspose(kt))
        outs[2].append(untranspose(vt))
        outs[3].append(conv_p)

        us, qs, ks, vs, gls = _in_proj(ys, gmix, w_in_bf, TM_IN)
        u3 = us.reshape(dec_batch, dec_seq, POOL_WIDTH)
        stpad = jnp.pad(state_pool[l], ((0, 0), (1, 0), (0, 0)))
        transposed = lambda c: c.transpose(0, 2, 3, 1).reshape(dec_batch, KV_WIDTH, WINDOW)
        ck, cv = transposed(cache_k[l]), transposed(cache_v[l])
        x1s, kst, vst = _mix_sample(ys, u3, stpad, qs, ks, vs, ck, cv, gls, sinks[l], wpool_bf,
                                    pscale, wbp_bf, wba_bf, wout_bf, GS_MIX, PAST_LEN)
        ys, conv_s = _ffn_sample(x1s, state_conv[l], gffn, wup_bf, cw, cb, wdn_bf, gfin, GS_FFN)
        outs[4].append(jnp.concatenate([state_pool[l], u3], axis=1)[:, dec_seq:])
        outs[5].append(untranspose(kst))
        outs[6].append(untranspose(vst))
        outs[7].append(conv_s)
    return (yp.reshape(batch, seq, D_MODEL), ys.reshape(dec_batch, dec_seq, D_MODEL),
            *[jnp.stack(o) for o in outs])
```

```python
import functools

import jax
import jax.numpy as jnp
from jax import lax
from jax.experimental import pallas as pl
from jax.experimental.pallas import tpu as pltpu

D_MODEL = 1024
POOL_WIDTH = 512
POOL_WINDOWS = (2, 4, 8, 16)
POOL_GROUP = 128
POOL_STATE = 15
HEAD_DIM = 64
N_HEADS = 8
N_KV_HEADS = 2
GQA_GROUP = N_HEADS // N_KV_HEADS
WINDOW = 128
ATTN_WIDTH = N_HEADS * HEAD_DIM
KV_WIDTH = N_KV_HEADS * HEAD_DIM
ATTN_SCALE = HEAD_DIM ** -0.5
LOG2_E = 1.4426950408889634
LOGIT_SCALE = ATTN_SCALE * LOG2_E
D_FF = 2816
CONV_STATE = 2
EPS = 1e-6
PAST_LEN = 16384

LANES = 128
SUBLANES = 8
VMEM_LIMIT = 56 * 1024 * 1024

F32 = jnp.float32
BF16 = jnp.bfloat16


def _rmsnorm(x, g):
    inv = lax.rsqrt(jnp.mean(x * x, axis=-1, keepdims=True) + EPS)
    return (x * inv) * g


def _dot(a, b):
    return jnp.dot(a, b, preferred_element_type=F32)


def _const_spec(shape):
    nd = len(shape)
    return pl.BlockSpec(shape, lambda *_: (0,) * nd, pipeline_mode=pl.Buffered(1))


def _params(n_axes):
    return pltpu.CompilerParams(dimension_semantics=("arbitrary",) * n_axes,
                                vmem_limit_bytes=VMEM_LIMIT)


def _in_proj_kernel(x_ref, g_ref, w_ref, u_ref, q_ref, k_ref, v_ref, gate_ref):
    h = _rmsnorm(x_ref[...], g_ref[...]).astype(BF16)
    o_q = POOL_WIDTH
    o_k = o_q + ATTN_WIDTH
    o_gl = o_k + 2 * KV_WIDTH
    gate_ref[...] = jax.nn.sigmoid(_dot(h, w_ref[:, o_gl:]))
    u_ref[...] = _dot(h, w_ref[:, 0:o_q])
    q_ref[...] = _dot(h, w_ref[:, o_q:o_k])
    kv = _dot(h, w_ref[:, o_k:o_gl])
    k_ref[...] = kv[:, :KV_WIDTH]
    v_ref[...] = kv[:, KV_WIDTH:]


def _in_proj(x, g_mix, w_in_bf, tm):
    n = x.shape[0]
    in_width = w_in_bf.shape[1]
    row = lambda c: pl.BlockSpec((tm, c), lambda i: (i, 0))
    return pl.pallas_call(
        _in_proj_kernel,
        grid=(n // tm,),
        in_specs=[row(D_MODEL), _const_spec((1, D_MODEL)), _const_spec((D_MODEL, in_width))],
        out_specs=[row(POOL_WIDTH), row(ATTN_WIDTH), row(KV_WIDTH), row(KV_WIDTH), row(2 * D_MODEL)],
        out_shape=[jax.ShapeDtypeStruct((n, POOL_WIDTH), F32),
                   jax.ShapeDtypeStruct((n, ATTN_WIDTH), F32),
                   jax.ShapeDtypeStruct((n, KV_WIDTH), F32),
                   jax.ShapeDtypeStruct((n, KV_WIDTH), F32),
                   jax.ShapeDtypeStruct((n, 2 * D_MODEL), F32)],
        compiler_params=_params(1),
        name="in_proj",
    )(x, g_mix, w_in_bf)


def _dup_kv_heads(x):
    lo = lax.broadcasted_iota(jnp.int32, x.shape, 1) < HEAD_DIM
    r = pltpu.roll(x, HEAD_DIM, axis=1)
    return [jnp.where(lo, x, r).astype(BF16), jnp.where(lo, r, x).astype(BF16)]


def _stack_heads(q, g):
    lo = lax.broadcasted_iota(jnp.int32, (q.shape[0], LANES), 1) < HEAD_DIM
    scale_lo = jnp.where(lo, LOGIT_SCALE, 0.0).astype(F32)
    scale_hi = jnp.where(lo, 0.0, LOGIT_SCALE).astype(F32)
    base = g * GQA_GROUP * HEAD_DIM
    p0 = q[:, base:base + LANES]
    p1 = q[:, base + LANES:base + 2 * LANES]
    return jnp.concatenate([p0 * scale_lo, p0 * scale_hi, p1 * scale_lo, p1 * scale_hi], axis=0)


def _attn_scores(q, kdup, g, bias4):
    return lax.dot_general(_stack_heads(q, g).astype(BF16), kdup, (((1,), (1,)), ((), ())),
                           preferred_element_type=F32) + bias4


def _sink_column(r, g, sinks_ref):
    row4 = lax.broadcasted_iota(jnp.int32, (GQA_GROUP * r, 1), 0)
    sk = jnp.full((GQA_GROUP * r, 1), sinks_ref[g * GQA_GROUP + GQA_GROUP - 1], F32)
    for hh in range(GQA_GROUP - 2, -1, -1):
        sk = jnp.where(row4 < (hh + 1) * r, sinks_ref[g * GQA_GROUP + hh], sk)
    return sk * LOG2_E


def _sink_softmax(s, sk):
    m = jnp.maximum(jnp.max(s, axis=-1, keepdims=True), sk)
    e = jnp.exp2(s - m)
    den = jnp.sum(e, axis=-1, keepdims=True) + jnp.exp2(sk - m)
    return e, 1.0 / den


def _unstack_heads(o):
    r = o.shape[0] // GQA_GROUP
    lo = lax.broadcasted_iota(jnp.int32, (r, LANES), 1) < HEAD_DIM
    return jnp.concatenate([jnp.where(lo, o[0:r], o[r:2 * r]),
                            jnp.where(lo, o[2 * r:3 * r], o[3 * r:4 * r])], axis=1)


def _software_pipeline(n_units, stages, after_tick=()):
    vals = {}
    for tick in range(n_units + len(stages) - 1):
        for k in reversed(range(len(stages))):
            u = tick - k
            if 0 <= u < n_units:
                vals[u] = stages[k](u, vals.get(u))
        if tick < len(after_tick) and after_tick[tick] is not None:
            after_tick[tick]()


def _band_bias4(r, s, first_key=0):
    i = lax.broadcasted_iota(jnp.int32, (GQA_GROUP * r, s), 0) & (r - 1)
    j = lax.broadcasted_iota(jnp.int32, (GQA_GROUP * r, s), 1)
    valid = (j >= jnp.maximum(i, first_key)) & (j <= i + WINDOW)
    return jnp.where(valid, 0.0, -jnp.inf).astype(F32)


def _window_sums(e, w):
    step = 1
    while step < w:
        e = e + pltpu.roll(e, step, axis=0)
        step *= 2
    return e


def _gate_merge(pa, po, gates):
    return (gates[:, :D_MODEL] * pa + gates[:, D_MODEL:] * po).astype(BF16)


def _merge(x, pool_a, attn_o, gl, wbp_ref, wba_ref, wout_ref):
    pa = _dot(pool_a.astype(BF16), wbp_ref[...])
    po = _dot(attn_o.astype(BF16), wba_ref[...])
    return x + _dot(_gate_merge(pa, po, gl), wout_ref[...])


def _mix_prompt_kernel(sinks_ref, u_ref, q_ref, k_ref, v_ref, gl_ref, wpool_ref, pscale_ref,
                       wbp_ref, wba_ref, wout_ref, xo_ref, o_ref, kt_ref, vt_ref, uext, kdup, vdup,
                       oscr, bias, pa_scr, mixed, *, tq, nt):
    s = pl.program_id(0)
    i = s % nt
    pre = POOL_STATE + 1

    @pl.when(s == 0)
    def _():
        mixed[...] = jnp.zeros(mixed.shape, BF16)

    @pl.when(i == 0)
    def _():
        uext[0:pre, :] = jnp.zeros((pre, POOL_WIDTH), F32)
        kdup[:, 0:WINDOW, :] = jnp.zeros((N_KV_HEADS, WINDOW, LANES), BF16)
        vdup[:, 0:WINDOW, :] = jnp.zeros((N_KV_HEADS, WINDOW, LANES), BF16)
        bias[0] = _band_bias4(WINDOW, 2 * WINDOW, WINDOW)
        bias[1] = _band_bias4(WINDOW, 2 * WINDOW)

    uext[pre:pre + tq, :] = u_ref[...]
    for dst, src in ((kdup, k_ref), (vdup, v_ref)):
        for g, d in enumerate(_dup_kv_heads(src[...])):
            dst[g, WINDOW:WINDOW + tq, :] = d
    sink_cols = [_sink_column(WINDOW, g, sinks_ref) for g in range(N_KV_HEADS)]

    piece = D_MODEL // 4

    def out_prev(n):
        cols = slice(n * piece, (n + 1) * piece)
        o_ref[:, cols] = xo_ref[:, cols] + _dot(mixed[...], wout_ref[:, cols])

    pos1 = i * tq + lax.broadcasted_iota(jnp.int32, (tq, 1), 0) + 1
    a_parts = []

    def pool_group(g):
        w = POOL_WINDOWS[g]
        cols = slice(g * POOL_GROUP, (g + 1) * POOL_GROUP)
        e = uext[:, cols]
        inv_cnt = 1.0 / jnp.minimum(pos1, w).astype(F32)
        d = _window_sums(e, w)[pre:] * inv_cnt - e[pre:]
        y = _dot(d.astype(BF16), wpool_ref[g])
        a_parts.append((y * pscale_ref[:, cols]).astype(BF16))

    n_units = N_KV_HEADS * tq // WINDOW

    def pooled_branch(n):
        cols = slice(n * 2 * piece, (n + 1) * 2 * piece)
        pa_scr[:, cols] = _dot(jnp.concatenate(a_parts, axis=1), wbp_ref[:, cols])

    def early(n):
        out_prev(n)
        pool_group(n)

    def merge_half(half):
        rows = slice(half * tq // 2, (half + 1) * tq // 2)
        po = _dot(oscr[rows, :].astype(BF16), wba_ref[...])
        mixed[rows, :] = _gate_merge(pa_scr[rows, :], po, gl_ref[rows, :])

    half_done = n_units // 2 + 1
    after_tick = [functools.partial(early, n) for n in range(4)]
    after_tick += [functools.partial(pooled_branch, n) for n in range(2)]
    assert len(after_tick) <= half_done + 1
    after_tick += [None] * (half_done + 1 - len(after_tick)) + [functools.partial(merge_half, 0)]

    def unit(u):
        nb, g = divmod(u, N_KV_HEADS)
        return nb, g, slice(nb * WINDOW, (nb + 1) * WINDOW), slice(nb * WINDOW, (nb + 2) * WINDOW)

    def scores(u, _):
        nb, g, rows, keys = unit(u)
        slot = jnp.minimum(i * (tq // WINDOW) + nb, 1)
        return _attn_scores(q_ref[rows, :], kdup[g, keys, :], g, bias[slot])

    def softmax(u, s):
        e, inv = _sink_softmax(s, sink_cols[unit(u)[1]])
        return e.astype(BF16), inv

    def values(u, weights):
        nb, g, rows, keys = unit(u)
        e, inv = weights
        oscr[rows, g * 2 * LANES:(g + 1) * 2 * LANES] = _unstack_heads(
            _dot(e, vdup[g, keys, :]) * inv)

    _software_pipeline(n_units, [scores, softmax, values], after_tick)
    merge_half(1)

    @pl.when(i == nt - 1)
    def _():
        kt_ref[0] = k_ref[tq - WINDOW:tq, :].T
        vt_ref[0] = v_ref[tq - WINDOW:tq, :].T

    uext[0:pre, :] = uext[tq:tq + pre, :]
    kdup[:, 0:WINDOW, :] = kdup[:, tq:tq + WINDOW, :]
    vdup[:, 0:WINDOW, :] = vdup[:, tq:tq + WINDOW, :]


def _mix_prompt(x, u, q, k, v, gl, sinks, wpool_bf, pscale, wbp_bf, wba_bf, wout_bf, batch, tq):
    n = x.shape[0]
    n_tiles = n // tq
    nt = n_tiles // batch
    row = lambda c: pl.BlockSpec((tq, c), lambda s: (jnp.minimum(s, n_tiles - 1), 0))
    row_prev = pl.BlockSpec((tq, D_MODEL), lambda s: (jnp.maximum(s - 1, 0), 0))
    last_t = pl.BlockSpec((1, KV_WIDTH, WINDOW),
                          lambda s: (jnp.minimum(s, n_tiles - 1) // nt, 0, 0))
    return pl.pallas_call(
        functools.partial(_mix_prompt_kernel, tq=tq, nt=nt),
        grid=(n_tiles + 1,),
        in_specs=[pl.BlockSpec(memory_space=pltpu.SMEM),
                  row(POOL_WIDTH), row(ATTN_WIDTH), row(KV_WIDTH), row(KV_WIDTH), row(2 * D_MODEL),
                  _const_spec(wpool_bf.shape), _const_spec(pscale.shape),
                  _const_spec(wbp_bf.shape), _const_spec(wba_bf.shape), _const_spec(wout_bf.shape),
                  row_prev],
        out_specs=[row_prev, last_t, last_t],
        out_shape=[jax.ShapeDtypeStruct((n, D_MODEL), F32),
                   jax.ShapeDtypeStruct((batch, KV_WIDTH, WINDOW), F32),
                   jax.ShapeDtypeStruct((batch, KV_WIDTH, WINDOW), F32)],
        scratch_shapes=[pltpu.VMEM((POOL_STATE + 1 + tq, POOL_WIDTH), F32),
                        pltpu.VMEM((N_KV_HEADS, WINDOW + tq, LANES), BF16),
                        pltpu.VMEM((N_KV_HEADS, WINDOW + tq, LANES), BF16),
                        pltpu.VMEM((tq, ATTN_WIDTH), F32),
                        pltpu.VMEM((2, GQA_GROUP * WINDOW, 2 * WINDOW), F32),
                        pltpu.VMEM((tq, D_MODEL), F32),
                        pltpu.VMEM((tq, D_MODEL), BF16)],
        compiler_params=_params(1),
        name="mix_prompt",
    )(sinks, u, q, k, v, gl, wpool_bf, pscale, wbp_bf, wba_bf, wout_bf, x)


def _tile_shift(cur, prev, j, row):
    return jnp.where(row < j, pltpu.roll(prev, j, axis=1), pltpu.roll(cur, j, axis=1))


SEQ_BLOCK = WINDOW // SUBLANES


def _sample_bias4():
    shape = (GQA_GROUP * WINDOW, 2 * WINDOW)
    i = lax.broadcasted_iota(jnp.int32, shape, 0) & (WINDOW - 1)
    j = lax.broadcasted_iota(jnp.int32, shape, 1)
    b, t = i >> 3, i & (SUBLANES - 1)
    new = j - WINDOW
    valid = ((j < WINDOW) & (j >= t)) | ((new >> 3 == b) & ((new & (SUBLANES - 1)) <= t) & (new >= 0))
    return jnp.where(valid, 0.0, -jnp.inf).astype(F32)


def _attend_cached(q, knew, vnew, ckt_ref, cvt_ref, b0, bias4, sinks_ref):
    r, t = WINDOW, SUBLANES
    nt_dims = (((1,), (1,)), ((), ()))
    kn, vn = _dup_kv_heads(knew), _dup_kv_heads(vnew)

    def per_seq(x):
        return [jnp.concatenate([x[j * r + b * t:j * r + (b + 1) * t] for j in range(GQA_GROUP)],
                                axis=0) for b in range(SEQ_BLOCK)]

    def stacked(xs):
        return jnp.concatenate([xs[b][j * t:(j + 1) * t] for j in range(GQA_GROUP)
                                for b in range(SEQ_BLOCK)], axis=0)

    outs = []
    for g in range(N_KV_HEADS):
        kv_rows = slice(g * HEAD_DIM, (g + 1) * HEAD_DIM)
        qs = _stack_heads(q, g)
        s_new = lax.dot_general(qs.astype(BF16), kn[g], nt_dims, preferred_element_type=F32)
        s_cache = []
        for b, qb in enumerate(per_seq(qs)):
            kt = ckt_ref[b0 + b, kv_rows, :]
            s_cache.append(_dot(qb.astype(BF16), jnp.concatenate([kt, kt], axis=0).astype(BF16)))
        s = jnp.concatenate([stacked(s_cache), s_new], axis=1) + bias4
        p, inv = _sink_softmax(s, _sink_column(r, g, sinks_ref))
        o_cache = []
        for b, pb in enumerate(per_seq(p[:, :r])):
            vt = cvt_ref[b0 + b, kv_rows, :]
            o_cache.append(lax.dot_general(pb.astype(BF16),
                                           jnp.concatenate([vt, vt], axis=0).astype(BF16),
                                           nt_dims, preferred_element_type=F32))
        o = stacked(o_cache) + _dot(p[:, r:].astype(BF16), vn[g])
        outs.append(_unstack_heads(o * inv))
    return jnp.concatenate(outs, axis=1)


def _slide_cache(new_rows, ct_ref, out_ref, b0):
    t = SUBLANES
    new_t = new_rows.T
    is_new = lax.broadcasted_iota(jnp.int32, (KV_WIDTH, WINDOW), 1) >= WINDOW - t
    for b in range(SEQ_BLOCK):
        kept = pltpu.roll(ct_ref[b0 + b], WINDOW - t, axis=1)
        fresh = pltpu.roll(new_t, (WINDOW - t - b * t) % WINDOW, axis=1)
        out_ref[b0 + b] = jnp.where(is_new, fresh, kept)


def _mix_sample_kernel(sinks_ref, x_ref, u_ref, st_ref, q_ref, k_ref, v_ref, ckt_ref, cvt_ref, gl_ref,
                       wpool_ref, pscale_ref, wbp_ref, wba_ref, wout_ref, o_ref, kst_ref, vst_ref,
                       oscr, *, gs, pos0):
    t = SUBLANES
    rows = gs * t

    row = lax.broadcasted_iota(jnp.int32, (gs, t, POOL_GROUP), 1)
    cnt_pos = pos0 + lax.broadcasted_iota(jnp.int32, (gs, t, 1), 1) + 1
    a_parts = []
    for g, w in enumerate(POOL_WINDOWS):
        cols = slice(g * POOL_GROUP, (g + 1) * POOL_GROUP)
        tiles = [st_ref[:, 0:t, cols], st_ref[:, t:2 * t, cols], u_ref[:, :, cols]]
        xt = tiles[2]
        step = 1
        while step < w and step < t:
            tiles = [tiles[n] + _tile_shift(tiles[n], tiles[max(n - 1, 0)], step, row)
                     for n in range(3)]
            step *= 2
        acc = tiles[2] + tiles[1] if w == 2 * t else tiles[2]
        cnt = jnp.minimum(cnt_pos, w).astype(F32)
        d = (acc / cnt - xt).reshape(rows, POOL_GROUP)
        y = _dot(d.astype(BF16), wpool_ref[g])
        a_parts.append(y * pscale_ref[:, cols])
    pool_a = jnp.concatenate(a_parts, axis=1)

    bias = _sample_bias4()
    for blk in range(gs // SEQ_BLOCK):
        rs = slice(blk * WINDOW, (blk + 1) * WINDOW)
        oscr[rs, :] = _attend_cached(q_ref[rs, :], k_ref[rs, :], v_ref[rs, :], ckt_ref, cvt_ref,
                                     blk * SEQ_BLOCK, bias, sinks_ref)
        _slide_cache(k_ref[rs, :], ckt_ref, kst_ref, blk * SEQ_BLOCK)
        _slide_cache(v_ref[rs, :], cvt_ref, vst_ref, blk * SEQ_BLOCK)

    o_ref[...] = _merge(x_ref[...], pool_a, oscr[...], gl_ref[...], wbp_ref, wba_ref, wout_ref)


def _mix_sample(x, u3, stpad, q, k, v, ck, cv, gl, sinks, wpool_bf, pscale, wbp_bf, wba_bf, wout_bf,
                gs, pos0):
    nseq, t = u3.shape[0], u3.shape[1]
    rows = gs * t
    row = lambda c: pl.BlockSpec((rows, c), lambda i: (i, 0))
    seq = lambda a, b: pl.BlockSpec((gs, a, b), lambda i: (i, 0, 0))
    return pl.pallas_call(
        functools.partial(_mix_sample_kernel, gs=gs, pos0=pos0),
        grid=(nseq // gs,),
        in_specs=[pl.BlockSpec(memory_space=pltpu.SMEM),
                  row(D_MODEL), seq(t, POOL_WIDTH), seq(2 * t, POOL_WIDTH),
                  row(ATTN_WIDTH), row(KV_WIDTH), row(KV_WIDTH),
                  seq(WINDOW, KV_WIDTH), seq(WINDOW, KV_WIDTH), row(2 * D_MODEL),
                  _const_spec(wpool_bf.shape), _const_spec(pscale.shape),
                  _const_spec(wbp_bf.shape), _const_spec(wba_bf.shape), _const_spec(wout_bf.shape)],
        out_specs=[row(D_MODEL), seq(KV_WIDTH, WINDOW), seq(KV_WIDTH, WINDOW)],
        out_shape=[jax.ShapeDtypeStruct((nseq * t, D_MODEL), F32),
                   jax.ShapeDtypeStruct((nseq, KV_WIDTH, WINDOW), F32),
                   jax.ShapeDtypeStruct((nseq, KV_WIDTH, WINDOW), F32)],
        scratch_shapes=[pltpu.VMEM((rows, ATTN_WIDTH), F32)],
        compiler_params=_params(1),
        name="mix_sample",
    )(sinks, x, u3, stpad, q, k, v, ck, cv, gl, wpool_bf, pscale, wbp_bf, wba_bf, wout_bf)


FF_CHUNK = 256


def _ff_cols(c, part):
    start = part * D_FF + c * FF_CHUNK
    return slice(start, start + FF_CHUNK)


def _gated_act(taps, cw_ref, cb_ref, c):
    def conv(part):
        x0, x1, x2 = taps(c, part)
        cols = _ff_cols(c, part)
        return (x0 * cw_ref[0:1, cols] + x1 * cw_ref[1:2, cols] + x2 * cw_ref[2:3, cols]
                + cb_ref[:, cols])
    return (jax.nn.silu(conv(0)) * conv(1)).astype(BF16)


def _ffn_prompt_kernel(x_ref, g_ref, wup_ref, cw_ref, cb_ref, wdn_ref, gf_ref, o_ref, cs_ref,
                       upx, act, *, tm):
    i = pl.program_id(1)
    pre = SUBLANES

    @pl.when(i == 0)
    def _():
        upx[0:pre, :] = jnp.zeros((pre, 2 * D_FF), F32)

    x = x_ref[...]
    h = _rmsnorm(x, g_ref[...]).astype(BF16)
    upx[pre:pre + tm, :] = _dot(h, wup_ref[...])

    def taps(c, part):
        e = upx[:, _ff_cols(c, part)]
        return pltpu.roll(e, 2, axis=0)[pre:], pltpu.roll(e, 1, axis=0)[pre:], e[pre:]

    for c in range(D_FF // FF_CHUNK):
        act[:, c * FF_CHUNK:(c + 1) * FF_CHUNK] = _gated_act(taps, cw_ref, cb_ref, c)

    y = x + _dot(act[...], wdn_ref[...])
    o_ref[...] = _rmsnorm(y, gf_ref[...])
    cs_ref[0] = upx[pre + tm - CONV_STATE:pre + tm, :]
    upx[0:pre, :] = upx[tm:tm + pre, :]


def _ffn_prompt(x, g_ffn, wup_bf, conv_w, conv_b, wdn_bf, g_final, batch, tm):
    n = x.shape[0]
    nt = n // batch // tm
    row = pl.BlockSpec((tm, D_MODEL), lambda b, i: (b * nt + i, 0))
    return pl.pallas_call(
        functools.partial(_ffn_prompt_kernel, tm=tm),
        grid=(batch, nt),
        in_specs=[row, _const_spec((1, D_MODEL)), _const_spec(wup_bf.shape),
                  _const_spec(conv_w.shape), _const_spec(conv_b.shape), _const_spec(wdn_bf.shape),
                  _const_spec((1, D_MODEL))],
        out_specs=[row, pl.BlockSpec((1, CONV_STATE, 2 * D_FF), lambda b, i: (b, 0, 0))],
        out_shape=[jax.ShapeDtypeStruct((n, D_MODEL), F32),
                   jax.ShapeDtypeStruct((batch, CONV_STATE, 2 * D_FF), F32)],
        scratch_shapes=[pltpu.VMEM((SUBLANES + tm, 2 * D_FF), F32),
                        pltpu.VMEM((tm, D_FF), BF16)],
        compiler_params=_params(2),
        name="ffn_prompt",
    )(x, g_ffn, wup_bf, conv_w, conv_b, wdn_bf, g_final)


def _ffn_sample_kernel(x_ref, st_ref, g_ref, wup_ref, cw_ref, cb_ref, wdn_ref, gf_ref,
                       o_ref, cs_ref, up, act, *, gs):
    t = SUBLANES
    rows = gs * t
    x = x_ref[...]
    h = _rmsnorm(x, g_ref[...]).astype(BF16)
    up[...] = _dot(h, wup_ref[...])
    row = lax.broadcasted_iota(jnp.int32, (gs, t, FF_CHUNK), 1)

    def taps(c, part):
        cols = _ff_cols(c, part)
        x2 = up[:, cols].reshape(gs, t, FF_CHUNK)
        s0 = jnp.broadcast_to(st_ref[:, 0:1, cols], x2.shape)
        s1 = jnp.broadcast_to(st_ref[:, 1:2, cols], x2.shape)
        x1 = jnp.where(row < 1, s1, pltpu.roll(x2, 1, axis=1))
        x0 = jnp.where(row < 1, s0, jnp.where(row < 2, s1, pltpu.roll(x2, 2, axis=1)))
        flat = lambda a: a.reshape(rows, FF_CHUNK)
        return flat(x0), flat(x1), flat(x2)

    for c in range(D_FF // FF_CHUNK):
        act[:, c * FF_CHUNK:(c + 1) * FF_CHUNK] = _gated_act(taps, cw_ref, cb_ref, c)

    y = x + _dot(act[...], wdn_ref[...])
    o_ref[...] = _rmsnorm(y, gf_ref[...])
    for c0 in range(0, 2 * D_FF, 4 * FF_CHUNK):
        cols = slice(c0, min(c0 + 4 * FF_CHUNK, 2 * D_FF))
        cs_ref[:, :, cols] = up[:, cols].reshape(gs, t, -1)[:, t - CONV_STATE:, :]


def _ffn_sample(x, st, g_ffn, wup_bf, conv_w, conv_b, wdn_bf, g_final, gs):
    nseq = st.shape[0]
    t = x.shape[0] // nseq
    rows = gs * t
    row = pl.BlockSpec((rows, D_MODEL), lambda i: (i, 0))
    st_spec = pl.BlockSpec((gs, CONV_STATE, 2 * D_FF), lambda i: (i, 0, 0))
    return pl.pallas_call(
        functools.partial(_ffn_sample_kernel, gs=gs),
        grid=(nseq // gs,),
        in_specs=[row, st_spec,
                  _const_spec((1, D_MODEL)), _const_spec(wup_bf.shape),
                  _const_spec(conv_w.shape), _const_spec(conv_b.shape), _const_spec(wdn_bf.shape),
                  _const_spec((1, D_MODEL))],
        out_specs=[row, st_spec],
        out_shape=[jax.ShapeDtypeStruct((nseq * t, D_MODEL), F32),
                   jax.ShapeDtypeStruct((nseq, CONV_STATE, 2 * D_FF), F32)],
        scratch_shapes=[pltpu.VMEM((rows, 2 * D_FF), F32), pltpu.VMEM((rows, D_FF), BF16)],
        compiler_params=_params(1),
        name="ffn_sample",
    )(x, st, g_ffn, wup_bf, conv_w, conv_b, wdn_bf, g_final)


TM_IN = 1024
TQ_MIX = 512
TM_FFN = 512
GS_MIX = 32
GS_FFN = 64


def kernel(x_prompt, x_sample, state_pool, cache_k, cache_v, state_conv, g_mix, w_in, w_pool_grp,
           pool_scale, sinks, w_branch_pool, w_branch_attn, w_out, g_ffn, w_up, conv_w, conv_b,
           w_down, g_final):
    depth = g_mix.shape[0]
    batch, seq, _ = x_prompt.shape
    dec_batch, dec_seq, _ = x_sample.shape
    assert dec_seq == SUBLANES and seq % TQ_MIX == 0 and seq % TM_FFN == 0
    assert depth == 1

    yp = x_prompt.reshape(batch * seq, D_MODEL)
    ys = x_sample.reshape(dec_batch * dec_seq, D_MODEL)
    outs = [[] for _ in range(8)]
    gfin = g_final.reshape(1, D_MODEL)
    for l in range(depth):
        gmix = g_mix[l].reshape(1, D_MODEL)
        gffn = g_ffn[l].reshape(1, D_MODEL)
        w_in_bf = w_in[l].astype(BF16)
        wpool_bf = w_pool_grp[l].astype(BF16)
        pscale = pool_scale[l].reshape(1, POOL_WIDTH)
        wbp_bf = w_branch_pool[l].astype(BF16)
        wba_bf = w_branch_attn[l].astype(BF16)
        wout_bf = w_out[l].astype(BF16)
        wup_bf = w_up[l].astype(BF16)
        wdn_bf = w_down[l].astype(BF16)
        cw = conv_w[l]
        cb = conv_b[l].reshape(1, 2 * D_FF)

        u, q, k, v, gl = _in_proj(yp, gmix, w_in_bf, TM_IN)
        x1, kt, vt = _mix_prompt(yp, u, q, k, v, gl, sinks[l], wpool_bf, pscale, wbp_bf, wba_bf,
                                 wout_bf, batch, TQ_MIX)
        yp, conv_p = _ffn_prompt(x1, gffn, wup_bf, cw, cb, wdn_bf, gfin, batch, TM_FFN)
        outs[0].append(u.reshape(batch, seq, POOL_WIDTH)[:, seq - POOL_STATE:])
        untranspose = lambda t: t.reshape(-1, N_KV_HEADS, HEAD_DIM, WINDOW).transpose(0, 3, 1, 2)
        outs[1].append(untranspose(kt))
        outs[2].append(untranspose(vt))
        outs[3].append(conv_p)

        us, qs, ks, vs, gls = _in_proj(ys, gmix, w_in_bf, TM_IN)
        u3 = us.reshape(dec_batch, dec_seq, POOL_WIDTH)
        stpad = jnp.pad(state_pool[l], ((0, 0), (1, 0), (0, 0)))
        transposed = lambda c: c.transpose(0, 2, 3, 1).reshape(dec_batch, KV_WIDTH, WINDOW)
        ck, cv = transposed(cache_k[l]), transposed(cache_v[l])
        x1s, kst, vst = _mix_sample(ys, u3, stpad, qs, ks, vs, ck, cv, gls, sinks[l], wpool_bf,
                                    pscale, wbp_bf, wba_bf, wout_bf, GS_MIX, PAST_LEN)
        ys, conv_s = _ffn_sample(x1s, state_conv[l], gffn, wup_bf, cw, cb, wdn_bf, gfin, GS_FFN)
        outs[4].append(jnp.concatenate([state_pool[l], u3], axis=1)[:, dec_seq:])
        outs[5].append(untranspose(kst))
        outs[6].append(untranspose(vst))
        outs[7].append(conv_s)
    return (yp.reshape(batch, seq, D_MODEL), ys.reshape(dec_batch, dec_seq, D_MODEL),
            *[jnp.stack(o) for o in outs])
```

```python
import functools

import jax
import jax.numpy as jnp
from jax import lax
from jax.experimental import pallas as pl
from jax.experimental.pallas import tpu as pltpu

D_MODEL = 1024
POOL_WIDTH = 512
POOL_WINDOWS = (2, 4, 8, 16)
POOL_GROUP = 128
POOL_STATE = 15
HEAD_DIM = 64
N_HEADS = 8
N_KV_HEADS = 2
GQA_GROUP = N_HEADS // N_KV_HEADS
WINDOW = 128
ATTN_WIDTH = N_HEADS * HEAD_DIM
KV_WIDTH = N_KV_HEADS * HEAD_DIM
ATTN_SCALE = HEAD_DIM ** -0.5
LOG2_E = 1.4426950408889634
LOGIT_SCALE = ATTN_SCALE * LOG2_E
D_FF = 2816
CONV_STATE = 2
EPS = 1e-6
PAST_LEN = 16384

LANES = 128
SUBLANES = 8
VMEM_LIMIT = 58 * 1024 * 1024

F32 = jnp.float32
BF16 = jnp.bfloat16


def _rmsnorm(x, g):
    inv = lax.rsqrt(jnp.mean(x * x, axis=-1, keepdims=True) + EPS)
    return (x * inv) * g


def _dot(a, b):
    return jnp.dot(a, b, preferred_element_type=F32)


def _const_spec(shape):
    nd = len(shape)
    return pl.BlockSpec(shape, lambda *_: (0,) * nd, pipeline_mode=pl.Buffered(1))


def _params(n_axes):
    return pltpu.CompilerParams(dimension_semantics=("arbitrary",) * n_axes,
                                vmem_limit_bytes=VMEM_LIMIT)


def _in_proj_kernel(xa_ref, xb_ref, g_ref, w_ref, u_ref, q_ref, k_ref, v_ref, gl_ref, *, tiles_a):
    def project(x_ref):
        h = _rmsnorm(x_ref[...], g_ref[...]).astype(BF16)
        o_q = POOL_WIDTH
        o_k = o_q + ATTN_WIDTH
        o_gl = o_k + 2 * KV_WIDTH
        u_ref[...] = _dot(h, w_ref[:, 0:o_q])
        q_ref[...] = _dot(h, w_ref[:, o_q:o_k])
        kv = _dot(h, w_ref[:, o_k:o_gl])
        k_ref[...] = kv[:, :KV_WIDTH]
        v_ref[...] = kv[:, KV_WIDTH:]
        gl_ref[...] = _dot(h, w_ref[:, o_gl:])

    i = pl.program_id(0)
    pl.when(i < tiles_a)(functools.partial(project, xa_ref))
    pl.when(i >= tiles_a)(functools.partial(project, xb_ref))


def _in_proj(xa, xb, g_mix, w_in_bf, tm):
    tiles_a, tiles_b = xa.shape[0] // tm, xb.shape[0] // tm
    n = xa.shape[0] + xb.shape[0]
    in_width = w_in_bf.shape[1]
    row = lambda c: pl.BlockSpec((tm, c), lambda i: (i, 0))
    return pl.pallas_call(
        functools.partial(_in_proj_kernel, tiles_a=tiles_a),
        grid=(tiles_a + tiles_b,),
        in_specs=[pl.BlockSpec((tm, D_MODEL), lambda i: (jnp.minimum(i, tiles_a - 1), 0)),
                  pl.BlockSpec((tm, D_MODEL), lambda i: (jnp.maximum(i - tiles_a, 0), 0)),
                  _const_spec((1, D_MODEL)), _const_spec((D_MODEL, in_width))],
        out_specs=[row(POOL_WIDTH), row(ATTN_WIDTH), row(KV_WIDTH), row(KV_WIDTH), row(2 * D_MODEL)],
        out_shape=[jax.ShapeDtypeStruct((n, POOL_WIDTH), F32),
                   jax.ShapeDtypeStruct((n, ATTN_WIDTH), F32),
                   jax.ShapeDtypeStruct((n, KV_WIDTH), F32),
                   jax.ShapeDtypeStruct((n, KV_WIDTH), F32),
                   jax.ShapeDtypeStruct((n, 2 * D_MODEL), F32)],
        compiler_params=_params(1),
        name="in_proj",
    )(xa, xb, g_mix, w_in_bf)


def _dup_kv_heads(x):
    lo = lax.broadcasted_iota(jnp.int32, x.shape, 1) < HEAD_DIM
    r = pltpu.roll(x, HEAD_DIM, axis=1)
    return [jnp.where(lo, x, r).astype(BF16), jnp.where(lo, r, x).astype(BF16)]


def _stack_heads(q, g):
    lo = lax.broadcasted_iota(jnp.int32, (q.shape[0], LANES), 1) < HEAD_DIM
    scale_lo = jnp.where(lo, LOGIT_SCALE, 0.0).astype(F32)
    scale_hi = jnp.where(lo, 0.0, LOGIT_SCALE).astype(F32)
    base = g * GQA_GROUP * HEAD_DIM
    p0 = q[:, base:base + LANES]
    p1 = q[:, base + LANES:base + 2 * LANES]
    return jnp.concatenate([p0 * scale_lo, p0 * scale_hi, p1 * scale_lo, p1 * scale_hi], axis=0)


def _attn_scores(q, kdup, g, bias4):
    return lax.dot_general(_stack_heads(q, g).astype(BF16), kdup, (((1,), (1,)), ((), ())),
                           preferred_element_type=F32) + bias4


def _sink_column(r, g, sinks_ref):
    row4 = lax.broadcasted_iota(jnp.int32, (GQA_GROUP * r, 1), 0)
    sk = jnp.full((GQA_GROUP * r, 1), sinks_ref[g * GQA_GROUP + GQA_GROUP - 1], F32)
    for hh in range(GQA_GROUP - 2, -1, -1):
        sk = jnp.where(row4 < (hh + 1) * r, sinks_ref[g * GQA_GROUP + hh], sk)
    return sk * LOG2_E


def _sink_softmax(s, sk):
    m = jnp.maximum(jnp.max(s, axis=-1, keepdims=True), sk)
    e = jnp.exp2(s - m)
    den = jnp.sum(e, axis=-1, keepdims=True) + jnp.exp2(sk - m)
    return e, 1.0 / den


def _unstack_heads(o):
    r = o.shape[0] // GQA_GROUP
    lo = lax.broadcasted_iota(jnp.int32, (r, LANES), 1) < HEAD_DIM
    return jnp.concatenate([jnp.where(lo, o[0:r], o[r:2 * r]),
                            jnp.where(lo, o[2 * r:3 * r], o[3 * r:4 * r])], axis=1)


def _software_pipeline(n_units, stages, after_tick=()):
    vals = {}
    for tick in range(n_units + len(stages) - 1):
        for k in reversed(range(len(stages))):
            u = tick - k
            if 0 <= u < n_units:
                vals[u] = stages[k](u, vals.get(u))
        if tick < len(after_tick) and after_tick[tick] is not None:
            after_tick[tick]()


def _band_bias4(r, s, first_key=0):
    i = lax.broadcasted_iota(jnp.int32, (GQA_GROUP * r, s), 0) & (r - 1)
    j = lax.broadcasted_iota(jnp.int32, (GQA_GROUP * r, s), 1)
    valid = (j >= jnp.maximum(i, first_key)) & (j <= i + WINDOW)
    return jnp.where(valid, 0.0, -jnp.inf).astype(F32)


def _window_sums(e, w):
    step = 1
    while step < w:
        e = e + pltpu.roll(e, step, axis=0)
        step *= 2
    return e


def _gate_merge(pa, po, gl):
    ga = jax.nn.sigmoid(gl[:, :D_MODEL])
    gb = jax.nn.sigmoid(gl[:, D_MODEL:])
    return (ga * pa + gb * po).astype(BF16)


def _merge(x, pool_a, attn_o, gl, wbp_ref, wba_ref, wout_ref):
    pa = _dot(pool_a.astype(BF16), wbp_ref[...])
    po = _dot(attn_o.astype(BF16), wba_ref[...])
    return x + _dot(_gate_merge(pa, po, gl), wout_ref[...])


def _mix_prompt_kernel(sinks_ref, u_ref, q_ref, k_ref, v_ref, gl_ref, wpool_ref, pscale_ref,
                       wbp_ref, wba_ref, wout_ref, xo_ref, o_ref, kt_ref, vt_ref, uext, kdup, vdup,
                       oscr, bias, pa_scr, mixed, *, tq, nt):
    s = pl.program_id(0)
    i = s % nt
    pre = POOL_STATE + 1

    @pl.when(s == 0)
    def _():
        mixed[...] = jnp.zeros(mixed.shape, BF16)

    @pl.when(i == 0)
    def _():
        uext[0:pre, :] = jnp.zeros((pre, POOL_WIDTH), F32)
        kdup[:, 0:WINDOW, :] = jnp.zeros((N_KV_HEADS, WINDOW, LANES), BF16)
        vdup[:, 0:WINDOW, :] = jnp.zeros((N_KV_HEADS, WINDOW, LANES), BF16)
        bias[0] = _band_bias4(WINDOW, 2 * WINDOW, WINDOW)
        bias[1] = _band_bias4(WINDOW, 2 * WINDOW)

    uext[pre:pre + tq, :] = u_ref[...]
    for dst, src in ((kdup, k_ref), (vdup, v_ref)):
        for g, d in enumerate(_dup_kv_heads(src[...])):
            dst[g, WINDOW:WINDOW + tq, :] = d
    sink_cols = [_sink_column(WINDOW, g, sinks_ref) for g in range(N_KV_HEADS)]

    piece = D_MODEL // 4

    def out_prev(n):
        cols = slice(n * piece, (n + 1) * piece)
        o_ref[:, cols] = xo_ref[:, cols] + _dot(mixed[...], wout_ref[:, cols])

    pos1 = i * tq + lax.broadcasted_iota(jnp.int32, (tq, 1), 0) + 1
    a_parts = []

    def pool_group(g):
        w = POOL_WINDOWS[g]
        cols = slice(g * POOL_GROUP, (g + 1) * POOL_GROUP)
        e = uext[:, cols]
        inv_cnt = 1.0 / jnp.minimum(pos1, w).astype(F32)
        d = _window_sums(e, w)[pre:] * inv_cnt - e[pre:]
        y = _dot(d.astype(BF16), wpool_ref[g])
        a_parts.append((y * pscale_ref[:, cols]).astype(BF16))

    n_units = N_KV_HEADS * tq // WINDOW

    def pooled_branch(n):
        cols = slice(n * 2 * piece, (n + 1) * 2 * piece)
        pa_scr[:, cols] = _dot(jnp.concatenate(a_parts, axis=1), wbp_ref[:, cols])

    def early(n):
        out_prev(n)
        pool_group(n)

    def merge_half(half):
        rows = slice(half * tq // 2, (half + 1) * tq // 2)
        po = _dot(oscr[rows, :].astype(BF16), wba_ref[...])
        mixed[rows, :] = _gate_merge(pa_scr[rows, :], po, gl_ref[rows, :])

    half_done = n_units // 2 + 1
    after_tick = [functools.partial(early, n) for n in range(4)]
    after_tick += [functools.partial(pooled_branch, n) for n in range(2)]
    assert len(after_tick) <= half_done + 1
    after_tick += [None] * (half_done + 1 - len(after_tick)) + [functools.partial(merge_half, 0)]

    def unit(u):
        nb, g = divmod(u, N_KV_HEADS)
        return nb, g, slice(nb * WINDOW, (nb + 1) * WINDOW), slice(nb * WINDOW, (nb + 2) * WINDOW)

    def scores(u, _):
        nb, g, rows, keys = unit(u)
        slot = jnp.minimum(i * (tq // WINDOW) + nb, 1)
        return _attn_scores(q_ref[rows, :], kdup[g, keys, :], g, bias[slot])

    def softmax(u, s):
        e, inv = _sink_softmax(s, sink_cols[unit(u)[1]])
        return e.astype(BF16), inv

    def values(u, weights):
        nb, g, rows, keys = unit(u)
        e, inv = weights
        oscr[rows, g * 2 * LANES:(g + 1) * 2 * LANES] = _unstack_heads(
            _dot(e, vdup[g, keys, :]) * inv)

    _software_pipeline(n_units, [scores, softmax, values], after_tick)
    merge_half(1)

    @pl.when(i == nt - 1)
    def _():
        kt_ref[0] = k_ref[tq - WINDOW:tq, :].T
        vt_ref[0] = v_ref[tq - WINDOW:tq, :].T

    uext[0:pre, :] = uext[tq:tq + pre, :]
    kdup[:, 0:WINDOW, :] = kdup[:, tq:tq + WINDOW, :]
    vdup[:, 0:WINDOW, :] = vdup[:, tq:tq + WINDOW, :]


def _mix_prompt(x, u, q, k, v, gl, sinks, wpool_bf, pscale, wbp_bf, wba_bf, wout_bf, batch, tq):
    n = x.shape[0]
    n_tiles = n // tq
    nt = n_tiles // batch
    row = lambda c: pl.BlockSpec((tq, c), lambda s: (jnp.minimum(s, n_tiles - 1), 0))
    row_prev = pl.BlockSpec((tq, D_MODEL), lambda s: (jnp.maximum(s - 1, 0), 0))
    last_t = pl.BlockSpec((1, KV_WIDTH, WINDOW),
                          lambda s: (jnp.minimum(s, n_tiles - 1) // nt, 0, 0))
    return pl.pallas_call(
        functools.partial(_mix_prompt_kernel, tq=tq, nt=nt),
        grid=(n_tiles + 1,),
        in_specs=[pl.BlockSpec(memory_space=pltpu.SMEM),
                  row(POOL_WIDTH), row(ATTN_WIDTH), row(KV_WIDTH), row(KV_WIDTH), row(2 * D_MODEL),
                  _const_spec(wpool_bf.shape), _const_spec(pscale.shape),
                  _const_spec(wbp_bf.shape), _const_spec(wba_bf.shape), _const_spec(wout_bf.shape),
                  row_prev],
        out_specs=[row_prev, last_t, last_t],
        out_shape=[jax.ShapeDtypeStruct((n, D_MODEL), F32),
                   jax.ShapeDtypeStruct((batch, KV_WIDTH, WINDOW), F32),
                   jax.ShapeDtypeStruct((batch, KV_WIDTH, WINDOW), F32)],
        scratch_shapes=[pltpu.VMEM((POOL_STATE + 1 + tq, POOL_WIDTH), F32),
                        pltpu.VMEM((N_KV_HEADS, WINDOW + tq, LANES), BF16),
                        pltpu.VMEM((N_KV_HEADS, WINDOW + tq, LANES), BF16),
                        pltpu.VMEM((tq, ATTN_WIDTH), F32),
                        pltpu.VMEM((2, GQA_GROUP * WINDOW, 2 * WINDOW), F32),
                        pltpu.VMEM((tq, D_MODEL), F32),
                        pltpu.VMEM((tq, D_MODEL), BF16)],
        compiler_params=_params(1),
        name="mix_prompt",
    )(sinks, u, q, k, v, gl, wpool_bf, pscale, wbp_bf, wba_bf, wout_bf, x)


def _tile_shift(cur, prev, j, row):
    return jnp.where(row < j, pltpu.roll(prev, j, axis=1), pltpu.roll(cur, j, axis=1))


SEQ_BLOCK = WINDOW // SUBLANES


def _sample_bias4():
    shape = (GQA_GROUP * WINDOW, 2 * WINDOW)
    i = lax.broadcasted_iota(jnp.int32, shape, 0) & (WINDOW - 1)
    j = lax.broadcasted_iota(jnp.int32, shape, 1)
    b, t = i >> 3, i & (SUBLANES - 1)
    new = j - WINDOW
    valid = ((j < WINDOW) & (j >= t)) | ((new >> 3 == b) & ((new & (SUBLANES - 1)) <= t) & (new >= 0))
    return jnp.where(valid, 0.0, -jnp.inf).astype(F32)


def _attend_cached(q, knew, vnew, ckt_ref, cvt_ref, b0, bias4, sinks_ref):
    r, t = WINDOW, SUBLANES
    nt_dims = (((1,), (1,)), ((), ()))
    kn, vn = _dup_kv_heads(knew), _dup_kv_heads(vnew)

    def per_seq(x):
        return [jnp.concatenate([x[j * r + b * t:j * r + (b + 1) * t] for j in range(GQA_GROUP)],
                                axis=0) for b in range(SEQ_BLOCK)]

    def stacked(xs):
        return jnp.concatenate([xs[b][j * t:(j + 1) * t] for j in range(GQA_GROUP)
                                for b in range(SEQ_BLOCK)], axis=0)

    outs = []
    for g in range(N_KV_HEADS):
        kv_rows = slice(g * HEAD_DIM, (g + 1) * HEAD_DIM)
        qs = _stack_heads(q, g)
        s_new = lax.dot_general(qs.astype(BF16), kn[g], nt_dims, preferred_element_type=F32)
        s_cache = []
        for b, qb in enumerate(per_seq(qs)):
            kt = ckt_ref[b0 + b, kv_rows, :]
            s_cache.append(_dot(qb.astype(BF16), jnp.concatenate([kt, kt], axis=0).astype(BF16)))
        s = jnp.concatenate([stacked(s_cache), s_new], axis=1) + bias4
        p, inv = _sink_softmax(s, _sink_column(r, g, sinks_ref))
        o_cache = []
        for b, pb in enumerate(per_seq(p[:, :r])):
            vt = cvt_ref[b0 + b, kv_rows, :]
            o_cache.append(lax.dot_general(pb.astype(BF16),
                                           jnp.concatenate([vt, vt], axis=0).astype(BF16),
                                           nt_dims, preferred_element_type=F32))
        o = stacked(o_cache) + _dot(p[:, r:].astype(BF16), vn[g])
        outs.append(_unstack_heads(o * inv))
    return jnp.concatenate(outs, axis=1)


def _slide_cache(new_rows, ct_ref, out_ref, b0):
    t = SUBLANES
    new_t = new_rows.T
    is_new = lax.broadcasted_iota(jnp.int32, (KV_WIDTH, WINDOW), 1) >= WINDOW - t
    for b in range(SEQ_BLOCK):
        kept = pltpu.roll(ct_ref[b0 + b], WINDOW - t, axis=1)
        fresh = pltpu.roll(new_t, (WINDOW - t - b * t) % WINDOW, axis=1)
        out_ref[b0 + b] = jnp.where(is_new, fresh, kept)


def _mix_sample_kernel(sinks_ref, x_ref, u_ref, st_ref, q_ref, k_ref, v_ref, ckt_ref, cvt_ref, gl_ref,
                       wpool_ref, pscale_ref, wbp_ref, wba_ref, wout_ref, o_ref, kst_ref, vst_ref,
                       oscr, *, gs, pos0):
    t = SUBLANES
    rows = gs * t

    row = lax.broadcasted_iota(jnp.int32, (gs, t, POOL_GROUP), 1)
    cnt_pos = pos0 + lax.broadcasted_iota(jnp.int32, (gs, t, 1), 1) + 1
    a_parts = []
    for g, w in enumerate(POOL_WINDOWS):
        cols = slice(g * POOL_GROUP, (g + 1) * POOL_GROUP)
        tiles = [st_ref[:, 0:t, cols], st_ref[:, t:2 * t, cols], u_ref[:, :, cols]]
        xt = tiles[2]
        step = 1
        while step < w and step < t:
            tiles = [tiles[n] + _tile_shift(tiles[n], tiles[max(n - 1, 0)], step, row)
                     for n in range(3)]
            step *= 2
        acc = tiles[2] + tiles[1] if w == 2 * t else tiles[2]
        cnt = jnp.minimum(cnt_pos, w).astype(F32)
        d = (acc / cnt - xt).reshape(rows, POOL_GROUP)
        y = _dot(d.astype(BF16), wpool_ref[g])
        a_parts.append(y * pscale_ref[:, cols])
    pool_a = jnp.concatenate(a_parts, axis=1)

    bias = _sample_bias4()
    for blk in range(gs // SEQ_BLOCK):
        rs = slice(blk * WINDOW, (blk + 1) * WINDOW)
        oscr[rs, :] = _attend_cached(q_ref[rs, :], k_ref[rs, :], v_ref[rs, :], ckt_ref, cvt_ref,
                                     blk * SEQ_BLOCK, bias, sinks_ref)
        _slide_cache(k_ref[rs, :], ckt_ref, kst_ref, blk * SEQ_BLOCK)
        _slide_cache(v_ref[rs, :], cvt_ref, vst_ref, blk * SEQ_BLOCK)

    o_ref[...] = _merge(x_ref[...], pool_a, oscr[...], gl_ref[...], wbp_ref, wba_ref, wout_ref)


def _mix_sample(x, u3, stpad, q, k, v, ck, cv, gl, sinks, wpool_bf, pscale, wbp_bf, wba_bf, wout_bf,
                gs, pos0):
    t = u3.shape[1]
    nseq = x.shape[0] // t
    rows = gs * t
    first = (q.shape[0] - x.shape[0]) // rows
    row = lambda c, off=0: pl.BlockSpec((rows, c), lambda i: (i + off, 0))
    seq = lambda a, b, off=0: pl.BlockSpec((gs, a, b), lambda i: (i + off, 0, 0))
    return pl.pallas_call(
        functools.partial(_mix_sample_kernel, gs=gs, pos0=pos0),
        grid=(nseq // gs,),
        in_specs=[pl.BlockSpec(memory_space=pltpu.SMEM),
                  row(D_MODEL), seq(t, POOL_WIDTH, first), seq(2 * t, POOL_WIDTH),
                  row(ATTN_WIDTH, first), row(KV_WIDTH, first), row(KV_WIDTH, first),
                  seq(WINDOW, KV_WIDTH), seq(WINDOW, KV_WIDTH), row(2 * D_MODEL, first),
                  _const_spec(wpool_bf.shape), _const_spec(pscale.shape),
                  _const_spec(wbp_bf.shape), _const_spec(wba_bf.shape), _const_spec(wout_bf.shape)],
        out_specs=[row(D_MODEL), seq(KV_WIDTH, WINDOW), seq(KV_WIDTH, WINDOW)],
        out_shape=[jax.ShapeDtypeStruct((nseq * t, D_MODEL), F32),
                   jax.ShapeDtypeStruct((nseq, KV_WIDTH, WINDOW), F32),
                   jax.ShapeDtypeStruct((nseq, KV_WIDTH, WINDOW), F32)],
        scratch_shapes=[pltpu.VMEM((rows, ATTN_WIDTH), F32)],
        compiler_params=_params(1),
        name="mix_sample",
    )(sinks, x, u3, stpad, q, k, v, ck, cv, gl, wpool_bf, pscale, wbp_bf, wba_bf, wout_bf)


FF_CHUNK = 256


def _ff_cols(c, part):
    start = part * D_FF + c * FF_CHUNK
    return slice(start, start + FF_CHUNK)


def _gated_act(taps, cw_ref, cb_ref, c):
    def conv(part):
        x0, x1, x2 = taps(c, part)
        cols = _ff_cols(c, part)
        return (x0 * cw_ref[0:1, cols] + x1 * cw_ref[1:2, cols] + x2 * cw_ref[2:3, cols]
                + cb_ref[:, cols])
    return (jax.nn.silu(conv(0)) * conv(1)).astype(BF16)


def _ffn_kernel(xp_ref, xs_ref, st_ref, g_ref, wup_ref, cw_ref, cb_ref, wdn_ref, gf_ref,
                op_ref, csp_ref, os_ref, css_ref, upx, act, *, tm, nt, tiles_p, gs):
    s = pl.program_id(0)
    pre = SUBLANES

    def finish(x, rows, o_ref):
        y = x + _dot(act[0:rows, :], wdn_ref[...])
        o_ref[...] = _rmsnorm(y, gf_ref[...])

    def prompt_tile():
        @pl.when(s % nt == 0)
        def _():
            upx[0:pre, :] = jnp.zeros((pre, 2 * D_FF), F32)

        x = xp_ref[...]
        h = _rmsnorm(x, g_ref[...]).astype(BF16)
        upx[pre:pre + tm, :] = _dot(h, wup_ref[...])

        def taps(c, part):
            e = upx[:, _ff_cols(c, part)]
            return pltpu.roll(e, 2, axis=0)[pre:], pltpu.roll(e, 1, axis=0)[pre:], e[pre:]

        for c in range(D_FF // FF_CHUNK):
            act[:, c * FF_CHUNK:(c + 1) * FF_CHUNK] = _gated_act(taps, cw_ref, cb_ref, c)
        finish(x, tm, op_ref)
        csp_ref[0] = upx[pre + tm - CONV_STATE:pre + tm, :]
        upx[0:pre, :] = upx[tm:tm + pre, :]

    def sample_tile():
        t = SUBLANES
        rows = gs * t
        up = upx.at[pre:pre + rows, :]
        x = xs_ref[...]
        h = _rmsnorm(x, g_ref[...]).astype(BF16)
        up[...] = _dot(h, wup_ref[...])
        row = lax.broadcasted_iota(jnp.int32, (gs, t, FF_CHUNK), 1)

        def taps(c, part):
            cols = _ff_cols(c, part)
            x2 = up[:, cols].reshape(gs, t, FF_CHUNK)
            s0 = jnp.broadcast_to(st_ref[:, 0:1, cols], x2.shape)
            s1 = jnp.broadcast_to(st_ref[:, 1:2, cols], x2.shape)
            x1 = jnp.where(row < 1, s1, pltpu.roll(x2, 1, axis=1))
            x0 = jnp.where(row < 1, s0, jnp.where(row < 2, s1, pltpu.roll(x2, 2, axis=1)))
            flat = lambda a: a.reshape(rows, FF_CHUNK)
            return flat(x0), flat(x1), flat(x2)

        for c in range(D_FF // FF_CHUNK):
            act[0:rows, c * FF_CHUNK:(c + 1) * FF_CHUNK] = _gated_act(taps, cw_ref, cb_ref, c)
        finish(x, rows, os_ref)
        for c0 in range(0, 2 * D_FF, 4 * FF_CHUNK):
            cols = slice(c0, min(c0 + 4 * FF_CHUNK, 2 * D_FF))
            css_ref[:, :, cols] = up[:, cols].reshape(gs, t, -1)[:, t - CONV_STATE:, :]

    pl.when(s < tiles_p)(prompt_tile)
    pl.when(s >= tiles_p)(sample_tile)


def _ffn(xp, xs, st, g_ffn, wup_bf, conv_w, conv_b, wdn_bf, g_final, batch, tm, gs):
    tiles_p = xp.shape[0] // tm
    nt = tiles_p // batch
    nseq = st.shape[0]
    rows_s = gs * (xs.shape[0] // nseq)
    assert rows_s <= tm
    p_idx = lambda s: jnp.minimum(s, tiles_p - 1)
    s_idx = lambda s: jnp.maximum(s - tiles_p, 0)
    row_p = pl.BlockSpec((tm, D_MODEL), lambda s: (p_idx(s), 0))
    row_s = pl.BlockSpec((rows_s, D_MODEL), lambda s: (s_idx(s), 0))
    st_spec = pl.BlockSpec((gs, CONV_STATE, 2 * D_FF), lambda s: (s_idx(s), 0, 0))
    return pl.pallas_call(
        functools.partial(_ffn_kernel, tm=tm, nt=nt, tiles_p=tiles_p, gs=gs),
        grid=(tiles_p + nseq // gs,),
        in_specs=[row_p, row_s, st_spec,
                  _const_spec((1, D_MODEL)), _const_spec(wup_bf.shape),
                  _const_spec(conv_w.shape), _const_spec(conv_b.shape), _const_spec(wdn_bf.shape),
                  _const_spec((1, D_MODEL))],
        out_specs=[row_p, pl.BlockSpec((1, CONV_STATE, 2 * D_FF), lambda s: (p_idx(s) // nt, 0, 0)),
                   row_s, st_spec],
        out_shape=[jax.ShapeDtypeStruct(xp.shape, F32),
                   jax.ShapeDtypeStruct((batch, CONV_STATE, 2 * D_FF), F32),
                   jax.ShapeDtypeStruct(xs.shape, F32),
                   jax.ShapeDtypeStruct(st.shape, F32)],
        scratch_shapes=[pltpu.VMEM((SUBLANES + tm, 2 * D_FF), F32),
                        pltpu.VMEM((tm, D_FF), BF16)],
        compiler_params=_params(1),
        name="ffn",
    )(xp, xs, st, g_ffn, wup_bf, conv_w, conv_b, wdn_bf, g_final)


TM_IN = 1024
TQ_MIX = 512
TM_FFN = 512
GS_MIX = 32
GS_FFN = 32


def kernel(x_prompt, x_sample, state_pool, cache_k, cache_v, state_conv, g_mix, w_in, w_pool_grp,
           pool_scale, sinks, w_branch_pool, w_branch_attn, w_out, g_ffn, w_up, conv_w, conv_b,
           w_down, g_final):
    depth = g_mix.shape[0]
    batch, seq, _ = x_prompt.shape
    dec_batch, dec_seq, _ = x_sample.shape
    assert dec_seq == SUBLANES and seq % TQ_MIX == 0 and seq % TM_FFN == 0
    assert depth == 1

    yp = x_prompt.reshape(batch * seq, D_MODEL)
    ys = x_sample.reshape(dec_batch * dec_seq, D_MODEL)
    outs = [[] for _ in range(8)]
    gfin = g_final.reshape(1, D_MODEL)
    for l in range(depth):
        gmix = g_mix[l].reshape(1, D_MODEL)
        gffn = g_ffn[l].reshape(1, D_MODEL)
        w_in_bf = w_in[l].astype(BF16)
        wpool_bf = w_pool_grp[l].astype(BF16)
        pscale = pool_scale[l].reshape(1, POOL_WIDTH)
        wbp_bf = w_branch_pool[l].astype(BF16)
        wba_bf = w_branch_attn[l].astype(BF16)
        wout_bf = w_out[l].astype(BF16)
        wup_bf = w_up[l].astype(BF16)
        wdn_bf = w_down[l].astype(BF16)
        cw = conv_w[l]
        cb = conv_b[l].reshape(1, 2 * D_FF)

        n_prompt = batch * seq
        u, q, k, v, gl = _in_proj(yp, ys, gmix, w_in_bf, TM_IN)

        x1, kt, vt = _mix_prompt(yp, u, q, k, v, gl, sinks[l], wpool_bf, pscale, wbp_bf, wba_bf,
                                 wout_bf, batch, TQ_MIX)
        outs[0].append(u[:n_prompt].reshape(batch, seq, POOL_WIDTH)[:, seq - POOL_STATE:])
        untranspose = lambda t: t.reshape(-1, N_KV_HEADS, HEAD_DIM, WINDOW).transpose(0, 3, 1, 2)
        outs[1].append(untranspose(kt))
        outs[2].append(untranspose(vt))

        u3 = u.reshape(-1, dec_seq, POOL_WIDTH)
        stpad = jnp.pad(state_pool[l], ((0, 0), (1, 0), (0, 0)))
        transposed = lambda c: c.transpose(0, 2, 3, 1).reshape(dec_batch, KV_WIDTH, WINDOW)
        ck, cv = transposed(cache_k[l]), transposed(cache_v[l])
        x1s, kst, vst = _mix_sample(ys, u3, stpad, q, k, v, ck, cv, gl, sinks[l], wpool_bf,
                                    pscale, wbp_bf, wba_bf, wout_bf, GS_MIX, PAST_LEN)
        yp, conv_p, ys, conv_s = _ffn(x1, x1s, state_conv[l], gffn, wup_bf, cw, cb, wdn_bf, gfin,
                                      batch, TM_FFN, GS_FFN)
        outs[3].append(conv_p)
        u3_sample = u3[n_prompt // dec_seq:]
        outs[4].append(jnp.concatenate([state_pool[l], u3_sample], axis=1)[:, dec_seq:])
        outs[5].append(untranspose(kst))
        outs[6].append(untranspose(vst))
        outs[7].append(conv_s)
    return (yp.reshape(batch, seq, D_MODEL), ys.reshape(dec_batch, dec_seq, D_MODEL),
            *[jnp.stack(o) for o in outs])
```

```python
import functools

import jax
import jax.numpy as jnp
from jax import lax
from jax.experimental import pallas as pl
from jax.experimental.pallas import tpu as pltpu

D_MODEL = 1024
POOL_WIDTH = 512
POOL_WINDOWS = (2, 4, 8, 16)
POOL_GROUP = 128
POOL_STATE = 15
HEAD_DIM = 64
N_HEADS = 8
N_KV_HEADS = 2
GQA_GROUP = N_HEADS // N_KV_HEADS
WINDOW = 128
ATTN_WIDTH = N_HEADS * HEAD_DIM
KV_WIDTH = N_KV_HEADS * HEAD_DIM
ATTN_SCALE = HEAD_DIM ** -0.5
LOG2_E = 1.4426950408889634
LOGIT_SCALE = ATTN_SCALE * LOG2_E
D_FF = 2816
CONV_STATE = 2
EPS = 1e-6
PAST_LEN = 16384

LANES = 128
SUBLANES = 8
VMEM_LIMIT = 56 * 1024 * 1024

F32 = jnp.float32
BF16 = jnp.bfloat16


def _rmsnorm(x, g):
    inv = lax.rsqrt(jnp.mean(x * x, axis=-1, keepdims=True) + EPS)
    return (x * inv) * g


def _dot(a, b):
    return jnp.dot(a, b, preferred_element_type=F32)


def _const_spec(shape):
    nd = len(shape)
    return pl.BlockSpec(shape, lambda *_: (0,) * nd, pipeline_mode=pl.Buffered(1))


def _params(n_axes):
    return pltpu.CompilerParams(dimension_semantics=("arbitrary",) * n_axes,
                                vmem_limit_bytes=VMEM_LIMIT)


def _in_proj_kernel(x_ref, g_ref, w_ref, u_ref, q_ref, k_ref, v_ref, gl_ref):
    h = _rmsnorm(x_ref[...], g_ref[...]).astype(BF16)
    o_q = POOL_WIDTH
    o_k = o_q + ATTN_WIDTH
    o_gl = o_k + 2 * KV_WIDTH
    u_ref[...] = _dot(h, w_ref[:, 0:o_q])
    q_ref[...] = _dot(h, w_ref[:, o_q:o_k])
    kv = _dot(h, w_ref[:, o_k:o_gl])
    k_ref[...] = kv[:, :KV_WIDTH]
    v_ref[...] = kv[:, KV_WIDTH:]
    gl_ref[...] = _dot(h, w_ref[:, o_gl:])


def _in_proj(x, g_mix, w_in_bf, tm):
    n = x.shape[0]
    in_width = w_in_bf.shape[1]
    row = lambda c: pl.BlockSpec((tm, c), lambda i: (i, 0))
    return pl.pallas_call(
        _in_proj_kernel,
        grid=(n // tm,),
        in_specs=[row(D_MODEL), _const_spec((1, D_MODEL)), _const_spec((D_MODEL, in_width))],
        out_specs=[row(POOL_WIDTH), row(ATTN_WIDTH), row(KV_WIDTH), row(KV_WIDTH), row(2 * D_MODEL)],
        out_shape=[jax.ShapeDtypeStruct((n, POOL_WIDTH), F32),
                   jax.ShapeDtypeStruct((n, ATTN_WIDTH), F32),
                   jax.ShapeDtypeStruct((n, KV_WIDTH), F32),
                   jax.ShapeDtypeStruct((n, KV_WIDTH), F32),
                   jax.ShapeDtypeStruct((n, 2 * D_MODEL), F32)],
        compiler_params=_params(1),
        name="in_proj",
    )(x, g_mix, w_in_bf)


def _dup_kv_heads(x):
    lo = lax.broadcasted_iota(jnp.int32, x.shape, 1) < HEAD_DIM
    r = pltpu.roll(x, HEAD_DIM, axis=1)
    return [jnp.where(lo, x, r).astype(BF16), jnp.where(lo, r, x).astype(BF16)]


def _stack_heads(q, g):
    lo = lax.broadcasted_iota(jnp.int32, (q.shape[0], LANES), 1) < HEAD_DIM
    scale_lo = jnp.where(lo, LOGIT_SCALE, 0.0).astype(F32)
    scale_hi = jnp.where(lo, 0.0, LOGIT_SCALE).astype(F32)
    base = g * GQA_GROUP * HEAD_DIM
    p0 = q[:, base:base + LANES]
    p1 = q[:, base + LANES:base + 2 * LANES]
    return jnp.concatenate([p0 * scale_lo, p0 * scale_hi, p1 * scale_lo, p1 * scale_hi], axis=0)


def _attn_scores(q, kdup, g, bias4):
    return lax.dot_general(_stack_heads(q, g).astype(BF16), kdup, (((1,), (1,)), ((), ())),
                           preferred_element_type=F32) + bias4


def _sink_column(r, g, sinks_ref):
    row4 = lax.broadcasted_iota(jnp.int32, (GQA_GROUP * r, 1), 0)
    sk = jnp.full((GQA_GROUP * r, 1), sinks_ref[g * GQA_GROUP + GQA_GROUP - 1], F32)
    for hh in range(GQA_GROUP - 2, -1, -1):
        sk = jnp.where(row4 < (hh + 1) * r, sinks_ref[g * GQA_GROUP + hh], sk)
    return sk * LOG2_E


def _sink_softmax(s, sk):
    m = jnp.maximum(jnp.max(s, axis=-1, keepdims=True), sk)
    e = jnp.exp2(s - m)
    den = jnp.sum(e, axis=-1, keepdims=True) + jnp.exp2(sk - m)
    return e, 1.0 / den


def _unstack_heads(o):
    r = o.shape[0] // GQA_GROUP
    lo = lax.broadcasted_iota(jnp.int32, (r, LANES), 1) < HEAD_DIM
    return jnp.concatenate([jnp.where(lo, o[0:r], o[r:2 * r]),
                            jnp.where(lo, o[2 * r:3 * r], o[3 * r:4 * r])], axis=1)


def _software_pipeline(n_units, stages, after_tick=()):
    vals = {}
    for tick in range(n_units + len(stages) - 1):
        for k in reversed(range(len(stages))):
            u = tick - k
            if 0 <= u < n_units:
                vals[u] = stages[k](u, vals.get(u))
        if tick < len(after_tick) and after_tick[tick] is not None:
            after_tick[tick]()


def _band_bias4(r, s, first_key=0):
    i = lax.broadcasted_iota(jnp.int32, (GQA_GROUP * r, s), 0) & (r - 1)
    j = lax.broadcasted_iota(jnp.int32, (GQA_GROUP * r, s), 1)
    valid = (j >= jnp.maximum(i, first_key)) & (j <= i + WINDOW)
    return jnp.where(valid, 0.0, -jnp.inf).astype(F32)


def _window_sums(e, w):
    step = 1
    while step < w:
        e = e + pltpu.roll(e, step, axis=0)
        step *= 2
    return e


def _sigmoid(x):
    return 0.5 * jnp.tanh(0.5 * x) + 0.5


def _gate_merge(pa, po, gl):
    ga = _sigmoid(gl[:, :D_MODEL])
    gb = _sigmoid(gl[:, D_MODEL:])
    return (ga * pa + gb * po).astype(BF16)


def _merge(x, pool_a, attn_o, gl, wbp_ref, wba_ref, wout_ref):
    pa = _dot(pool_a.astype(BF16), wbp_ref[...])
    po = _dot(attn_o.astype(BF16), wba_ref[...])
    return x + _dot(_gate_merge(pa, po, gl), wout_ref[...])


def _mix_prompt_kernel(sinks_ref, u_ref, q_ref, k_ref, v_ref, gl_ref, wpool_ref, pscale_ref,
                       wbp_ref, wba_ref, wout_ref, xo_ref, o_ref, kt_ref, vt_ref, uext, kdup, vdup,
                       oscr, bias, pa_scr, mixed, *, tq, nt):
    s = pl.program_id(0)
    i = s % nt
    pre = POOL_STATE + 1

    @pl.when(s == 0)
    def _():
        mixed[...] = jnp.zeros(mixed.shape, BF16)

    @pl.when(i == 0)
    def _():
        uext[0:pre, :] = jnp.zeros((pre, POOL_WIDTH), F32)
        kdup[:, 0:WINDOW, :] = jnp.zeros((N_KV_HEADS, WINDOW, LANES), BF16)
        vdup[:, 0:WINDOW, :] = jnp.zeros((N_KV_HEADS, WINDOW, LANES), BF16)
        bias[0] = _band_bias4(WINDOW, 2 * WINDOW, WINDOW)
        bias[1] = _band_bias4(WINDOW, 2 * WINDOW)

    uext[pre:pre + tq, :] = u_ref[...]
    for dst, src in ((kdup, k_ref), (vdup, v_ref)):
        for g, d in enumerate(_dup_kv_heads(src[...])):
            dst[g, WINDOW:WINDOW + tq, :] = d
    sink_cols = [_sink_column(WINDOW, g, sinks_ref) for g in range(N_KV_HEADS)]

    piece = D_MODEL // 4

    def out_prev(n):
        cols = slice(n * piece, (n + 1) * piece)
        o_ref[:, cols] = xo_ref[:, cols] + _dot(mixed[...], wout_ref[:, cols])

    pos1 = i * tq + lax.broadcasted_iota(jnp.int32, (tq, 1), 0) + 1
    a_parts = []

    def pool_group(g):
        w = POOL_WINDOWS[g]
        cols = slice(g * POOL_GROUP, (g + 1) * POOL_GROUP)
        e = uext[:, cols]
        inv_cnt = 1.0 / jnp.minimum(pos1, w).astype(F32)
        d = _window_sums(e, w)[pre:] * inv_cnt - e[pre:]
        y = _dot(d.astype(BF16), wpool_ref[g])
        a_parts.append((y * pscale_ref[:, cols]).astype(BF16))

    n_units = N_KV_HEADS * tq // WINDOW

    def pooled_branch(n):
        cols = slice(n * 2 * piece, (n + 1) * 2 * piece)
        pa_scr[:, cols] = _dot(jnp.concatenate(a_parts, axis=1), wbp_ref[:, cols])

    def early(n):
        out_prev(n)
        pool_group(n)

    def merge_half(half):
        rows = slice(half * tq // 2, (half + 1) * tq // 2)
        po = _dot(oscr[rows, :].astype(BF16), wba_ref[...])
        mixed[rows, :] = _gate_merge(pa_scr[rows, :], po, gl_ref[rows, :])

    half_done = n_units // 2 + 1
    after_tick = [functools.partial(early, n) for n in range(4)]
    after_tick += [functools.partial(pooled_branch, n) for n in range(2)]
    assert len(after_tick) <= half_done + 1
    after_tick += [None] * (half_done + 1 - len(after_tick)) + [functools.partial(merge_half, 0)]

    def unit(u):
        nb, g = divmod(u, N_KV_HEADS)
        return nb, g, slice(nb * WINDOW, (nb + 1) * WINDOW), slice(nb * WINDOW, (nb + 2) * WINDOW)

    def scores(u, _):
        nb, g, rows, keys = unit(u)
        slot = jnp.minimum(i * (tq // WINDOW) + nb, 1)
        return _attn_scores(q_ref[rows, :], kdup[g, keys, :], g, bias[slot])

    def softmax(u, s):
        e, inv = _sink_softmax(s, sink_cols[unit(u)[1]])
        return e.astype(BF16), inv

    def values(u, weights):
        nb, g, rows, keys = unit(u)
        e, inv = weights
        oscr[rows, g * 2 * LANES:(g + 1) * 2 * LANES] = _unstack_heads(
            _dot(e, vdup[g, keys, :]) * inv)

    _software_pipeline(n_units, [scores, softmax, values], after_tick)
    merge_half(1)

    @pl.when(i == nt - 1)
    def _():
        kt_ref[0] = k_ref[tq - WINDOW:tq, :].T
        vt_ref[0] = v_ref[tq - WINDOW:tq, :].T

    uext[0:pre, :] = uext[tq:tq + pre, :]
    kdup[:, 0:WINDOW, :] = kdup[:, tq:tq + WINDOW, :]
    vdup[:, 0:WINDOW, :] = vdup[:, tq:tq + WINDOW, :]


def _mix_prompt(x, u, q, k, v, gl, sinks, wpool_bf, pscale, wbp_bf, wba_bf, wout_bf, batch, tq):
    n = x.shape[0]
    n_tiles = n // tq
    nt = n_tiles // batch
    row = lambda c: pl.BlockSpec((tq, c), lambda s: (jnp.minimum(s, n_tiles - 1), 0))
    row_prev = pl.BlockSpec((tq, D_MODEL), lambda s: (jnp.maximum(s - 1, 0), 0))
    last_t = pl.BlockSpec((1, KV_WIDTH, WINDOW),
                          lambda s: (jnp.minimum(s, n_tiles - 1) // nt, 0, 0))
    return pl.pallas_call(
        functools.partial(_mix_prompt_kernel, tq=tq, nt=nt),
        grid=(n_tiles + 1,),
        in_specs=[pl.BlockSpec(memory_space=pltpu.SMEM),
                  row(POOL_WIDTH), row(ATTN_WIDTH), row(KV_WIDTH), row(KV_WIDTH), row(2 * D_MODEL),
                  _const_spec(wpool_bf.shape), _const_spec(pscale.shape),
                  _const_spec(wbp_bf.shape), _const_spec(wba_bf.shape), _const_spec(wout_bf.shape),
                  row_prev],
        out_specs=[row_prev, last_t, last_t],
        out_shape=[jax.ShapeDtypeStruct((n, D_MODEL), F32),
                   jax.ShapeDtypeStruct((batch, KV_WIDTH, WINDOW), F32),
                   jax.ShapeDtypeStruct((batch, KV_WIDTH, WINDOW), F32)],
        scratch_shapes=[pltpu.VMEM((POOL_STATE + 1 + tq, POOL_WIDTH), F32),
                        pltpu.VMEM((N_KV_HEADS, WINDOW + tq, LANES), BF16),
                        pltpu.VMEM((N_KV_HEADS, WINDOW + tq, LANES), BF16),
                        pltpu.VMEM((tq, ATTN_WIDTH), F32),
                        pltpu.VMEM((2, GQA_GROUP * WINDOW, 2 * WINDOW), F32),
                        pltpu.VMEM((tq, D_MODEL), F32),
                        pltpu.VMEM((tq, D_MODEL), BF16)],
        compiler_params=_params(1),
        name="mix_prompt",
    )(sinks, u, q, k, v, gl, wpool_bf, pscale, wbp_bf, wba_bf, wout_bf, x)


def _tile_shift(cur, prev, j, row):
    return jnp.where(row < j, pltpu.roll(prev, j, axis=1), pltpu.roll(cur, j, axis=1))


SEQ_BLOCK = WINDOW // SUBLANES


def _sample_bias4():
    shape = (GQA_GROUP * WINDOW, 2 * WINDOW)
    i = lax.broadcasted_iota(jnp.int32, shape, 0) & (WINDOW - 1)
    j = lax.broadcasted_iota(jnp.int32, shape, 1)
    b, t = i >> 3, i & (SUBLANES - 1)
    new = j - WINDOW
    valid = ((j < WINDOW) & (j >= t)) | ((new >> 3 == b) & ((new & (SUBLANES - 1)) <= t) & (new >= 0))
    return jnp.where(valid, 0.0, -jnp.inf).astype(F32)


def _attend_cached(q, knew, vnew, ckt_ref, cvt_ref, b0, bias4, sinks_ref):
    r, t = WINDOW, SUBLANES
    nt_dims = (((1,), (1,)), ((), ()))
    kn, vn = _dup_kv_heads(knew), _dup_kv_heads(vnew)

    def per_seq(x):
        return [jnp.concatenate([x[j * r + b * t:j * r + (b + 1) * t] for j in range(GQA_GROUP)],
                                axis=0) for b in range(SEQ_BLOCK)]

    def stacked(xs):
        return jnp.concatenate([xs[b][j * t:(j + 1) * t] for j in range(GQA_GROUP)
                                for b in range(SEQ_BLOCK)], axis=0)

    outs = []
    for g in range(N_KV_HEADS):
        kv_rows = slice(g * HEAD_DIM, (g + 1) * HEAD_DIM)
        qs = _stack_heads(q, g)
        s_new = lax.dot_general(qs.astype(BF16), kn[g], nt_dims, preferred_element_type=F32)
        s_cache = []
        for b, qb in enumerate(per_seq(qs)):
            kt = ckt_ref[b0 + b, kv_rows, :]
            s_cache.append(_dot(qb.astype(BF16), jnp.concatenate([kt, kt], axis=0).astype(BF16)))
        s = jnp.concatenate([stacked(s_cache), s_new], axis=1) + bias4
        p, inv = _sink_softmax(s, _sink_column(r, g, sinks_ref))
        o_cache = []
        for b, pb in enumerate(per_seq(p[:, :r])):
            vt = cvt_ref[b0 + b, kv_rows, :]
            o_cache.append(lax.dot_general(pb.astype(BF16),
                                           jnp.concatenate([vt, vt], axis=0).astype(BF16),
                                           nt_dims, preferred_element_type=F32))
        o = stacked(o_cache) + _dot(p[:, r:].astype(BF16), vn[g])
        outs.append(_unstack_heads(o * inv))
    return jnp.concatenate(outs, axis=1)


def _slide_cache(new_rows, ct_ref, out_ref, b0):
    t = SUBLANES
    new_t = new_rows.T
    is_new = lax.broadcasted_iota(jnp.int32, (KV_WIDTH, WINDOW), 1) >= WINDOW - t
    for b in range(SEQ_BLOCK):
        kept = pltpu.roll(ct_ref[b0 + b], WINDOW - t, axis=1)
        fresh = pltpu.roll(new_t, (WINDOW - t - b * t) % WINDOW, axis=1)
        out_ref[b0 + b] = jnp.where(is_new, fresh, kept)


def _mix_sample_kernel(sinks_ref, x_ref, u_ref, st_ref, q_ref, k_ref, v_ref, ckt_ref, cvt_ref, gl_ref,
                       wpool_ref, pscale_ref, wbp_ref, wba_ref, wout_ref, o_ref, kst_ref, vst_ref,
                       oscr, *, gs, pos0):
    t = SUBLANES
    rows = gs * t

    row = lax.broadcasted_iota(jnp.int32, (gs, t, POOL_GROUP), 1)
    cnt_pos = pos0 + lax.broadcasted_iota(jnp.int32, (gs, t, 1), 1) + 1
    a_parts = []
    for g, w in enumerate(POOL_WINDOWS):
        cols = slice(g * POOL_GROUP, (g + 1) * POOL_GROUP)
        tiles = [st_ref[:, 0:t, cols], st_ref[:, t:2 * t, cols], u_ref[:, :, cols]]
        xt = tiles[2]
        step = 1
        while step < w and step < t:
            tiles = [tiles[n] + _tile_shift(tiles[n], tiles[max(n - 1, 0)], step, row)
                     for n in range(3)]
            step *= 2
        acc = tiles[2] + tiles[1] if w == 2 * t else tiles[2]
        cnt = jnp.minimum(cnt_pos, w).astype(F32)
        d = (acc / cnt - xt).reshape(rows, POOL_GROUP)
        y = _dot(d.astype(BF16), wpool_ref[g])
        a_parts.append(y * pscale_ref[:, cols])
    pool_a = jnp.concatenate(a_parts, axis=1)

    bias = _sample_bias4()
    for blk in range(gs // SEQ_BLOCK):
        rs = slice(blk * WINDOW, (blk + 1) * WINDOW)
        oscr[rs, :] = _attend_cached(q_ref[rs, :], k_ref[rs, :], v_ref[rs, :], ckt_ref, cvt_ref,
                                     blk * SEQ_BLOCK, bias, sinks_ref)
        _slide_cache(k_ref[rs, :], ckt_ref, kst_ref, blk * SEQ_BLOCK)
        _slide_cache(v_ref[rs, :], cvt_ref, vst_ref, blk * SEQ_BLOCK)

    o_ref[...] = _merge(x_ref[...], pool_a, oscr[...], gl_ref[...], wbp_ref, wba_ref, wout_ref)


def _mix_sample(x, u3, stpad, q, k, v, ck, cv, gl, sinks, wpool_bf, pscale, wbp_bf, wba_bf, wout_bf,
                gs, pos0):
    nseq, t = u3.shape[0], u3.shape[1]
    rows = gs * t
    row = lambda c: pl.BlockSpec((rows, c), lambda i: (i, 0))
    seq = lambda a, b: pl.BlockSpec((gs, a, b), lambda i: (i, 0, 0))
    return pl.pallas_call(
        functools.partial(_mix_sample_kernel, gs=gs, pos0=pos0),
        grid=(nseq // gs,),
        in_specs=[pl.BlockSpec(memory_space=pltpu.SMEM),
                  row(D_MODEL), seq(t, POOL_WIDTH), seq(2 * t, POOL_WIDTH),
                  row(ATTN_WIDTH), row(KV_WIDTH), row(KV_WIDTH),
                  seq(WINDOW, KV_WIDTH), seq(WINDOW, KV_WIDTH), row(2 * D_MODEL),
                  _const_spec(wpool_bf.shape), _const_spec(pscale.shape),
                  _const_spec(wbp_bf.shape), _const_spec(wba_bf.shape), _const_spec(wout_bf.shape)],
        out_specs=[row(D_MODEL), seq(KV_WIDTH, WINDOW), seq(KV_WIDTH, WINDOW)],
        out_shape=[jax.ShapeDtypeStruct((nseq * t, D_MODEL), F32),
                   jax.ShapeDtypeStruct((nseq, KV_WIDTH, WINDOW), F32),
                   jax.ShapeDtypeStruct((nseq, KV_WIDTH, WINDOW), F32)],
        scratch_shapes=[pltpu.VMEM((rows, ATTN_WIDTH), F32)],
        compiler_params=_params(1),
        name="mix_sample",
    )(sinks, x, u3, stpad, q, k, v, ck, cv, gl, wpool_bf, pscale, wbp_bf, wba_bf, wout_bf)


FF_CHUNK = 256


def _ff_cols(c, part):
    start = part * D_FF + c * FF_CHUNK
    return slice(start, start + FF_CHUNK)


def _gated_act(taps, cw_ref, cb_ref, c):
    def conv(part):
        x0, x1, x2 = taps(c, part)
        cols = _ff_cols(c, part)
        return (x0 * cw_ref[0:1, cols] + x1 * cw_ref[1:2, cols] + x2 * cw_ref[2:3, cols]
                + cb_ref[:, cols])
    return (jax.nn.silu(conv(0)) * conv(1)).astype(BF16)


def _ffn_prompt_kernel(x_ref, g_ref, wup_ref, cw_ref, cb_ref, wdn_ref, gf_ref, o_ref, cs_ref,
                       upx, act, *, tm):
    i = pl.program_id(1)
    pre = SUBLANES

    @pl.when(i == 0)
    def _():
        upx[0:pre, :] = jnp.zeros((pre, 2 * D_FF), F32)

    x = x_ref[...]
    h = _rmsnorm(x, g_ref[...]).astype(BF16)
    upx[pre:pre + tm, :] = _dot(h, wup_ref[...])

    def taps(c, part):
        e = upx[:, _ff_cols(c, part)]
        return pltpu.roll(e, 2, axis=0)[pre:], pltpu.roll(e, 1, axis=0)[pre:], e[pre:]

    for c in range(D_FF // FF_CHUNK):
        act[:, c * FF_CHUNK:(c + 1) * FF_CHUNK] = _gated_act(taps, cw_ref, cb_ref, c)

    y = x + _dot(act[...], wdn_ref[...])
    o_ref[...] = _rmsnorm(y, gf_ref[...])
    cs_ref[0] = upx[pre + tm - CONV_STATE:pre + tm, :]
    upx[0:pre, :] = upx[tm:tm + pre, :]


def _ffn_prompt(x, g_ffn, wup_bf, conv_w, conv_b, wdn_bf, g_final, batch, tm):
    n = x.shape[0]
    nt = n // batch // tm
    row = pl.BlockSpec((tm, D_MODEL), lambda b, i: (b * nt + i, 0))
    return pl.pallas_call(
        functools.partial(_ffn_prompt_kernel, tm=tm),
        grid=(batch, nt),
        in_specs=[row, _const_spec((1, D_MODEL)), _const_spec(wup_bf.shape),
                  _const_spec(conv_w.shape), _const_spec(conv_b.shape), _const_spec(wdn_bf.shape),
                  _const_spec((1, D_MODEL))],
        out_specs=[row, pl.BlockSpec((1, CONV_STATE, 2 * D_FF), lambda b, i: (b, 0, 0))],
        out_shape=[jax.ShapeDtypeStruct((n, D_MODEL), F32),
                   jax.ShapeDtypeStruct((batch, CONV_STATE, 2 * D_FF), F32)],
        scratch_shapes=[pltpu.VMEM((SUBLANES + tm, 2 * D_FF), F32),
                        pltpu.VMEM((tm, D_FF), BF16)],
        compiler_params=_params(2),
        name="ffn_prompt",
    )(x, g_ffn, wup_bf, conv_w, conv_b, wdn_bf, g_final)


def _ffn_sample_kernel(x_ref, st_ref, g_ref, wup_ref, cw_ref, cb_ref, wdn_ref, gf_ref,
                       o_ref, cs_ref, up, act, *, gs):
    t = SUBLANES
    rows = gs * t
    x = x_ref[...]
    h = _rmsnorm(x, g_ref[...]).astype(BF16)
    up[...] = _dot(h, wup_ref[...])
    row = lax.broadcasted_iota(jnp.int32, (gs, t, FF_CHUNK), 1)

    def taps(c, part):
        cols = _ff_cols(c, part)
        x2 = up[:, cols].reshape(gs, t, FF_CHUNK)
        s0 = jnp.broadcast_to(st_ref[:, 0:1, cols], x2.shape)
        s1 = jnp.broadcast_to(st_ref[:, 1:2, cols], x2.shape)
        x1 = jnp.where(row < 1, s1, pltpu.roll(x2, 1, axis=1))
        x0 = jnp.where(row < 1, s0, jnp.where(row < 2, s1, pltpu.roll(x2, 2, axis=1)))
        flat = lambda a: a.reshape(rows, FF_CHUNK)
        return flat(x0), flat(x1), flat(x2)

    for c in range(D_FF // FF_CHUNK):
        act[:, c * FF_CHUNK:(c + 1) * FF_CHUNK] = _gated_act(taps, cw_ref, cb_ref, c)

    y = x + _dot(act[...], wdn_ref[...])
    o_ref[...] = _rmsnorm(y, gf_ref[...])
    for c0 in range(0, 2 * D_FF, 4 * FF_CHUNK):
        cols = slice(c0, min(c0 + 4 * FF_CHUNK, 2 * D_FF))
        cs_ref[:, :, cols] = up[:, cols].reshape(gs, t, -1)[:, t - CONV_STATE:, :]


def _ffn_sample(x, st, g_ffn, wup_bf, conv_w, conv_b, wdn_bf, g_final, gs):
    nseq = st.shape[0]
    t = x.shape[0] // nseq
    rows = gs * t
    row = pl.BlockSpec((rows, D_MODEL), lambda i: (i, 0))
    st_spec = pl.BlockSpec((gs, CONV_STATE, 2 * D_FF), lambda i: (i, 0, 0))
    return pl.pallas_call(
        functools.partial(_ffn_sample_kernel, gs=gs),
        grid=(nseq // gs,),
        in_specs=[row, st_spec,
                  _const_spec((1, D_MODEL)), _const_spec(wup_bf.shape),
                  _const_spec(conv_w.shape), _const_spec(conv_b.shape), _const_spec(wdn_bf.shape),
                  _const_spec((1, D_MODEL))],
        out_specs=[row, st_spec],
        out_shape=[jax.ShapeDtypeStruct((nseq * t, D_MODEL), F32),
                   jax.ShapeDtypeStruct((nseq, CONV_STATE, 2 * D_FF), F32)],
        scratch_shapes=[pltpu.VMEM((rows, 2 * D_FF), F32), pltpu.VMEM((rows, D_FF), BF16)],
        compiler_params=_params(1),
        name="ffn_sample",
    )(x, st, g_ffn, wup_bf, conv_w, conv_b, wdn_bf, g_final)


TM_IN = 1024
TQ_MIX = 512
TM_FFN = 512
GS_MIX = 32
GS_FFN = 64


def kernel(x_prompt, x_sample, state_pool, cache_k, cache_v, state_conv, g_mix, w_in, w_pool_grp,
           pool_scale, sinks, w_branch_pool, w_branch_attn, w_out, g_ffn, w_up, conv_w, conv_b,
           w_down, g_final):
    depth = g_mix.shape[0]
    batch, seq, _ = x_prompt.shape
    dec_batch, dec_seq, _ = x_sample.shape
    assert dec_seq == SUBLANES and seq % TQ_MIX == 0 and seq % TM_FFN == 0
    assert depth == 1

    yp = x_prompt.reshape(batch * seq, D_MODEL)
    ys = x_sample.reshape(dec_batch * dec_seq, D_MODEL)
    outs = [[] for _ in range(8)]
    gfin = g_final.reshape(1, D_MODEL)
    for l in range(depth):
        gmix = g_mix[l].reshape(1, D_MODEL)
        gffn = g_ffn[l].reshape(1, D_MODEL)
        w_in_bf = w_in[l].astype(BF16)
        wpool_bf = w_pool_grp[l].astype(BF16)
        pscale = pool_scale[l].reshape(1, POOL_WIDTH)
        wbp_bf = w_branch_pool[l].astype(BF16)
        wba_bf = w_branch_attn[l].astype(BF16)
        wout_bf = w_out[l].astype(BF16)
        wup_bf = w_up[l].astype(BF16)
        wdn_bf = w_down[l].astype(BF16)
        cw = conv_w[l]
        cb = conv_b[l].reshape(1, 2 * D_FF)

        u, q, k, v, gl = _in_proj(yp, gmix, w_in_bf, TM_IN)
        x1, kt, vt = _mix_prompt(yp, u, q, k, v, gl, sinks[l], wpool_bf, pscale, wbp_bf, wba_bf,
                                 wout_bf, batch, TQ_MIX)
        yp, conv_p = _ffn_prompt(x1, gffn, wup_bf, cw, cb, wdn_bf, gfin, batch, TM_FFN)
        outs[0].append(u.reshape(batch, seq, POOL_WIDTH)[:, seq - POOL_STATE:])
        untranspose = lambda t: t.reshape(-1, N_KV_HEADS, HEAD_DIM, WINDOW).transpose(0, 3, 1, 2)
        outs[1].append(untranspose(kt))
        outs[2].append(untranspose(vt))
        outs[3].append(conv_p)

        us, qs, ks, vs, gls = _in_proj(ys, gmix, w_in_bf, TM_IN)
        u3 = us.reshape(dec_batch, dec_seq, POOL_WIDTH)
        stpad = jnp.pad(state_pool[l], ((0, 0), (1, 0), (0, 0)))
        transposed = lambda c: c.transpose(0, 2, 3, 1).reshape(dec_batch, KV_WIDTH, WINDOW)
        ck, cv = transposed(cache_k[l]), transposed(cache_v[l])
        x1s, kst, vst = _mix_sample(ys, u3, stpad, qs, ks, vs, ck, cv, gls, sinks[l], wpool_bf,
                                    pscale, wbp_bf, wba_bf, wout_bf, GS_MIX, PAST_LEN)
        ys, conv_s = _ffn_sample(x1s, state_conv[l], gffn, wup_bf, cw, cb, wdn_bf, gfin, GS_FFN)
        outs[4].append(jnp.concatenate([state_pool[l], u3], axis=1)[:, dec_seq:])
        outs[5].append(untranspose(kst))
        outs[6].append(untranspose(vst))
        outs[7].append(conv_s)
    return (yp.reshape(batch, seq, D_MODEL), ys.reshape(dec_batch, dec_seq, D_MODEL),
            *[jnp.stack(o) for o in outs])
```

```python
import functools

import jax
import jax.numpy as jnp
from jax import lax
from jax.experimental import pallas as pl
from jax.experimental.pallas import tpu as pltpu

D_MODEL = 1024
POOL_WIDTH = 512
POOL_WINDOWS = (2, 4, 8, 16)
POOL_GROUP = 128
POOL_STATE = 15
HEAD_DIM = 64
N_HEADS = 8
N_KV_HEADS = 2
GQA_GROUP = N_HEADS // N_KV_HEADS
WINDOW = 128
ATTN_WIDTH = N_HEADS * HEAD_DIM
KV_WIDTH = N_KV_HEADS * HEAD_DIM
ATTN_SCALE = HEAD_DIM ** -0.5
LOG2_E = 1.4426950408889634
LOGIT_SCALE = ATTN_SCALE * LOG2_E
D_FF = 2816
CONV_STATE = 2
EPS = 1e-6
PAST_LEN = 16384

LANES = 128
SUBLANES = 8
VMEM_LIMIT = 56 * 1024 * 1024

F32 = jnp.float32
BF16 = jnp.bfloat16


def _rmsnorm(x, g):
    inv = lax.rsqrt(jnp.mean(x * x, axis=-1, keepdims=True) + EPS)
    return (x * inv) * g


def _dot(a, b):
    return jnp.dot(a, b, preferred_element_type=F32)


def _const_spec(shape):
    nd = len(shape)
    return pl.BlockSpec(shape, lambda *_: (0,) * nd, pipeline_mode=pl.Buffered(1))


def _params(n_axes):
    return pltpu.CompilerParams(dimension_semantics=("arbitrary",) * n_axes,
                                vmem_limit_bytes=VMEM_LIMIT)


def _in_proj_kernel(x_ref, g_ref, w_ref, u_ref, q_ref, k_ref, v_ref, gl_ref):
    h = _rmsnorm(x_ref[...], g_ref[...]).astype(BF16)
    o_q = POOL_WIDTH
    o_k = o_q + ATTN_WIDTH
    o_gl = o_k + 2 * KV_WIDTH
    u_ref[...] = _dot(h, w_ref[:, 0:o_q])
    q_ref[...] = _dot(h, w_ref[:, o_q:o_k])
    kv = _dot(h, w_ref[:, o_k:o_gl])
    k_ref[...] = kv[:, :KV_WIDTH]
    v_ref[...] = kv[:, KV_WIDTH:]
    gl_ref[...] = _dot(h, w_ref[:, o_gl:])


def _in_proj(x, g_mix, w_in_bf, tm):
    n = x.shape[0]
    in_width = w_in_bf.shape[1]
    row = lambda c: pl.BlockSpec((tm, c), lambda i: (i, 0))
    return pl.pallas_call(
        _in_proj_kernel,
        grid=(n // tm,),
        in_specs=[row(D_MODEL), _const_spec((1, D_MODEL)), _const_spec((D_MODEL, in_width))],
        out_specs=[row(POOL_WIDTH), row(ATTN_WIDTH), row(KV_WIDTH), row(KV_WIDTH), row(2 * D_MODEL)],
        out_shape=[jax.ShapeDtypeStruct((n, POOL_WIDTH), F32),
                   jax.ShapeDtypeStruct((n, ATTN_WIDTH), F32),
                   jax.ShapeDtypeStruct((n, KV_WIDTH), F32),
                   jax.ShapeDtypeStruct((n, KV_WIDTH), F32),
                   jax.ShapeDtypeStruct((n, 2 * D_MODEL), F32)],
        compiler_params=_params(1),
        name="in_proj",
    )(x, g_mix, w_in_bf)


def _dup_kv_heads(x):
    lo = lax.broadcasted_iota(jnp.int32, x.shape, 1) < HEAD_DIM
    r = pltpu.roll(x, HEAD_DIM, axis=1)
    return [jnp.where(lo, x, r).astype(BF16), jnp.where(lo, r, x).astype(BF16)]


def _stack_heads(q, g):
    lo = lax.broadcasted_iota(jnp.int32, (q.shape[0], LANES), 1) < HEAD_DIM
    scale_lo = jnp.where(lo, LOGIT_SCALE, 0.0).astype(F32)
    scale_hi = jnp.where(lo, 0.0, LOGIT_SCALE).astype(F32)
    base = g * GQA_GROUP * HEAD_DIM
    p0 = q[:, base:base + LANES]
    p1 = q[:, base + LANES:base + 2 * LANES]
    return jnp.concatenate([p0 * scale_lo, p0 * scale_hi, p1 * scale_lo, p1 * scale_hi], axis=0)


def _attn_scores(q, kdup, g, bias4):
    return lax.dot_general(_stack_heads(q, g).astype(BF16), kdup, (((1,), (1,)), ((), ())),
                           preferred_element_type=F32) + bias4


def _sink_column(r, g, sinks_ref):
    row4 = lax.broadcasted_iota(jnp.int32, (GQA_GROUP * r, 1), 0)
    sk = jnp.full((GQA_GROUP * r, 1), sinks_ref[g * GQA_GROUP + GQA_GROUP - 1], F32)
    for hh in range(GQA_GROUP - 2, -1, -1):
        sk = jnp.where(row4 < (hh + 1) * r, sinks_ref[g * GQA_GROUP + hh], sk)
    return sk * LOG2_E


def _sink_softmax(s, sk):
    m = jnp.maximum(jnp.max(s, axis=-1, keepdims=True), sk)
    e = jnp.exp2(s - m)
    den = jnp.sum(e, axis=-1, keepdims=True) + jnp.exp2(sk - m)
    return e, 1.0 / den


def _unstack_heads(o):
    r = o.shape[0] // GQA_GROUP
    lo = lax.broadcasted_iota(jnp.int32, (r, LANES), 1) < HEAD_DIM
    return jnp.concatenate([jnp.where(lo, o[0:r], o[r:2 * r]),
                            jnp.where(lo, o[2 * r:3 * r], o[3 * r:4 * r])], axis=1)


def _software_pipeline(n_units, stages, after_tick=()):
    vals = {}
    for tick in range(n_units + len(stages) - 1):
        for k in reversed(range(len(stages))):
            u = tick - k
            if 0 <= u < n_units:
                vals[u] = stages[k](u, vals.get(u))
        if tick < len(after_tick) and after_tick[tick] is not None:
            after_tick[tick]()


def _band_bias4(r, s, first_key=0):
    i = lax.broadcasted_iota(jnp.int32, (GQA_GROUP * r, s), 0) & (r - 1)
    j = lax.broadcasted_iota(jnp.int32, (GQA_GROUP * r, s), 1)
    valid = (j >= jnp.maximum(i, first_key)) & (j <= i + WINDOW)
    return jnp.where(valid, 0.0, -jnp.inf).astype(F32)


def _window_sums(e, w):
    step = 1
    while step < w:
        e = e + pltpu.roll(e, step, axis=0)
        step *= 2
    return e


def _sigmoid(x):
    return 0.5 * jnp.tanh(0.5 * x) + 0.5


def _gate_merge(pa, po, gl):
    ga = _sigmoid(gl[:, :D_MODEL])
    gb = _sigmoid(gl[:, D_MODEL:])
    return (ga * pa + gb * po).astype(BF16)


def _merge(x, pool_a, attn_o, gl, wbp_ref, wba_ref, wout_ref):
    pa = _dot(pool_a.astype(BF16), wbp_ref[...])
    po = _dot(attn_o.astype(BF16), wba_ref[...])
    return x + _dot(_gate_merge(pa, po, gl), wout_ref[...])


def _mix_prompt_kernel(sinks_ref, u_ref, q_ref, k_ref, v_ref, gl_ref, wpool_ref, pscale_ref,
                       wbp_ref, wba_ref, wout_ref, xo_ref, o_ref, kt_ref, vt_ref, uext, kdup, vdup,
                       oscr, bias, pa_scr, mixed, *, tq, nt):
    s = pl.program_id(0)
    i = s % nt
    pre = POOL_STATE + 1

    @pl.when(s == 0)
    def _():
        mixed[...] = jnp.zeros(mixed.shape, BF16)

    @pl.when(i == 0)
    def _():
        uext[0:pre, :] = jnp.zeros((pre, POOL_WIDTH), F32)
        kdup[:, 0:WINDOW, :] = jnp.zeros((N_KV_HEADS, WINDOW, LANES), BF16)
        vdup[:, 0:WINDOW, :] = jnp.zeros((N_KV_HEADS, WINDOW, LANES), BF16)
        bias[0] = _band_bias4(WINDOW, 2 * WINDOW, WINDOW)
        bias[1] = _band_bias4(WINDOW, 2 * WINDOW)

    uext[pre:pre + tq, :] = u_ref[...]
    for dst, src in ((kdup, k_ref), (vdup, v_ref)):
        for g, d in enumerate(_dup_kv_heads(src[...])):
            dst[g, WINDOW:WINDOW + tq, :] = d
    sink_cols = [_sink_column(WINDOW, g, sinks_ref) for g in range(N_KV_HEADS)]

    piece = D_MODEL // 4

    def out_prev(n):
        cols = slice(n * piece, (n + 1) * piece)
        o_ref[:, cols] = xo_ref[:, cols] + _dot(mixed[...], wout_ref[:, cols])

    pos1 = i * tq + lax.broadcasted_iota(jnp.int32, (tq, 1), 0) + 1
    a_parts = []

    def pool_group(g):
        w = POOL_WINDOWS[g]
        cols = slice(g * POOL_GROUP, (g + 1) * POOL_GROUP)
        e = uext[:, cols]
        inv_cnt = 1.0 / jnp.minimum(pos1, w).astype(F32)
        d = _window_sums(e, w)[pre:] * inv_cnt - e[pre:]
        y = _dot(d.astype(BF16), wpool_ref[g])
        a_parts.append((y * pscale_ref[:, cols]).astype(BF16))

    n_units = N_KV_HEADS * tq // WINDOW

    def pooled_branch(n):
        cols = slice(n * 2 * piece, (n + 1) * 2 * piece)
        pa_scr[:, cols] = _dot(jnp.concatenate(a_parts, axis=1), wbp_ref[:, cols])

    def early(n):
        out_prev(n)
        pool_group(n)

    def merge_half(half):
        rows = slice(half * tq // 2, (half + 1) * tq // 2)
        po = _dot(oscr[rows, :].astype(BF16), wba_ref[...])
        mixed[rows, :] = _gate_merge(pa_scr[rows, :], po, gl_ref[rows, :])

    half_done = n_units // 2 + 1
    after_tick = [functools.partial(early, n) for n in range(4)]
    after_tick += [functools.partial(pooled_branch, n) for n in range(2)]
    assert len(after_tick) <= half_done + 1
    after_tick += [None] * (half_done + 1 - len(after_tick)) + [functools.partial(merge_half, 0)]

    def unit(u):
        nb, g = divmod(u, N_KV_HEADS)
        return nb, g, slice(nb * WINDOW, (nb + 1) * WINDOW), slice(nb * WINDOW, (nb + 2) * WINDOW)

    def scores(u, _):
        nb, g, rows, keys = unit(u)
        slot = jnp.minimum(i * (tq // WINDOW) + nb, 1)
        return _attn_scores(q_ref[rows, :], kdup[g, keys, :], g, bias[slot])

    def softmax(u, s):
        e, inv = _sink_softmax(s, sink_cols[unit(u)[1]])
        return e.astype(BF16), inv

    def values(u, weights):
        nb, g, rows, keys = unit(u)
        e, inv = weights
        oscr[rows, g * 2 * LANES:(g + 1) * 2 * LANES] = _unstack_heads(
            _dot(e, vdup[g, keys, :]) * inv)

    _software_pipeline(n_units, [scores, softmax, values], after_tick)
    merge_half(1)

    @pl.when(i == nt - 1)
    def _():
        kt_ref[0] = k_ref[tq - WINDOW:tq, :].T
        vt_ref[0] = v_ref[tq - WINDOW:tq, :].T

    uext[0:pre, :] = uext[tq:tq + pre, :]
    kdup[:, 0:WINDOW, :] = kdup[:, tq:tq + WINDOW, :]
    vdup[:, 0:WINDOW, :] = vdup[:, tq:tq + WINDOW, :]


def _mix_prompt(x, u, q, k, v, gl, sinks, wpool_bf, pscale, wbp_bf, wba_bf, wout_bf, batch, tq):
    n = x.shape[0]
    n_tiles = n // tq
    nt = n_tiles // batch
    row = lambda c: pl.BlockSpec((tq, c), lambda s: (jnp.minimum(s, n_tiles - 1), 0))
    row_prev = pl.BlockSpec((tq, D_MODEL), lambda s: (jnp.maximum(s - 1, 0), 0))
    last_t = pl.BlockSpec((1, KV_WIDTH, WINDOW),
                          lambda s: (jnp.minimum(s, n_tiles - 1) // nt, 0, 0))
    return pl.pallas_call(
        functools.partial(_mix_prompt_kernel, tq=tq, nt=nt),
        grid=(n_tiles + 1,),
        in_specs=[pl.BlockSpec(memory_space=pltpu.SMEM),
                  row(POOL_WIDTH), row(ATTN_WIDTH), row(KV_WIDTH), row(KV_WIDTH), row(2 * D_MODEL),
                  _const_spec(wpool_bf.shape), _const_spec(pscale.shape),
                  _const_spec(wbp_bf.shape), _const_spec(wba_bf.shape), _const_spec(wout_bf.shape),
                  row_prev],
        out_specs=[row_prev, last_t, last_t],
        out_shape=[jax.ShapeDtypeStruct((n, D_MODEL), F32),
                   jax.ShapeDtypeStruct((batch, KV_WIDTH, WINDOW), F32),
                   jax.ShapeDtypeStruct((batch, KV_WIDTH, WINDOW), F32)],
        scratch_shapes=[pltpu.VMEM((POOL_STATE + 1 + tq, POOL_WIDTH), F32),
                        pltpu.VMEM((N_KV_HEADS, WINDOW + tq, LANES), BF16),
                        pltpu.VMEM((N_KV_HEADS, WINDOW + tq, LANES), BF16),
                        pltpu.VMEM((tq, ATTN_WIDTH), F32),
                        pltpu.VMEM((2, GQA_GROUP * WINDOW, 2 * WINDOW), F32),
                        pltpu.VMEM((tq, D_MODEL), F32),
                        pltpu.VMEM((tq, D_MODEL), BF16)],
        compiler_params=_params(1),
        name="mix_prompt",
    )(sinks, u, q, k, v, gl, wpool_bf, pscale, wbp_bf, wba_bf, wout_bf, x)


def _tile_shift(cur, prev, j, row):
    return jnp.where(row < j, pltpu.roll(prev, j, axis=1), pltpu.roll(cur, j, axis=1))


SEQ_BLOCK = WINDOW // SUBLANES


def _sample_bias4():
    shape = (GQA_GROUP * WINDOW, 2 * WINDOW)
    i = lax.broadcasted_iota(jnp.int32, shape, 0) & (WINDOW - 1)
    j = lax.broadcasted_iota(jnp.int32, shape, 1)
    b, t = i >> 3, i & (SUBLANES - 1)
    new = j - WINDOW
    valid = ((j < WINDOW) & (j >= t)) | ((new >> 3 == b) & ((new & (SUBLANES - 1)) <= t) & (new >= 0))
    return jnp.where(valid, 0.0, -jnp.inf).astype(F32)


def _attend_cached(q, knew, vnew, ckt_ref, cvt_ref, b0, bias4, sinks_ref):
    r, t = WINDOW, SUBLANES
    nt_dims = (((1,), (1,)), ((), ()))
    kn, vn = _dup_kv_heads(knew), _dup_kv_heads(vnew)

    def per_seq(x):
        return [jnp.concatenate([x[j * r + b * t:j * r + (b + 1) * t] for j in range(GQA_GROUP)],
                                axis=0) for b in range(SEQ_BLOCK)]

    def stacked(xs):
        return jnp.concatenate([xs[b][j * t:(j + 1) * t] for j in range(GQA_GROUP)
                                for b in range(SEQ_BLOCK)], axis=0)

    outs = []
    for g in range(N_KV_HEADS):
        kv_rows = slice(g * HEAD_DIM, (g + 1) * HEAD_DIM)
        qs = _stack_heads(q, g)
        s_new = lax.dot_general(qs.astype(BF16), kn[g], nt_dims, preferred_element_type=F32)
        s_cache = []
        for b, qb in enumerate(per_seq(qs)):
            kt = ckt_ref[b0 + b, kv_rows, :]
            s_cache.append(_dot(qb.astype(BF16), jnp.concatenate([kt, kt], axis=0).astype(BF16)))
        s = jnp.concatenate([stacked(s_cache), s_new], axis=1) + bias4
        p, inv = _sink_softmax(s, _sink_column(r, g, sinks_ref))
        o_cache = []
        for b, pb in enumerate(per_seq(p[:, :r])):
            vt = cvt_ref[b0 + b, kv_rows, :]
            o_cache.append(lax.dot_general(pb.astype(BF16),
                                           jnp.concatenate([vt, vt], axis=0).astype(BF16),
                                           nt_dims, preferred_element_type=F32))
        o = stacked(o_cache) + _dot(p[:, r:].astype(BF16), vn[g])
        outs.append(_unstack_heads(o * inv))
    return jnp.concatenate(outs, axis=1)


def _slide_cache(new_rows, ct_ref, out_ref, b0):
    t = SUBLANES
    new_t = new_rows.T
    is_new = lax.broadcasted_iota(jnp.int32, (KV_WIDTH, WINDOW), 1) >= WINDOW - t
    for b in range(SEQ_BLOCK):
        kept = pltpu.roll(ct_ref[b0 + b], WINDOW - t, axis=1)
        fresh = pltpu.roll(new_t, (WINDOW - t - b * t) % WINDOW, axis=1)
        out_ref[b0 + b] = jnp.where(is_new, fresh, kept)


def _mix_sample_kernel(sinks_ref, x_ref, u_ref, st_ref, q_ref, k_ref, v_ref, ckt_ref, cvt_ref, gl_ref,
                       wpool_ref, pscale_ref, wbp_ref, wba_ref, wout_ref, o_ref, kst_ref, vst_ref,
                       oscr, *, gs, pos0):
    t = SUBLANES
    rows = gs * t

    row = lax.broadcasted_iota(jnp.int32, (gs, t, POOL_GROUP), 1)
    cnt_pos = pos0 + lax.broadcasted_iota(jnp.int32, (gs, t, 1), 1) + 1
    a_parts = []
    for g, w in enumerate(POOL_WINDOWS):
        cols = slice(g * POOL_GROUP, (g + 1) * POOL_GROUP)
        tiles = [st_ref[:, 0:t, cols], st_ref[:, t:2 * t, cols], u_ref[:, :, cols]]
        xt = tiles[2]
        step = 1
        while step < w and step < t:
            tiles = [tiles[n] + _tile_shift(tiles[n], tiles[max(n - 1, 0)], step, row)
                     for n in range(3)]
            step *= 2
        acc = tiles[2] + tiles[1] if w == 2 * t else tiles[2]
        cnt = jnp.minimum(cnt_pos, w).astype(F32)
        d = (acc / cnt - xt).reshape(rows, POOL_GROUP)
        y = _dot(d.astype(BF16), wpool_ref[g])
        a_parts.append(y * pscale_ref[:, cols])
    pool_a = jnp.concatenate(a_parts, axis=1)

    bias = _sample_bias4()
    for blk in range(gs // SEQ_BLOCK):
        rs = slice(blk * WINDOW, (blk + 1) * WINDOW)
        oscr[rs, :] = _attend_cached(q_ref[rs, :], k_ref[rs, :], v_ref[rs, :], ckt_ref, cvt_ref,
                                     blk * SEQ_BLOCK, bias, sinks_ref)
        _slide_cache(k_ref[rs, :], ckt_ref, kst_ref, blk * SEQ_BLOCK)
        _slide_cache(v_ref[rs, :], cvt_ref, vst_ref, blk * SEQ_BLOCK)

    o_ref[...] = _merge(x_ref[...], pool_a, oscr[...], gl_ref[...], wbp_ref, wba_ref, wout_ref)


def _mix_sample(x, u3, stpad, q, k, v, ck, cv, gl, sinks, wpool_bf, pscale, wbp_bf, wba_bf, wout_bf,
                gs, pos0):
    nseq, t = u3.shape[0], u3.shape[1]
    rows = gs * t
    row = lambda c: pl.BlockSpec((rows, c), lambda i: (i, 0))
    seq = lambda a, b: pl.BlockSpec((gs, a, b), lambda i: (i, 0, 0))
    return pl.pallas_call(
        functools.partial(_mix_sample_kernel, gs=gs, pos0=pos0),
        grid=(nseq // gs,),
        in_specs=[pl.BlockSpec(memory_space=pltpu.SMEM),
                  row(D_MODEL), seq(t, POOL_WIDTH), seq(2 * t, POOL_WIDTH),
                  row(ATTN_WIDTH), row(KV_WIDTH), row(KV_WIDTH),
                  seq(WINDOW, KV_WIDTH), seq(WINDOW, KV_WIDTH), row(2 * D_MODEL),
                  _const_spec(wpool_bf.shape), _const_spec(pscale.shape),
                  _const_spec(wbp_bf.shape), _const_spec(wba_bf.shape), _const_spec(wout_bf.shape)],
        out_specs=[row(D_MODEL), seq(KV_WIDTH, WINDOW), seq(KV_WIDTH, WINDOW)],
        out_shape=[jax.ShapeDtypeStruct((nseq * t, D_MODEL), F32),
                   jax.ShapeDtypeStruct((nseq, KV_WIDTH, WINDOW), F32),
                   jax.ShapeDtypeStruct((nseq, KV_WIDTH, WINDOW), F32)],
        scratch_shapes=[pltpu.VMEM((rows, ATTN_WIDTH), F32)],
        compiler_params=_params(1),
        name="mix_sample",
    )(sinks, x, u3, stpad, q, k, v, ck, cv, gl, wpool_bf, pscale, wbp_bf, wba_bf, wout_bf)


FF_CHUNK = 256


def _ff_cols(c, part):
    start = part * D_FF + c * FF_CHUNK
    return slice(start, start + FF_CHUNK)


def _gated_act(taps, cw_ref, cb_ref, c):
    def conv(part):
        x0, x1, x2 = taps(c, part)
        cols = _ff_cols(c, part)
        return (x0 * cw_ref[0:1, cols] + x1 * cw_ref[1:2, cols] + x2 * cw_ref[2:3, cols]
                + cb_ref[:, cols])
    h = conv(0)
    return ((h * (1.0 + jnp.tanh(h))) * conv(1)).astype(BF16)


def _ffn_prompt_kernel(x_ref, g_ref, wup_ref, cw_ref, cb_ref, wdn_ref, gf_ref, o_ref, cs_ref,
                       upx, act, *, tm):
    i = pl.program_id(1)
    pre = SUBLANES

    @pl.when(i == 0)
    def _():
        upx[0:pre, :] = jnp.zeros((pre, 2 * D_FF), F32)

    x = x_ref[...]
    h = _rmsnorm(x, g_ref[...]).astype(BF16)
    upx[pre:pre + tm, :] = _dot(h, wup_ref[...])

    def taps(c, part):
        e = upx[:, _ff_cols(c, part)]
        return pltpu.roll(e, 2, axis=0)[pre:], pltpu.roll(e, 1, axis=0)[pre:], e[pre:]

    for c in range(D_FF // FF_CHUNK):
        act[:, c * FF_CHUNK:(c + 1) * FF_CHUNK] = _gated_act(taps, cw_ref, cb_ref, c)

    y = x + _dot(act[...], wdn_ref[...])
    o_ref[...] = _rmsnorm(y, gf_ref[...])
    cs_ref[0] = upx[pre + tm - CONV_STATE:pre + tm, :]
    upx[0:pre, :] = upx[tm:tm + pre, :]


def _ffn_prompt(x, g_ffn, wup_bf, conv_w, conv_b, wdn_bf, g_final, batch, tm):
    n = x.shape[0]
    nt = n // batch // tm
    row = pl.BlockSpec((tm, D_MODEL), lambda b, i: (b * nt + i, 0))
    return pl.pallas_call(
        functools.partial(_ffn_prompt_kernel, tm=tm),
        grid=(batch, nt),
        in_specs=[row, _const_spec((1, D_MODEL)), _const_spec(wup_bf.shape),
                  _const_spec(conv_w.shape), _const_spec(conv_b.shape), _const_spec(wdn_bf.shape),
                  _const_spec((1, D_MODEL))],
        out_specs=[row, pl.BlockSpec((1, CONV_STATE, 2 * D_FF), lambda b, i: (b, 0, 0))],
        out_shape=[jax.ShapeDtypeStruct((n, D_MODEL), F32),
                   jax.ShapeDtypeStruct((batch, CONV_STATE, 2 * D_FF), F32)],
        scratch_shapes=[pltpu.VMEM((SUBLANES + tm, 2 * D_FF), F32),
                        pltpu.VMEM((tm, D_FF), BF16)],
        compiler_params=_params(2),
        name="ffn_prompt",
    )(x, g_ffn, wup_bf, conv_w, conv_b, wdn_bf, g_final)


def _ffn_sample_kernel(x_ref, st_ref, g_ref, wup_ref, cw_ref, cb_ref, wdn_ref, gf_ref,
                       o_ref, cs_ref, up, act, *, gs):
    t = SUBLANES
    rows = gs * t
    x = x_ref[...]
    h = _rmsnorm(x, g_ref[...]).astype(BF16)
    up[...] = _dot(h, wup_ref[...])
    row = lax.broadcasted_iota(jnp.int32, (gs, t, FF_CHUNK), 1)

    def taps(c, part):
        cols = _ff_cols(c, part)
        x2 = up[:, cols].reshape(gs, t, FF_CHUNK)
        s0 = jnp.broadcast_to(st_ref[:, 0:1, cols], x2.shape)
        s1 = jnp.broadcast_to(st_ref[:, 1:2, cols], x2.shape)
        x1 = jnp.where(row < 1, s1, pltpu.roll(x2, 1, axis=1))
        x0 = jnp.where(row < 1, s0, jnp.where(row < 2, s1, pltpu.roll(x2, 2, axis=1)))
        flat = lambda a: a.reshape(rows, FF_CHUNK)
        return flat(x0), flat(x1), flat(x2)

    for c in range(D_FF // FF_CHUNK):
        act[:, c * FF_CHUNK:(c + 1) * FF_CHUNK] = _gated_act(taps, cw_ref, cb_ref, c)

    y = x + _dot(act[...], wdn_ref[...])
    o_ref[...] = _rmsnorm(y, gf_ref[...])
    for c0 in range(0, 2 * D_FF, 4 * FF_CHUNK):
        cols = slice(c0, min(c0 + 4 * FF_CHUNK, 2 * D_FF))
        cs_ref[:, :, cols] = up[:, cols].reshape(gs, t, -1)[:, t - CONV_STATE:, :]


def _ffn_sample(x, st, g_ffn, wup_bf, conv_w, conv_b, wdn_bf, g_final, gs):
    nseq = st.shape[0]
    t = x.shape[0] // nseq
    rows = gs * t
    row = pl.BlockSpec((rows, D_MODEL), lambda i: (i, 0))
    st_spec = pl.BlockSpec((gs, CONV_STATE, 2 * D_FF), lambda i: (i, 0, 0))
    return pl.pallas_call(
        functools.partial(_ffn_sample_kernel, gs=gs),
        grid=(nseq // gs,),
        in_specs=[row, st_spec,
                  _const_spec((1, D_MODEL)), _const_spec(wup_bf.shape),
                  _const_spec(conv_w.shape), _const_spec(conv_b.shape), _const_spec(wdn_bf.shape),
                  _const_spec((1, D_MODEL))],
        out_specs=[row, st_spec],
        out_shape=[jax.ShapeDtypeStruct((nseq * t, D_MODEL), F32),
                   jax.ShapeDtypeStruct((nseq, CONV_STATE, 2 * D_FF), F32)],
        scratch_shapes=[pltpu.VMEM((rows, 2 * D_FF), F32), pltpu.VMEM((rows, D_FF), BF16)],
        compiler_params=_params(1),
        name="ffn_sample",
    )(x, st, g_ffn, wup_bf, conv_w, conv_b, wdn_bf, g_final)


TM_IN = 1024
TQ_MIX = 512
TM_FFN = 512
GS_MIX = 32
GS_FFN = 64


def kernel(x_prompt, x_sample, state_pool, cache_k, cache_v, state_conv, g_mix, w_in, w_pool_grp,
           pool_scale, sinks, w_branch_pool, w_branch_attn, w_out, g_ffn, w_up, conv_w, conv_b,
           w_down, g_final):
    depth = g_mix.shape[0]
    batch, seq, _ = x_prompt.shape
    dec_batch, dec_seq, _ = x_sample.shape
    assert dec_seq == SUBLANES and seq % TQ_MIX == 0 and seq % TM_FFN == 0
    assert depth == 1

    yp = x_prompt.reshape(batch * seq, D_MODEL)
    ys = x_sample.reshape(dec_batch * dec_seq, D_MODEL)
    outs = [[] for _ in range(8)]
    gfin = g_final.reshape(1, D_MODEL)
    for l in range(depth):
        gmix = g_mix[l].reshape(1, D_MODEL)
        gffn = g_ffn[l].reshape(1, D_MODEL)
        w_in_bf = w_in[l].astype(BF16)
        wpool_bf = w_pool_grp[l].astype(BF16)
        pscale = pool_scale[l].reshape(1, POOL_WIDTH)
        wbp_bf = w_branch_pool[l].astype(BF16)
        wba_bf = w_branch_attn[l].astype(BF16)
        wout_bf = w_out[l].astype(BF16)
        wup_bf = w_up[l].astype(BF16)
        wdn_bf = w_down[l].astype(BF16)
        half_gate = jnp.where(jnp.arange(2 * D_FF) < D_FF, 0.5, 1.0).astype(F32)
        cw = conv_w[l] * half_gate
        cb = (conv_b[l] * half_gate).reshape(1, 2 * D_FF)

        u, q, k, v, gl = _in_proj(yp, gmix, w_in_bf, TM_IN)
        x1, kt, vt = _mix_prompt(yp, u, q, k, v, gl, sinks[l], wpool_bf, pscale, wbp_bf, wba_bf,
                                 wout_bf, batch, TQ_MIX)
        yp, conv_p = _ffn_prompt(x1, gffn, wup_bf, cw, cb, wdn_bf, gfin, batch, TM_FFN)
        outs[0].append(u.reshape(batch, seq, POOL_WIDTH)[:, seq - POOL_STATE:])
        untranspose = lambda t: t.reshape(-1, N_KV_HEADS, HEAD_DIM, WINDOW).transpose(0, 3, 1, 2)
        outs[1].append(untranspose(kt))
        outs[2].append(untranspose(vt))
        outs[3].append(conv_p)

        us, qs, ks, vs, gls = _in_proj(ys, gmix, w_in_bf, TM_IN)
        u3 = us.reshape(dec_batch, dec_seq, POOL_WIDTH)
        stpad = jnp.pad(state_pool[l], ((0, 0), (1, 0), (0, 0)))
        transposed = lambda c: c.transpose(0, 2, 3, 1).reshape(dec_batch, KV_WIDTH, WINDOW)
        ck, cv = transposed(cache_k[l]), transposed(cache_v[l])
        x1s, kst, vst = _mix_sample(ys, u3, stpad, qs, ks, vs, ck, cv, gls, sinks[l], wpool_bf,
                                    pscale, wbp_bf, wba_bf, wout_bf, GS_MIX, PAST_LEN)
        ys, conv_s = _ffn_sample(x1s, state_conv[l], gffn, wup_bf, cw, cb, wdn_bf, gfin, GS_FFN)
        outs[4].append(jnp.concatenate([state_pool[l], u3], axis=1)[:, dec_seq:])
        outs[5].append(untranspose(kst))
        outs[6].append(untranspose(vst))
        outs[7].append(conv_s)
    return (yp.reshape(batch, seq, D_MODEL), ys.reshape(dec_batch, dec_seq, D_MODEL),
            *[jnp.stack(o) for o in outs])
```

```python
import functools

import jax
import jax.numpy as jnp
from jax import lax
from jax.experimental import pallas as pl
from jax.experimental.pallas import tpu as pltpu

D_MODEL = 1024
POOL_WIDTH = 512
POOL_WINDOWS = (2, 4, 8, 16)
POOL_GROUP = 128
POOL_STATE = 15
HEAD_DIM = 64
N_HEADS = 8
N_KV_HEADS = 2
GQA_GROUP = N_HEADS // N_KV_HEADS
WINDOW = 128
ATTN_WIDTH = N_HEADS * HEAD_DIM
KV_WIDTH = N_KV_HEADS * HEAD_DIM
ATTN_SCALE = HEAD_DIM ** -0.5
LOG2_E = 1.4426950408889634
LOGIT_SCALE = ATTN_SCALE * LOG2_E
D_FF = 2816
CONV_STATE = 2
EPS = 1e-6
PAST_LEN = 16384

LANES = 128
SUBLANES = 8
VMEM_LIMIT = 56 * 1024 * 1024

F32 = jnp.float32
BF16 = jnp.bfloat16


def _rmsnorm(x, g):
    inv = lax.rsqrt(jnp.mean(x * x, axis=-1, keepdims=True) + EPS)
    return (x * inv) * g


def _dot(a, b):
    return jnp.dot(a, b, preferred_element_type=F32)


def _const_spec(shape):
    nd = len(shape)
    return pl.BlockSpec(shape, lambda *_: (0,) * nd, pipeline_mode=pl.Buffered(1))


def _params(n_axes):
    return pltpu.CompilerParams(dimension_semantics=("arbitrary",) * n_axes,
                                vmem_limit_bytes=VMEM_LIMIT)


def _in_proj_kernel(x_ref, g_ref, w_ref, u_ref, q_ref, k_ref, v_ref, gl_ref):
    h = _rmsnorm(x_ref[...], g_ref[...]).astype(BF16)
    o_q = POOL_WIDTH
    o_k = o_q + ATTN_WIDTH
    o_gl = o_k + 2 * KV_WIDTH
    u_ref[...] = _dot(h, w_ref[:, 0:o_q])
    q_ref[...] = _dot(h, w_ref[:, o_q:o_k])
    kv = _dot(h, w_ref[:, o_k:o_gl])
    k_ref[...] = kv[:, :KV_WIDTH]
    v_ref[...] = kv[:, KV_WIDTH:]
    gl_ref[...] = _dot(h, w_ref[:, o_gl:])


def _in_proj(x, g_mix, w_in_bf, tm):
    n = x.shape[0]
    in_width = w_in_bf.shape[1]
    row = lambda c: pl.BlockSpec((tm, c), lambda i: (i, 0))
    return pl.pallas_call(
        _in_proj_kernel,
        grid=(n // tm,),
        in_specs=[row(D_MODEL), _const_spec((1, D_MODEL)), _const_spec((D_MODEL, in_width))],
        out_specs=[row(POOL_WIDTH), row(ATTN_WIDTH), row(KV_WIDTH), row(KV_WIDTH), row(2 * D_MODEL)],
        out_shape=[jax.ShapeDtypeStruct((n, POOL_WIDTH), F32),
                   jax.ShapeDtypeStruct((n, ATTN_WIDTH), F32),
                   jax.ShapeDtypeStruct((n, KV_WIDTH), F32),
                   jax.ShapeDtypeStruct((n, KV_WIDTH), F32),
                   jax.ShapeDtypeStruct((n, 2 * D_MODEL), F32)],
        compiler_params=_params(1),
        name="in_proj",
    )(x, g_mix, w_in_bf)


def _dup_kv_heads(x):
    lo = lax.broadcasted_iota(jnp.int32, x.shape, 1) < HEAD_DIM
    r = pltpu.roll(x, HEAD_DIM, axis=1)
    return [jnp.where(lo, x, r).astype(BF16), jnp.where(lo, r, x).astype(BF16)]


def _stack_heads(q, g):
    lo = lax.broadcasted_iota(jnp.int32, (q.shape[0], LANES), 1) < HEAD_DIM
    scale_lo = jnp.where(lo, LOGIT_SCALE, 0.0).astype(F32)
    scale_hi = jnp.where(lo, 0.0, LOGIT_SCALE).astype(F32)
    base = g * GQA_GROUP * HEAD_DIM
    p0 = q[:, base:base + LANES]
    p1 = q[:, base + LANES:base + 2 * LANES]
    return jnp.concatenate([p0 * scale_lo, p0 * scale_hi, p1 * scale_lo, p1 * scale_hi], axis=0)


def _attn_scores(q, kdup, g, bias4):
    return lax.dot_general(_stack_heads(q, g).astype(BF16), kdup, (((1,), (1,)), ((), ())),
                           preferred_element_type=F32) + bias4


def _sink_column(r, g, sinks_ref):
    row4 = lax.broadcasted_iota(jnp.int32, (GQA_GROUP * r, 1), 0)
    sk = jnp.full((GQA_GROUP * r, 1), sinks_ref[g * GQA_GROUP + GQA_GROUP - 1], F32)
    for hh in range(GQA_GROUP - 2, -1, -1):
        sk = jnp.where(row4 < (hh + 1) * r, sinks_ref[g * GQA_GROUP + hh], sk)
    return sk * LOG2_E


def _sink_softmax(s, sk):
    m = jnp.maximum(jnp.max(s, axis=-1, keepdims=True), sk)
    e = jnp.exp2(s - m)
    den = jnp.sum(e, axis=-1, keepdims=True) + jnp.exp2(sk - m)
    return e, 1.0 / den


def _unstack_heads(o):
    r = o.shape[0] // GQA_GROUP
    lo = lax.broadcasted_iota(jnp.int32, (r, LANES), 1) < HEAD_DIM
    return jnp.concatenate([jnp.where(lo, o[0:r], o[r:2 * r]),
                            jnp.where(lo, o[2 * r:3 * r], o[3 * r:4 * r])], axis=1)


def _software_pipeline(n_units, stages, after_tick=()):
    vals = {}
    for tick in range(n_units + len(stages) - 1):
        for k in reversed(range(len(stages))):
            u = tick - k
            if 0 <= u < n_units:
                vals[u] = stages[k](u, vals.get(u))
        if tick < len(after_tick) and after_tick[tick] is not None:
            after_tick[tick]()


def _band_bias4(r, s, first_key=0):
    i = lax.broadcasted_iota(jnp.int32, (GQA_GROUP * r, s), 0) & (r - 1)
    j = lax.broadcasted_iota(jnp.int32, (GQA_GROUP * r, s), 1)
    valid = (j >= jnp.maximum(i, first_key)) & (j <= i + WINDOW)
    return jnp.where(valid, 0.0, -jnp.inf).astype(F32)


def _window_sums(e, w):
    step = 1
    while step < w:
        e = e + pltpu.roll(e, step, axis=0)
        step *= 2
    return e


def _sigmoid(x):
    return 0.5 * jnp.tanh(0.5 * x) + 0.5


def _gate_merge(pa, po, gl):
    ga = _sigmoid(gl[:, :D_MODEL])
    gb = _sigmoid(gl[:, D_MODEL:])
    return (ga * pa + gb * po).astype(BF16)


def _merge(x, pool_a, attn_o, gl, wbp_ref, wba_ref, wout_ref):
    pa = _dot(pool_a.astype(BF16), wbp_ref[...])
    po = _dot(attn_o.astype(BF16), wba_ref[...])
    return x + _dot(_gate_merge(pa, po, gl), wout_ref[...])


def _mix_prompt_kernel(sinks_ref, u_ref, q_ref, k_ref, v_ref, gl_ref, wpool_ref, pscale_ref,
                       wbp_ref, wba_ref, wout_ref, xo_ref, o_ref, kt_ref, vt_ref, uext, kdup, vdup,
                       oscr, bias, pa_scr, mixed, *, tq, nt):
    s = pl.program_id(0)
    i = s % nt
    pre = POOL_STATE + 1

    @pl.when(s == 0)
    def _():
        mixed[...] = jnp.zeros(mixed.shape, BF16)

    @pl.when(i == 0)
    def _():
        uext[0:pre, :] = jnp.zeros((pre, POOL_WIDTH), F32)
        kdup[:, 0:WINDOW, :] = jnp.zeros((N_KV_HEADS, WINDOW, LANES), BF16)
        vdup[:, 0:WINDOW, :] = jnp.zeros((N_KV_HEADS, WINDOW, LANES), BF16)
        bias[0] = _band_bias4(WINDOW, 2 * WINDOW, WINDOW)
        bias[1] = _band_bias4(WINDOW, 2 * WINDOW)

    uext[pre:pre + tq, :] = u_ref[...]
    for dst, src in ((kdup, k_ref), (vdup, v_ref)):
        for g, d in enumerate(_dup_kv_heads(src[...])):
            dst[g, WINDOW:WINDOW + tq, :] = d
    sink_cols = [_sink_column(WINDOW, g, sinks_ref) for g in range(N_KV_HEADS)]

    piece = D_MODEL // 4

    def out_prev(n):
        cols = slice(n * piece, (n + 1) * piece)
        o_ref[:, cols] = xo_ref[:, cols] + _dot(mixed[...], wout_ref[:, cols])

    pos1 = i * tq + lax.broadcasted_iota(jnp.int32, (tq, 1), 0) + 1
    a_parts = []

    def pool_group(g):
        w = POOL_WINDOWS[g]
        cols = slice(g * POOL_GROUP, (g + 1) * POOL_GROUP)
        e = uext[:, cols]
        inv_cnt = 1.0 / jnp.minimum(pos1, w).astype(F32)
        d = _window_sums(e, w)[pre:] * inv_cnt - e[pre:]
        y = _dot(d.astype(BF16), wpool_ref[g])
        a_parts.append((y * pscale_ref[:, cols]).astype(BF16))

    n_units = N_KV_HEADS * tq // WINDOW

    def pooled_branch(n):
        cols = slice(n * 2 * piece, (n + 1) * 2 * piece)
        pa_scr[:, cols] = _dot(jnp.concatenate(a_parts, axis=1), wbp_ref[:, cols])

    def early(n):
        out_prev(n)
        pool_group(n)

    def merge_half(half):
        rows = slice(half * tq // 2, (half + 1) * tq // 2)
        po = _dot(oscr[rows, :].astype(BF16), wba_ref[...])
        mixed[rows, :] = _gate_merge(pa_scr[rows, :], po, gl_ref[rows, :])

    half_done = n_units // 2 + 1
    after_tick = [functools.partial(early, n) for n in range(4)]
    after_tick += [functools.partial(pooled_branch, n) for n in range(2)]
    assert len(after_tick) <= half_done + 1
    after_tick += [None] * (half_done + 1 - len(after_tick)) + [functools.partial(merge_half, 0)]

    def unit(u):
        nb, g = divmod(u, N_KV_HEADS)
        return nb, g, slice(nb * WINDOW, (nb + 1) * WINDOW), slice(nb * WINDOW, (nb + 2) * WINDOW)

    def scores(u, _):
        nb, g, rows, keys = unit(u)
        slot = jnp.minimum(i * (tq // WINDOW) + nb, 1)
        return _attn_scores(q_ref[rows, :], kdup[g, keys, :], g, bias[slot])

    def softmax(u, s):
        e, inv = _sink_softmax(s, sink_cols[unit(u)[1]])
        return e.astype(BF16), inv

    def values(u, weights):
        nb, g, rows, keys = unit(u)
        e, inv = weights
        oscr[rows, g * 2 * LANES:(g + 1) * 2 * LANES] = _unstack_heads(
            _dot(e, vdup[g, keys, :]) * inv)

    _software_pipeline(n_units, [scores, softmax, values], after_tick)
    merge_half(1)

    @pl.when(i == nt - 1)
    def _():
        kt_ref[0] = k_ref[tq - WINDOW:tq, :].T
        vt_ref[0] = v_ref[tq - WINDOW:tq, :].T

    uext[0:pre, :] = uext[tq:tq + pre, :]
    kdup[:, 0:WINDOW, :] = kdup[:, tq:tq + WINDOW, :]
    vdup[:, 0:WINDOW, :] = vdup[:, tq:tq + WINDOW, :]


def _mix_prompt(x, u, q, k, v, gl, sinks, wpool_bf, pscale, wbp_bf, wba_bf, wout_bf, batch, tq):
    n = x.shape[0]
    n_tiles = n // tq
    nt = n_tiles // batch
    row = lambda c: pl.BlockSpec((tq, c), lambda s: (jnp.minimum(s, n_tiles - 1), 0))
    row_prev = pl.BlockSpec((tq, D_MODEL), lambda s: (jnp.maximum(s - 1, 0), 0))
    last_t = pl.BlockSpec((1, KV_WIDTH, WINDOW),
                          lambda s: (jnp.minimum(s, n_tiles - 1) // nt, 0, 0))
    return pl.pallas_call(
        functools.partial(_mix_prompt_kernel, tq=tq, nt=nt),
        grid=(n_tiles + 1,),
        in_specs=[pl.BlockSpec(memory_space=pltpu.SMEM),
                  row(POOL_WIDTH), row(ATTN_WIDTH), row(KV_WIDTH), row(KV_WIDTH), row(2 * D_MODEL),
                  _const_spec(wpool_bf.shape), _const_spec(pscale.shape),
                  _const_spec(wbp_bf.shape), _const_spec(wba_bf.shape), _const_spec(wout_bf.shape),
                  row_prev],
        out_specs=[row_prev, last_t, last_t],
        out_shape=[jax.ShapeDtypeStruct((n, D_MODEL), F32),
                   jax.ShapeDtypeStruct((batch, KV_WIDTH, WINDOW), F32),
                   jax.ShapeDtypeStruct((batch, KV_WIDTH, WINDOW), F32)],
        scratch_shapes=[pltpu.VMEM((POOL_STATE + 1 + tq, POOL_WIDTH), F32),
                        pltpu.VMEM((N_KV_HEADS, WINDOW + tq, LANES), BF16),
                        pltpu.VMEM((N_KV_HEADS, WINDOW + tq, LANES), BF16),
                        pltpu.VMEM((tq, ATTN_WIDTH), F32),
                        pltpu.VMEM((2, GQA_GROUP * WINDOW, 2 * WINDOW), F32),
                        pltpu.VMEM((tq, D_MODEL), F32),
                        pltpu.VMEM((tq, D_MODEL), BF16)],
        compiler_params=_params(1),
        name="mix_prompt",
    )(sinks, u, q, k, v, gl, wpool_bf, pscale, wbp_bf, wba_bf, wout_bf, x)


def _tile_shift(cur, prev, j, row):
    return jnp.where(row < j, pltpu.roll(prev, j, axis=1), pltpu.roll(cur, j, axis=1))


SEQ_BLOCK = WINDOW // SUBLANES


def _sample_bias4():
    shape = (GQA_GROUP * WINDOW, 2 * WINDOW)
    i = lax.broadcasted_iota(jnp.int32, shape, 0) & (WINDOW - 1)
    j = lax.broadcasted_iota(jnp.int32, shape, 1)
    b, t = i >> 3, i & (SUBLANES - 1)
    new = j - WINDOW
    valid = ((j < WINDOW) & (j >= t)) | ((new >> 3 == b) & ((new & (SUBLANES - 1)) <= t) & (new >= 0))
    return jnp.where(valid, 0.0, -jnp.inf).astype(F32)


def _attend_cached(q, knew, vnew, ckt_ref, cvt_ref, b0, bias4, sinks_ref):
    r, t = WINDOW, SUBLANES
    nt_dims = (((1,), (1,)), ((), ()))
    kn, vn = _dup_kv_heads(knew), _dup_kv_heads(vnew)

    def per_seq(x):
        return [jnp.concatenate([x[j * r + b * t:j * r + (b + 1) * t] for j in range(GQA_GROUP)],
                                axis=0) for b in range(SEQ_BLOCK)]

    def stacked(xs):
        return jnp.concatenate([xs[b][j * t:(j + 1) * t] for j in range(GQA_GROUP)
                                for b in range(SEQ_BLOCK)], axis=0)

    outs = []
    for g in range(N_KV_HEADS):
        kv_rows = slice(g * HEAD_DIM, (g + 1) * HEAD_DIM)
        qs = _stack_heads(q, g)
        s_new = lax.dot_general(qs.astype(BF16), kn[g], nt_dims, preferred_element_type=F32)
        s_cache = []
        for b, qb in enumerate(per_seq(qs)):
            kt = ckt_ref[b0 + b, kv_rows, :]
            s_cache.append(_dot(qb.astype(BF16), jnp.concatenate([kt, kt], axis=0).astype(BF16)))
        s = jnp.concatenate([stacked(s_cache), s_new], axis=1) + bias4
        p, inv = _sink_softmax(s, _sink_column(r, g, sinks_ref))
        o_cache = []
        for b, pb in enumerate(per_seq(p[:, :r])):
            vt = cvt_ref[b0 + b, kv_rows, :]
            o_cache.append(lax.dot_general(pb.astype(BF16),
                                           jnp.concatenate([vt, vt], axis=0).astype(BF16),
                                           nt_dims, preferred_element_type=F32))
        o = stacked(o_cache) + _dot(p[:, r:].astype(BF16), vn[g])
        outs.append(_unstack_heads(o * inv))
    return jnp.concatenate(outs, axis=1)


def _slide_cache(new_rows, ct_ref, out_ref, b0):
    t = SUBLANES
    new_t = new_rows.T
    is_new = lax.broadcasted_iota(jnp.int32, (KV_WIDTH, WINDOW), 1) >= WINDOW - t
    for b in range(SEQ_BLOCK):
        kept = pltpu.roll(ct_ref[b0 + b], WINDOW - t, axis=1)
        fresh = pltpu.roll(new_t, (WINDOW - t - b * t) % WINDOW, axis=1)
        out_ref[b0 + b] = jnp.where(is_new, fresh, kept)


def _mix_sample_kernel(sinks_ref, x_ref, u_ref, st_ref, q_ref, k_ref, v_ref, ckt_ref, cvt_ref, gl_ref,
                       wpool_ref, pscale_ref, wbp_ref, wba_ref, wout_ref, o_ref, kst_ref, vst_ref,
                       oscr, *, gs, pos0):
    t = SUBLANES
    rows = gs * t

    row = lax.broadcasted_iota(jnp.int32, (gs, t, POOL_GROUP), 1)
    cnt_pos = pos0 + lax.broadcasted_iota(jnp.int32, (gs, t, 1), 1) + 1
    a_parts = []
    for g, w in enumerate(POOL_WINDOWS):
        cols = slice(g * POOL_GROUP, (g + 1) * POOL_GROUP)
        tiles = [st_ref[:, 0:t, cols], st_ref[:, t:2 * t, cols], u_ref[:, :, cols]]
        xt = tiles[2]
        step = 1
        while step < w and step < t:
            tiles = [tiles[n] + _tile_shift(tiles[n], tiles[max(n - 1, 0)], step, row)
                     for n in range(3)]
            step *= 2
        acc = tiles[2] + tiles[1] if w == 2 * t else tiles[2]
        cnt = jnp.minimum(cnt_pos, w).astype(F32)
        d = (acc / cnt - xt).reshape(rows, POOL_GROUP)
        y = _dot(d.astype(BF16), wpool_ref[g])
        a_parts.append(y * pscale_ref[:, cols])
    pool_a = jnp.concatenate(a_parts, axis=1)

    bias = _sample_bias4()
    for blk in range(gs // SEQ_BLOCK):
        rs = slice(blk * WINDOW, (blk + 1) * WINDOW)
        oscr[rs, :] = _attend_cached(q_ref[rs, :], k_ref[rs, :], v_ref[rs, :], ckt_ref, cvt_ref,
                                     blk * SEQ_BLOCK, bias, sinks_ref)
        _slide_cache(k_ref[rs, :], ckt_ref, kst_ref, blk * SEQ_BLOCK)
        _slide_cache(v_ref[rs, :], cvt_ref, vst_ref, blk * SEQ_BLOCK)

    o_ref[...] = _merge(x_ref[...], pool_a, oscr[...], gl_ref[...], wbp_ref, wba_ref, wout_ref)


def _mix_sample(x, u3, stpad, q, k, v, ck, cv, gl, sinks, wpool_bf, pscale, wbp_bf, wba_bf, wout_bf,
                gs, pos0):
    nseq, t = u3.shape[0], u3.shape[1]
    rows = gs * t
    row = lambda c: pl.BlockSpec((rows, c), lambda i: (i, 0))
    seq = lambda a, b: pl.BlockSpec((gs, a, b), lambda i: (i, 0, 0))
    return pl.pallas_call(
        functools.partial(_mix_sample_kernel, gs=gs, pos0=pos0),
        grid=(nseq // gs,),
        in_specs=[pl.BlockSpec(memory_space=pltpu.SMEM),
                  row(D_MODEL), seq(t, POOL_WIDTH), seq(2 * t, POOL_WIDTH),
                  row(ATTN_WIDTH), row(KV_WIDTH), row(KV_WIDTH),
                  seq(WINDOW, KV_WIDTH), seq(WINDOW, KV_WIDTH), row(2 * D_MODEL),
                  _const_spec(wpool_bf.shape), _const_spec(pscale.shape),
                  _const_spec(wbp_bf.shape), _const_spec(wba_bf.shape), _const_spec(wout_bf.shape)],
        out_specs=[row(D_MODEL), seq(KV_WIDTH, WINDOW), seq(KV_WIDTH, WINDOW)],
        out_shape=[jax.ShapeDtypeStruct((nseq * t, D_MODEL), F32),
                   jax.ShapeDtypeStruct((nseq, KV_WIDTH, WINDOW), F32),
                   jax.ShapeDtypeStruct((nseq, KV_WIDTH, WINDOW), F32)],
        scratch_shapes=[pltpu.VMEM((rows, ATTN_WIDTH), F32)],
        compiler_params=_params(1),
        name="mix_sample",
    )(sinks, x, u3, stpad, q, k, v, ck, cv, gl, wpool_bf, pscale, wbp_bf, wba_bf, wout_bf)


FF_CHUNK = 1408


def _ff_cols(c, part):
    start = part * D_FF + c * FF_CHUNK
    return slice(start, start + FF_CHUNK)


def _gated_act(taps, cw_ref, cb_ref, c):
    def conv(part):
        x0, x1, x2 = taps(c, part)
        cols = _ff_cols(c, part)
        return (x0 * cw_ref[0:1, cols] + x1 * cw_ref[1:2, cols] + x2 * cw_ref[2:3, cols]
                + cb_ref[:, cols])
    h = conv(0)
    return ((h * (1.0 + jnp.tanh(h))) * conv(1)).astype(BF16)


def _ffn_prompt_kernel(x_ref, g_ref, wup_ref, cw_ref, cb_ref, wdn_ref, gf_ref, o_ref, cs_ref,
                       upx, act, *, tm):
    i = pl.program_id(1)
    pre = SUBLANES

    @pl.when(i == 0)
    def _():
        upx[0:pre, :] = jnp.zeros((pre, 2 * D_FF), F32)

    x = x_ref[...]
    h = _rmsnorm(x, g_ref[...]).astype(BF16)
    upx[pre:pre + tm, :] = _dot(h, wup_ref[...])

    def taps(c, part):
        e = upx[:, _ff_cols(c, part)]
        return pltpu.roll(e, 2, axis=0)[pre:], pltpu.roll(e, 1, axis=0)[pre:], e[pre:]

    for c in range(D_FF // FF_CHUNK):
        act[:, c * FF_CHUNK:(c + 1) * FF_CHUNK] = _gated_act(taps, cw_ref, cb_ref, c)

    y = x + _dot(act[...], wdn_ref[...])
    o_ref[...] = _rmsnorm(y, gf_ref[...])
    cs_ref[0] = upx[pre + tm - CONV_STATE:pre + tm, :]
    upx[0:pre, :] = upx[tm:tm + pre, :]


def _ffn_prompt(x, g_ffn, wup_bf, conv_w, conv_b, wdn_bf, g_final, batch, tm):
    n = x.shape[0]
    nt = n // batch // tm
    row = pl.BlockSpec((tm, D_MODEL), lambda b, i: (b * nt + i, 0))
    return pl.pallas_call(
        functools.partial(_ffn_prompt_kernel, tm=tm),
        grid=(batch, nt),
        in_specs=[row, _const_spec((1, D_MODEL)), _const_spec(wup_bf.shape),
                  _const_spec(conv_w.shape), _const_spec(conv_b.shape), _const_spec(wdn_bf.shape),
                  _const_spec((1, D_MODEL))],
        out_specs=[row, pl.BlockSpec((1, CONV_STATE, 2 * D_FF), lambda b, i: (b, 0, 0))],
        out_shape=[jax.ShapeDtypeStruct((n, D_MODEL), F32),
                   jax.ShapeDtypeStruct((batch, CONV_STATE, 2 * D_FF), F32)],
        scratch_shapes=[pltpu.VMEM((SUBLANES + tm, 2 * D_FF), F32),
                        pltpu.VMEM((tm, D_FF), BF16)],
        compiler_params=_params(2),
        name="ffn_prompt",
    )(x, g_ffn, wup_bf, conv_w, conv_b, wdn_bf, g_final)


def _ffn_sample_kernel(x_ref, st_ref, g_ref, wup_ref, cw_ref, cb_ref, wdn_ref, gf_ref,
                       o_ref, cs_ref, up, act, *, gs):
    t = SUBLANES
    rows = gs * t
    x = x_ref[...]
    h = _rmsnorm(x, g_ref[...]).astype(BF16)
    up[...] = _dot(h, wup_ref[...])
    row = lax.broadcasted_iota(jnp.int32, (gs, t, FF_CHUNK), 1)

    def taps(c, part):
        cols = _ff_cols(c, part)
        x2 = up[:, cols].reshape(gs, t, FF_CHUNK)
        s0 = jnp.broadcast_to(st_ref[:, 0:1, cols], x2.shape)
        s1 = jnp.broadcast_to(st_ref[:, 1:2, cols], x2.shape)
        x1 = jnp.where(row < 1, s1, pltpu.roll(x2, 1, axis=1))
        x0 = jnp.where(row < 1, s0, jnp.where(row < 2, s1, pltpu.roll(x2, 2, axis=1)))
        flat = lambda a: a.reshape(rows, FF_CHUNK)
        return flat(x0), flat(x1), flat(x2)

    for c in range(D_FF // FF_CHUNK):
        act[:, c * FF_CHUNK:(c + 1) * FF_CHUNK] = _gated_act(taps, cw_ref, cb_ref, c)

    y = x + _dot(act[...], wdn_ref[...])
    o_ref[...] = _rmsnorm(y, gf_ref[...])
    for c0 in range(0, 2 * D_FF, 4 * FF_CHUNK):
        cols = slice(c0, min(c0 + 4 * FF_CHUNK, 2 * D_FF))
        cs_ref[:, :, cols] = up[:, cols].reshape(gs, t, -1)[:, t - CONV_STATE:, :]


def _ffn_sample(x, st, g_ffn, wup_bf, conv_w, conv_b, wdn_bf, g_final, gs):
    nseq = st.shape[0]
    t = x.shape[0] // nseq
    rows = gs * t
    row = pl.BlockSpec((rows, D_MODEL), lambda i: (i, 0))
    st_spec = pl.BlockSpec((gs, CONV_STATE, 2 * D_FF), lambda i: (i, 0, 0))
    return pl.pallas_call(
        functools.partial(_ffn_sample_kernel, gs=gs),
        grid=(nseq // gs,),
        in_specs=[row, st_spec,
                  _const_spec((1, D_MODEL)), _const_spec(wup_bf.shape),
                  _const_spec(conv_w.shape), _const_spec(conv_b.shape), _const_spec(wdn_bf.shape),
                  _const_spec((1, D_MODEL))],
        out_specs=[row, st_spec],
        out_shape=[jax.ShapeDtypeStruct((nseq * t, D_MODEL), F32),
                   jax.ShapeDtypeStruct((nseq, CONV_STATE, 2 * D_FF), F32)],
        scratch_shapes=[pltpu.VMEM((rows, 2 * D_FF), F32), pltpu.VMEM((rows, D_FF), BF16)],
        compiler_params=_params(1),
        name="ffn_sample",
    )(x, st, g_ffn, wup_bf, conv_w, conv_b, wdn_bf, g_final)


TM_IN = 1024
TQ_MIX = 512
TM_FFN = 512
GS_MIX = 32
GS_FFN = 64


def kernel(x_prompt, x_sample, state_pool, cache_k, cache_v, state_conv, g_mix, w_in, w_pool_grp,
           pool_scale, sinks, w_branch_pool, w_branch_attn, w_out, g_ffn, w_up, conv_w, conv_b,
           w_down, g_final):
    depth = g_mix.shape[0]
    batch, seq, _ = x_prompt.shape
    dec_batch, dec_seq, _ = x_sample.shape
    assert dec_seq == SUBLANES and seq % TQ_MIX == 0 and seq % TM_FFN == 0
    assert depth == 1

    yp = x_prompt.reshape(batch * seq, D_MODEL)
    ys = x_sample.reshape(dec_batch * dec_seq, D_MODEL)
    outs = [[] for _ in range(8)]
    gfin = g_final.reshape(1, D_MODEL)
    for l in range(depth):
        gmix = g_mix[l].reshape(1, D_MODEL)
        gffn = g_ffn[l].reshape(1, D_MODEL)
        w_in_bf = w_in[l].astype(BF16)
        wpool_bf = w_pool_grp[l].astype(BF16)
        pscale = pool_scale[l].reshape(1, POOL_WIDTH)
        wbp_bf = w_branch_pool[l].astype(BF16)
        wba_bf = w_branch_attn[l].astype(BF16)
        wout_bf = w_out[l].astype(BF16)
        wup_bf = w_up[l].astype(BF16)
        wdn_bf = w_down[l].astype(BF16)
        half_gate = jnp.where(jnp.arange(2 * D_FF) < D_FF, 0.5, 1.0).astype(F32)
        cw = conv_w[l] * half_gate
        cb = (conv_b[l] * half_gate).reshape(1, 2 * D_FF)

        u, q, k, v, gl = _in_proj(yp, gmix, w_in_bf, TM_IN)
        x1, kt, vt = _mix_prompt(yp, u, q, k, v, gl, sinks[l], wpool_bf, pscale, wbp_bf, wba_bf,
                                 wout_bf, batch, TQ_MIX)
        yp, conv_p = _ffn_prompt(x1, gffn, wup_bf, cw, cb, wdn_bf, gfin, batch, TM_FFN)
        outs[0].append(u.reshape(batch, seq, POOL_WIDTH)[:, seq - POOL_STATE:])
        untranspose = lambda t: t.reshape(-1, N_KV_HEADS, HEAD_DIM, WINDOW).transpose(0, 3, 1, 2)
        outs[1].append(untranspose(kt))
        outs[2].append(untranspose(vt))
        outs[3].append(conv_p)

        us, qs, ks, vs, gls = _in_proj(ys, gmix, w_in_bf, TM_IN)
        u3 = us.reshape(dec_batch, dec_seq, POOL_WIDTH)
        stpad = jnp.pad(state_pool[l], ((0, 0), (1, 0), (0, 0)))
        transposed = lambda c: c.transpose(0, 2, 3, 1).reshape(dec_batch, KV_WIDTH, WINDOW)
        ck, cv = transposed(cache_k[l]), transposed(cache_v[l])
        x1s, kst, vst = _mix_sample(ys, u3, stpad, qs, ks, vs, ck, cv, gls, sinks[l], wpool_bf,
                                    pscale, wbp_bf, wba_bf, wout_bf, GS_MIX, PAST_LEN)
        ys, conv_s = _ffn_sample(x1s, state_conv[l], gffn, wup_bf, cw, cb, wdn_bf, gfin, GS_FFN)
        outs[4].append(jnp.concatenate([state_pool[l], u3], axis=1)[:, dec_seq:])
        outs[5].append(untranspose(kst))
        outs[6].append(untranspose(vst))
        outs[7].append(conv_s)
    return (yp.reshape(batch, seq, D_MODEL), ys.reshape(dec_batch, dec_seq, D_MODEL),
            *[jnp.stack(o) for o in outs])
```

```python
import functools

import jax
import jax.numpy as jnp
from jax import lax
from jax.experimental import pallas as pl
from jax.experimental.pallas import tpu as pltpu

D_MODEL = 1024
POOL_WIDTH = 512
POOL_WINDOWS = (2, 4, 8, 16)
POOL_GROUP = 128
POOL_STATE = 15
HEAD_DIM = 64
N_HEADS = 8
N_KV_HEADS = 2
GQA_GROUP = N_HEADS // N_KV_HEADS
WINDOW = 128
ATTN_WIDTH = N_HEADS * HEAD_DIM
KV_WIDTH = N_KV_HEADS * HEAD_DIM
ATTN_SCALE = HEAD_DIM ** -0.5
LOG2_E = 1.4426950408889634
LOGIT_SCALE = ATTN_SCALE * LOG2_E
D_FF = 2816
CONV_STATE = 2
EPS = 1e-6
PAST_LEN = 16384

LANES = 128
SUBLANES = 8
VMEM_LIMIT = 56 * 1024 * 1024

F32 = jnp.float32
BF16 = jnp.bfloat16


def _rmsnorm(x, g):
    inv = lax.rsqrt(jnp.mean(x * x, axis=-1, keepdims=True) + EPS)
    return (x * inv) * g


def _dot(a, b):
    return jnp.dot(a, b, preferred_element_type=F32)


def _const_spec(shape):
    nd = len(shape)
    return pl.BlockSpec(shape, lambda *_: (0,) * nd, pipeline_mode=pl.Buffered(1))


def _params(n_axes):
    return pltpu.CompilerParams(dimension_semantics=("arbitrary",) * n_axes,
                                vmem_limit_bytes=VMEM_LIMIT)


def _in_proj_kernel(x_ref, g_ref, w_ref, u_ref, q_ref, k_ref, v_ref, gl_ref):
    h = _rmsnorm(x_ref[...], g_ref[...]).astype(BF16)
    o_q = POOL_WIDTH
    o_k = o_q + ATTN_WIDTH
    o_gl = o_k + 2 * KV_WIDTH
    u_ref[...] = _dot(h, w_ref[:, 0:o_q])
    q_ref[...] = _dot(h, w_ref[:, o_q:o_k])
    kv = _dot(h, w_ref[:, o_k:o_gl])
    k_ref[...] = kv[:, :KV_WIDTH]
    v_ref[...] = kv[:, KV_WIDTH:]
    gl_ref[...] = _dot(h, w_ref[:, o_gl:])


def _in_proj(x, g_mix, w_in_bf, tm):
    n = x.shape[0]
    in_width = w_in_bf.shape[1]
    row = lambda c: pl.BlockSpec((tm, c), lambda i: (i, 0))
    return pl.pallas_call(
        _in_proj_kernel,
        grid=(n // tm,),
        in_specs=[row(D_MODEL), _const_spec((1, D_MODEL)), _const_spec((D_MODEL, in_width))],
        out_specs=[row(POOL_WIDTH), row(ATTN_WIDTH), row(KV_WIDTH), row(KV_WIDTH), row(2 * D_MODEL)],
        out_shape=[jax.ShapeDtypeStruct((n, POOL_WIDTH), F32),
                   jax.ShapeDtypeStruct((n, ATTN_WIDTH), F32),
                   jax.ShapeDtypeStruct((n, KV_WIDTH), F32),
                   jax.ShapeDtypeStruct((n, KV_WIDTH), F32),
                   jax.ShapeDtypeStruct((n, 2 * D_MODEL), F32)],
        compiler_params=_params(1),
        name="in_proj",
    )(x, g_mix, w_in_bf)


def _dup_kv_heads(x):
    lo = lax.broadcasted_iota(jnp.int32, x.shape, 1) < HEAD_DIM
    r = pltpu.roll(x, HEAD_DIM, axis=1)
    return [jnp.where(lo, x, r).astype(BF16), jnp.where(lo, r, x).astype(BF16)]


def _stack_heads(q, g):
    lo = lax.broadcasted_iota(jnp.int32, (q.shape[0], LANES), 1) < HEAD_DIM
    scale_lo = jnp.where(lo, LOGIT_SCALE, 0.0).astype(F32)
    scale_hi = jnp.where(lo, 0.0, LOGIT_SCALE).astype(F32)
    base = g * GQA_GROUP * HEAD_DIM
    p0 = q[:, base:base + LANES]
    p1 = q[:, base + LANES:base + 2 * LANES]
    return jnp.concatenate([p0 * scale_lo, p0 * scale_hi, p1 * scale_lo, p1 * scale_hi], axis=0)


def _attn_scores(q, kdup, g, bias4):
    return lax.dot_general(_stack_heads(q, g).astype(BF16), kdup, (((1,), (1,)), ((), ())),
                           preferred_element_type=F32) + bias4


def _sink_column(r, g, sinks_ref):
    row4 = lax.broadcasted_iota(jnp.int32, (GQA_GROUP * r, 1), 0)
    sk = jnp.full((GQA_GROUP * r, 1), sinks_ref[g * GQA_GROUP + GQA_GROUP - 1], F32)
    for hh in range(GQA_GROUP - 2, -1, -1):
        sk = jnp.where(row4 < (hh + 1) * r, sinks_ref[g * GQA_GROUP + hh], sk)
    return sk * LOG2_E


def _sink_softmax(s, sk):
    m = jnp.maximum(jnp.max(s, axis=-1, keepdims=True), sk)
    e = jnp.exp2(s - m)
    den = jnp.sum(e, axis=-1, keepdims=True) + jnp.exp2(sk - m)
    return e, 1.0 / den


def _unstack_heads(o):
    r = o.shape[0] // GQA_GROUP
    lo = lax.broadcasted_iota(jnp.int32, (r, LANES), 1) < HEAD_DIM
    return jnp.concatenate([jnp.where(lo, o[0:r], o[r:2 * r]),
                            jnp.where(lo, o[2 * r:3 * r], o[3 * r:4 * r])], axis=1)


def _software_pipeline(n_units, stages, after_tick=()):
    vals = {}
    for tick in range(n_units + len(stages) - 1):
        for k in reversed(range(len(stages))):
            u = tick - k
            if 0 <= u < n_units:
                vals[u] = stages[k](u, vals.get(u))
        if tick < len(after_tick) and after_tick[tick] is not None:
            after_tick[tick]()


def _band_bias4(r, s, first_key=0):
    i = lax.broadcasted_iota(jnp.int32, (GQA_GROUP * r, s), 0) & (r - 1)
    j = lax.broadcasted_iota(jnp.int32, (GQA_GROUP * r, s), 1)
    valid = (j >= jnp.maximum(i, first_key)) & (j <= i + WINDOW)
    return jnp.where(valid, 0.0, -jnp.inf).astype(F32)


def _window_sums(e, w):
    step = 1
    while step < w:
        e = e + pltpu.roll(e, step, axis=0)
        step *= 2
    return e


def _sigmoid(x):
    return 0.5 * jnp.tanh(0.5 * x) + 0.5


def _gate_merge(pa, po, gl):
    ga = _sigmoid(gl[:, :D_MODEL])
    gb = _sigmoid(gl[:, D_MODEL:])
    return (ga * pa + gb * po).astype(BF16)


def _merge(x, pool_a, attn_o, gl, wbp_ref, wba_ref, wout_ref):
    pa = _dot(pool_a.astype(BF16), wbp_ref[...])
    po = _dot(attn_o.astype(BF16), wba_ref[...])
    return x + _dot(_gate_merge(pa, po, gl), wout_ref[...])


def _mix_prompt_kernel(sinks_ref, u_ref, q_ref, k_ref, v_ref, gl_ref, wpool_ref, pscale_ref,
                       wbp_ref, wba_ref, wout_ref, xo_ref, o_ref, kt_ref, vt_ref, uext, kdup, vdup,
                       oscr, bias, pa_scr, mixed, *, tq, nt):
    s = pl.program_id(0)
    i = s % nt
    pre = POOL_STATE + 1

    @pl.when(s == 0)
    def _():
        mixed[...] = jnp.zeros(mixed.shape, BF16)

    @pl.when(i == 0)
    def _():
        uext[0:pre, :] = jnp.zeros((pre, POOL_WIDTH), F32)
        kdup[:, 0:WINDOW, :] = jnp.zeros((N_KV_HEADS, WINDOW, LANES), BF16)
        vdup[:, 0:WINDOW, :] = jnp.zeros((N_KV_HEADS, WINDOW, LANES), BF16)
        bias[0] = _band_bias4(WINDOW, 2 * WINDOW, WINDOW)
        bias[1] = _band_bias4(WINDOW, 2 * WINDOW)

    uext[pre:pre + tq, :] = u_ref[...]
    for dst, src in ((kdup, k_ref), (vdup, v_ref)):
        for g, d in enumerate(_dup_kv_heads(src[...])):
            dst[g, WINDOW:WINDOW + tq, :] = d
    sink_cols = [_sink_column(WINDOW, g, sinks_ref) for g in range(N_KV_HEADS)]

    piece = D_MODEL // 4

    def out_prev(n):
        cols = slice(n * piece, (n + 1) * piece)
        o_ref[:, cols] = xo_ref[:, cols] + _dot(mixed[...], wout_ref[:, cols])

    pos1 = i * tq + lax.broadcasted_iota(jnp.int32, (tq, 1), 0) + 1
    a_parts = []

    def pool_group(g):
        w = POOL_WINDOWS[g]
        cols = slice(g * POOL_GROUP, (g + 1) * POOL_GROUP)
        e = uext[:, cols]
        inv_cnt = 1.0 / jnp.minimum(pos1, w).astype(F32)
        d = _window_sums(e, w)[pre:] * inv_cnt - e[pre:]
        y = _dot(d.astype(BF16), wpool_ref[g])
        a_parts.append((y * pscale_ref[:, cols]).astype(BF16))

    n_units = N_KV_HEADS * tq // WINDOW

    def pooled_branch(n):
        cols = slice(n * 2 * piece, (n + 1) * 2 * piece)
        pa_scr[:, cols] = _dot(jnp.concatenate(a_parts, axis=1), wbp_ref[:, cols])

    def early(n):
        out_prev(n)
        pool_group(n)

    def merge_half(half):
        rows = slice(half * tq // 2, (half + 1) * tq // 2)
        po = _dot(oscr[rows, :].astype(BF16), wba_ref[...])
        mixed[rows, :] = _gate_merge(pa_scr[rows, :], po, gl_ref[rows, :])

    half_done = n_units // 2 + 1
    after_tick = [functools.partial(early, n) for n in range(4)]
    after_tick += [functools.partial(pooled_branch, n) for n in range(2)]
    assert len(after_tick) <= half_done + 1
    after_tick += [None] * (half_done + 1 - len(after_tick)) + [functools.partial(merge_half, 0)]

    def unit(u):
        nb, g = divmod(u, N_KV_HEADS)
        return nb, g, slice(nb * WINDOW, (nb + 1) * WINDOW), slice(nb * WINDOW, (nb + 2) * WINDOW)

    def scores(u, _):
        nb, g, rows, keys = unit(u)
        slot = jnp.minimum(i * (tq // WINDOW) + nb, 1)
        return _attn_scores(q_ref[rows, :], kdup[g, keys, :], g, bias[slot])

    def softmax(u, s):
        e, inv = _sink_softmax(s, sink_cols[unit(u)[1]])
        return e.astype(BF16), inv

    def values(u, weights):
        nb, g, rows, keys = unit(u)
        e, inv = weights
        oscr[rows, g * 2 * LANES:(g + 1) * 2 * LANES] = _unstack_heads(
            _dot(e, vdup[g, keys, :]) * inv)

    _software_pipeline(n_units, [scores, softmax, values], after_tick)
    merge_half(1)

    @pl.when(i == nt - 1)
    def _():
        kt_ref[0] = k_ref[tq - WINDOW:tq, :].T
        vt_ref[0] = v_ref[tq - WINDOW:tq, :].T

    uext[0:pre, :] = uext[tq:tq + pre, :]
    kdup[:, 0:WINDOW, :] = kdup[:, tq:tq + WINDOW, :]
    vdup[:, 0:WINDOW, :] = vdup[:, tq:tq + WINDOW, :]


def _mix_prompt(x, u, q, k, v, gl, sinks, wpool_bf, pscale, wbp_bf, wba_bf, wout_bf, batch, tq):
    n = x.shape[0]
    n_tiles = n // tq
    nt = n_tiles // batch
    row = lambda c: pl.BlockSpec((tq, c), lambda s: (jnp.minimum(s, n_tiles - 1), 0))
    row_prev = pl.BlockSpec((tq, D_MODEL), lambda s: (jnp.maximum(s - 1, 0), 0))
    last_t = pl.BlockSpec((1, KV_WIDTH, WINDOW),
                          lambda s: (jnp.minimum(s, n_tiles - 1) // nt, 0, 0))
    return pl.pallas_call(
        functools.partial(_mix_prompt_kernel, tq=tq, nt=nt),
        grid=(n_tiles + 1,),
        in_specs=[pl.BlockSpec(memory_space=pltpu.SMEM),
                  row(POOL_WIDTH), row(ATTN_WIDTH), row(KV_WIDTH), row(KV_WIDTH), row(2 * D_MODEL),
                  _const_spec(wpool_bf.shape), _const_spec(pscale.shape),
                  _const_spec(wbp_bf.shape), _const_spec(wba_bf.shape), _const_spec(wout_bf.shape),
                  row_prev],
        out_specs=[row_prev, last_t, last_t],
        out_shape=[jax.ShapeDtypeStruct((n, D_MODEL), F32),
                   jax.ShapeDtypeStruct((batch, KV_WIDTH, WINDOW), F32),
                   jax.ShapeDtypeStruct((batch, KV_WIDTH, WINDOW), F32)],
        scratch_shapes=[pltpu.VMEM((POOL_STATE + 1 + tq, POOL_WIDTH), F32),
                        pltpu.VMEM((N_KV_HEADS, WINDOW + tq, LANES), BF16),
                        pltpu.VMEM((N_KV_HEADS, WINDOW + tq, LANES), BF16),
                        pltpu.VMEM((tq, ATTN_WIDTH), F32),
                        pltpu.VMEM((2, GQA_GROUP * WINDOW, 2 * WINDOW), F32),
                        pltpu.VMEM((tq, D_MODEL), F32),
                        pltpu.VMEM((tq, D_MODEL), BF16)],
        compiler_params=_params(1),
        name="mix_prompt",
    )(sinks, u, q, k, v, gl, wpool_bf, pscale, wbp_bf, wba_bf, wout_bf, x)


def _tile_shift(cur, prev, j, row):
    return jnp.where(row < j, pltpu.roll(prev, j, axis=1), pltpu.roll(cur, j, axis=1))


SEQ_BLOCK = WINDOW // SUBLANES


def _sample_bias4():
    shape = (GQA_GROUP * WINDOW, 2 * WINDOW)
    i = lax.broadcasted_iota(jnp.int32, shape, 0) & (WINDOW - 1)
    j = lax.broadcasted_iota(jnp.int32, shape, 1)
    b, t = i >> 3, i & (SUBLANES - 1)
    new = j - WINDOW
    valid = ((j < WINDOW) & (j >= t)) | ((new >> 3 == b) & ((new & (SUBLANES - 1)) <= t) & (new >= 0))
    return jnp.where(valid, 0.0, -jnp.inf).astype(F32)


def _attend_cached(q, knew, vnew, ckt_ref, cvt_ref, b0, bias4, sinks_ref):
    r, t = WINDOW, SUBLANES
    nt_dims = (((1,), (1,)), ((), ()))
    kn, vn = _dup_kv_heads(knew), _dup_kv_heads(vnew)

    def per_seq(x):
        return [jnp.concatenate([x[j * r + b * t:j * r + (b + 1) * t] for j in range(GQA_GROUP)],
                                axis=0) for b in range(SEQ_BLOCK)]

    def stacked(xs):
        return jnp.concatenate([xs[b][j * t:(j + 1) * t] for j in range(GQA_GROUP)
                                for b in range(SEQ_BLOCK)], axis=0)

    outs = []
    for g in range(N_KV_HEADS):
        kv_rows = slice(g * HEAD_DIM, (g + 1) * HEAD_DIM)
        qs = _stack_heads(q, g)
        s_new = lax.dot_general(qs.astype(BF16), kn[g], nt_dims, preferred_element_type=F32)
        s_cache = []
        for b, qb in enumerate(per_seq(qs)):
            kt = ckt_ref[b0 + b, kv_rows, :]
            s_cache.append(_dot(qb.astype(BF16), jnp.concatenate([kt, kt], axis=0).astype(BF16)))
        s = jnp.concatenate([stacked(s_cache), s_new], axis=1) + bias4
        p, inv = _sink_softmax(s, _sink_column(r, g, sinks_ref))
        o_cache = []
        for b, pb in enumerate(per_seq(p[:, :r])):
            vt = cvt_ref[b0 + b, kv_rows, :]
            o_cache.append(lax.dot_general(pb.astype(BF16),
                                           jnp.concatenate([vt, vt], axis=0).astype(BF16),
                                           nt_dims, preferred_element_type=F32))
        o = stacked(o_cache) + _dot(p[:, r:].astype(BF16), vn[g])
        outs.append(_unstack_heads(o * inv))
    return jnp.concatenate(outs, axis=1)


def _slide_cache(new_rows, ct_ref, out_ref, b0):
    t = SUBLANES
    new_t = new_rows.T
    is_new = lax.broadcasted_iota(jnp.int32, (KV_WIDTH, WINDOW), 1) >= WINDOW - t
    for b in range(SEQ_BLOCK):
        kept = pltpu.roll(ct_ref[b0 + b], WINDOW - t, axis=1)
        fresh = pltpu.roll(new_t, (WINDOW - t - b * t) % WINDOW, axis=1)
        out_ref[b0 + b] = jnp.where(is_new, fresh, kept)


def _mix_sample_kernel(sinks_ref, x_ref, u_ref, st_ref, q_ref, k_ref, v_ref, ckt_ref, cvt_ref, gl_ref,
                       wpool_ref, pscale_ref, wbp_ref, wba_ref, wout_ref, o_ref, kst_ref, vst_ref,
                       oscr, *, gs, pos0):
    t = SUBLANES
    rows = gs * t

    row = lax.broadcasted_iota(jnp.int32, (gs, t, POOL_GROUP), 1)
    cnt_pos = pos0 + lax.broadcasted_iota(jnp.int32, (gs, t, 1), 1) + 1
    a_parts = []
    for g, w in enumerate(POOL_WINDOWS):
        cols = slice(g * POOL_GROUP, (g + 1) * POOL_GROUP)
        tiles = [st_ref[:, 0:t, cols], st_ref[:, t:2 * t, cols], u_ref[:, :, cols]]
        xt = tiles[2]
        step = 1
        while step < w and step < t:
            tiles = [tiles[n] + _tile_shift(tiles[n], tiles[max(n - 1, 0)], step, row)
                     for n in range(3)]
            step *= 2
        acc = tiles[2] + tiles[1] if w == 2 * t else tiles[2]
        cnt = jnp.minimum(cnt_pos, w).astype(F32)
        d = (acc / cnt - xt).reshape(rows, POOL_GROUP)
        y = _dot(d.astype(BF16), wpool_ref[g])
        a_parts.append(y * pscale_ref[:, cols])
    pool_a = jnp.concatenate(a_parts, axis=1)

    bias = _sample_bias4()
    for blk in range(gs // SEQ_BLOCK):
        rs = slice(blk * WINDOW, (blk + 1) * WINDOW)
        oscr[rs, :] = _attend_cached(q_ref[rs, :], k_ref[rs, :], v_ref[rs, :], ckt_ref, cvt_ref,
                                     blk * SEQ_BLOCK, bias, sinks_ref)
        _slide_cache(k_ref[rs, :], ckt_ref, kst_ref, blk * SEQ_BLOCK)
        _slide_cache(v_ref[rs, :], cvt_ref, vst_ref, blk * SEQ_BLOCK)

    o_ref[...] = _merge(x_ref[...], pool_a, oscr[...], gl_ref[...], wbp_ref, wba_ref, wout_ref)


def _mix_sample(x, u3, stpad, q, k, v, ck, cv, gl, sinks, wpool_bf, pscale, wbp_bf, wba_bf, wout_bf,
                gs, pos0):
    nseq, t = u3.shape[0], u3.shape[1]
    rows = gs * t
    row = lambda c: pl.BlockSpec((rows, c), lambda i: (i, 0))
    seq = lambda a, b: pl.BlockSpec((gs, a, b), lambda i: (i, 0, 0))
    return pl.pallas_call(
        functools.partial(_mix_sample_kernel, gs=gs, pos0=pos0),
        grid=(nseq // gs,),
        in_specs=[pl.BlockSpec(memory_space=pltpu.SMEM),
                  row(D_MODEL), seq(t, POOL_WIDTH), seq(2 * t, POOL_WIDTH),
                  row(ATTN_WIDTH), row(KV_WIDTH), row(KV_WIDTH),
                  seq(WINDOW, KV_WIDTH), seq(WINDOW, KV_WIDTH), row(2 * D_MODEL),
                  _const_spec(wpool_bf.shape), _const_spec(pscale.shape),
                  _const_spec(wbp_bf.shape), _const_spec(wba_bf.shape), _const_spec(wout_bf.shape)],
        out_specs=[row(D_MODEL), seq(KV_WIDTH, WINDOW), seq(KV_WIDTH, WINDOW)],
        out_shape=[jax.ShapeDtypeStruct((nseq * t, D_MODEL), F32),
                   jax.ShapeDtypeStruct((nseq, KV_WIDTH, WINDOW), F32),
                   jax.ShapeDtypeStruct((nseq, KV_WIDTH, WINDOW), F32)],
        scratch_shapes=[pltpu.VMEM((rows, ATTN_WIDTH), F32)],
        compiler_params=_params(1),
        name="mix_sample",
    )(sinks, x, u3, stpad, q, k, v, ck, cv, gl, wpool_bf, pscale, wbp_bf, wba_bf, wout_bf)


FF_CHUNK = 256
FF_CHUNK_PROMPT = 512


def _ff_chunks(width):
    return [slice(s, min(s + width, D_FF)) for s in range(0, D_FF, width)]


def _ff_cols(ch, part):
    return slice(part * D_FF + ch.start, part * D_FF + ch.stop)


def _gated_act(taps, cw_ref, cb_ref, ch):
    def conv(part):
        x0, x1, x2 = taps(ch, part)
        cols = _ff_cols(ch, part)
        return (x0 * cw_ref[0:1, cols] + x1 * cw_ref[1:2, cols] + x2 * cw_ref[2:3, cols]
                + cb_ref[:, cols])
    h = conv(0)
    return ((h * (1.0 + jnp.tanh(h))) * conv(1)).astype(BF16)


def _ffn_prompt_kernel(x_ref, g_ref, wup_ref, cw_ref, cb_ref, wdn_ref, gf_ref, o_ref, cs_ref,
                       upx, act, *, tm):
    i = pl.program_id(1)
    pre = SUBLANES

    @pl.when(i == 0)
    def _():
        upx[0:pre, :] = jnp.zeros((pre, 2 * D_FF), F32)

    x = x_ref[...]
    h = _rmsnorm(x, g_ref[...]).astype(BF16)
    upx[pre:pre + tm, :] = _dot(h, wup_ref[...])

    def taps(ch, part):
        e = upx[:, _ff_cols(ch, part)]
        return pltpu.roll(e, 2, axis=0)[pre:], pltpu.roll(e, 1, axis=0)[pre:], e[pre:]

    for ch in _ff_chunks(FF_CHUNK_PROMPT):
        act[:, ch] = _gated_act(taps, cw_ref, cb_ref, ch)

    y = x + _dot(act[...], wdn_ref[...])
    o_ref[...] = _rmsnorm(y, gf_ref[...])
    cs_ref[0] = upx[pre + tm - CONV_STATE:pre + tm, :]
    upx[0:pre, :] = upx[tm:tm + pre, :]


def _ffn_prompt(x, g_ffn, wup_bf, conv_w, conv_b, wdn_bf, g_final, batch, tm):
    n = x.shape[0]
    nt = n // batch // tm
    row = pl.BlockSpec((tm, D_MODEL), lambda b, i: (b * nt + i, 0))
    return pl.pallas_call(
        functools.partial(_ffn_prompt_kernel, tm=tm),
        grid=(batch, nt),
        in_specs=[row, _const_spec((1, D_MODEL)), _const_spec(wup_bf.shape),
                  _const_spec(conv_w.shape), _const_spec(conv_b.shape), _const_spec(wdn_bf.shape),
                  _const_spec((1, D_MODEL))],
        out_specs=[row, pl.BlockSpec((1, CONV_STATE, 2 * D_FF), lambda b, i: (b, 0, 0))],
        out_shape=[jax.ShapeDtypeStruct((n, D_MODEL), F32),
                   jax.ShapeDtypeStruct((batch, CONV_STATE, 2 * D_FF), F32)],
        scratch_shapes=[pltpu.VMEM((SUBLANES + tm, 2 * D_FF), F32),
                        pltpu.VMEM((tm, D_FF), BF16)],
        compiler_params=_params(2),
        name="ffn_prompt",
    )(x, g_ffn, wup_bf, conv_w, conv_b, wdn_bf, g_final)


def _ffn_sample_kernel(x_ref, st_ref, g_ref, wup_ref, cw_ref, cb_ref, wdn_ref, gf_ref,
                       o_ref, cs_ref, up, act, *, gs):
    t = SUBLANES
    rows = gs * t
    x = x_ref[...]
    h = _rmsnorm(x, g_ref[...]).astype(BF16)
    up[...] = _dot(h, wup_ref[...])
    row = lax.broadcasted_iota(jnp.int32, (gs, t, FF_CHUNK), 1)

    def taps(ch, part):
        cols = _ff_cols(ch, part)
        x2 = up[:, cols].reshape(gs, t, FF_CHUNK)
        s0 = jnp.broadcast_to(st_ref[:, 0:1, cols], x2.shape)
        s1 = jnp.broadcast_to(st_ref[:, 1:2, cols], x2.shape)
        x1 = jnp.where(row < 1, s1, pltpu.roll(x2, 1, axis=1))
        x0 = jnp.where(row < 1, s0, jnp.where(row < 2, s1, pltpu.roll(x2, 2, axis=1)))
        flat = lambda a: a.reshape(rows, FF_CHUNK)
        return flat(x0), flat(x1), flat(x2)

    for ch in _ff_chunks(FF_CHUNK):
        act[:, ch] = _gated_act(taps, cw_ref, cb_ref, ch)

    y = x + _dot(act[...], wdn_ref[...])
    o_ref[...] = _rmsnorm(y, gf_ref[...])
    for c0 in range(0, 2 * D_FF, 4 * FF_CHUNK):
        cols = slice(c0, min(c0 + 4 * FF_CHUNK, 2 * D_FF))
        cs_ref[:, :, cols] = up[:, cols].reshape(gs, t, -1)[:, t - CONV_STATE:, :]


def _ffn_sample(x, st, g_ffn, wup_bf, conv_w, conv_b, wdn_bf, g_final, gs):
    nseq = st.shape[0]
    t = x.shape[0] // nseq
    rows = gs * t
    row = pl.BlockSpec((rows, D_MODEL), lambda i: (i, 0))
    st_spec = pl.BlockSpec((gs, CONV_STATE, 2 * D_FF), lambda i: (i, 0, 0))
    return pl.pallas_call(
        functools.partial(_ffn_sample_kernel, gs=gs),
        grid=(nseq // gs,),
        in_specs=[row, st_spec,
                  _const_spec((1, D_MODEL)), _const_spec(wup_bf.shape),
                  _const_spec(conv_w.shape), _const_spec(conv_b.shape), _const_spec(wdn_bf.shape),
                  _const_spec((1, D_MODEL))],
        out_specs=[row, st_spec],
        out_shape=[jax.ShapeDtypeStruct((nseq * t, D_MODEL), F32),
                   jax.ShapeDtypeStruct((nseq, CONV_STATE, 2 * D_FF), F32)],
        scratch_shapes=[pltpu.VMEM((rows, 2 * D_FF), F32), pltpu.VMEM((rows, D_FF), BF16)],
        compiler_params=_params(1),
        name="ffn_sample",
    )(x, st, g_ffn, wup_bf, conv_w, conv_b, wdn_bf, g_final)


TM_IN = 1024
TQ_MIX = 512
TM_FFN = 512
GS_MIX = 32
GS_FFN = 64


def kernel(x_prompt, x_sample, state_pool, cache_k, cache_v, state_conv, g_mix, w_in, w_pool_grp,
           pool_scale, sinks, w_branch_pool, w_branch_attn, w_out, g_ffn, w_up, conv_w, conv_b,
           w_down, g_final):
    depth = g_mix.shape[0]
    batch, seq, _ = x_prompt.shape
    dec_batch, dec_seq, _ = x_sample.shape
    assert dec_seq == SUBLANES and seq % TQ_MIX == 0 and seq % TM_FFN == 0
    assert depth == 1

    yp = x_prompt.reshape(batch * seq, D_MODEL)
    ys = x_sample.reshape(dec_batch * dec_seq, D_MODEL)
    outs = [[] for _ in range(8)]
    gfin = g_final.reshape(1, D_MODEL)
    for l in range(depth):
        gmix = g_mix[l].reshape(1, D_MODEL)
        gffn = g_ffn[l].reshape(1, D_MODEL)
        w_in_bf = w_in[l].astype(BF16)
        wpool_bf = w_pool_grp[l].astype(BF16)
        pscale = pool_scale[l].reshape(1, POOL_WIDTH)
        wbp_bf = w_branch_pool[l].astype(BF16)
        wba_bf = w_branch_attn[l].astype(BF16)
        wout_bf = w_out[l].astype(BF16)
        wup_bf = w_up[l].astype(BF16)
        wdn_bf = w_down[l].astype(BF16)
        half_gate = jnp.where(jnp.arange(2 * D_FF) < D_FF, 0.5, 1.0).astype(F32)
        cw = conv_w[l] * half_gate
        cb = (conv_b[l] * half_gate).reshape(1, 2 * D_FF)

        u, q, k, v, gl = _in_proj(yp, gmix, w_in_bf, TM_IN)
        x1, kt, vt = _mix_prompt(yp, u, q, k, v, gl, sinks[l], wpool_bf, pscale, wbp_bf, wba_bf,
                                 wout_bf, batch, TQ_MIX)
        yp, conv_p = _ffn_prompt(x1, gffn, wup_bf, cw, cb, wdn_bf, gfin, batch, TM_FFN)
        outs[0].append(u.reshape(batch, seq, POOL_WIDTH)[:, seq - POOL_STATE:])
        untranspose = lambda t: t.reshape(-1, N_KV_HEADS, HEAD_DIM, WINDOW).transpose(0, 3, 1, 2)
        outs[1].append(untranspose(kt))
        outs[2].append(untranspose(vt))
        outs[3].append(conv_p)

        us, qs, ks, vs, gls = _in_proj(ys, gmix, w_in_bf, TM_IN)
        u3 = us.reshape(dec_batch, dec_seq, POOL_WIDTH)
        stpad = jnp.pad(state_pool[l], ((0, 0), (1, 0), (0, 0)))
        transposed = lambda c: c.transpose(0, 2, 3, 1).reshape(dec_batch, KV_WIDTH, WINDOW)
        ck, cv = transposed(cache_k[l]), transposed(cache_v[l])
        x1s, kst, vst = _mix_sample(ys, u3, stpad, qs, ks, vs, ck, cv, gls, sinks[l], wpool_bf,
                                    pscale, wbp_bf, wba_bf, wout_bf, GS_MIX, PAST_LEN)
        ys, conv_s = _ffn_sample(x1s, state_conv[l], gffn, wup_bf, cw, cb, wdn_bf, gfin, GS_FFN)
        outs[4].append(jnp.concatenate([state_pool[l], u3], axis=1)[:, dec_seq:])
        outs[5].append(untranspose(kst))
        outs[6].append(untranspose(vst))
        outs[7].append(conv_s)
    return (yp.reshape(batch, seq, D_MODEL), ys.reshape(dec_batch, dec_seq, D_MODEL),
            *[jnp.stack(o) for o in outs])
```

```python
import functools

import jax
import jax.numpy as jnp
from jax import lax
from jax.experimental import pallas as pl
from jax.experimental.pallas import tpu as pltpu

D_MODEL = 1024
POOL_WIDTH = 512
POOL_WINDOWS = (2, 4, 8, 16)
POOL_GROUP = 128
POOL_STATE = 15
HEAD_DIM = 64
N_HEADS = 8
N_KV_HEADS = 2
GQA_GROUP = N_HEADS // N_KV_HEADS
WINDOW = 128
ATTN_WIDTH = N_HEADS * HEAD_DIM
KV_WIDTH = N_KV_HEADS * HEAD_DIM
ATTN_SCALE = HEAD_DIM ** -0.5
LOG2_E = 1.4426950408889634
LOGIT_SCALE = ATTN_SCALE * LOG2_E
D_FF = 2816
CONV_STATE = 2
EPS = 1e-6
PAST_LEN = 16384

LANES = 128
SUBLANES = 8
VMEM_LIMIT = 56 * 1024 * 1024

F32 = jnp.float32
BF16 = jnp.bfloat16


def _rmsnorm(x, g):
    inv = lax.rsqrt(jnp.mean(x * x, axis=-1, keepdims=True) + EPS)
    return (x * inv) * g


def _dot(a, b):
    return jnp.dot(a, b, preferred_element_type=F32)


def _const_spec(shape):
    nd = len(shape)
    return pl.BlockSpec(shape, lambda *_: (0,) * nd, pipeline_mode=pl.Buffered(1))


def _params(n_axes):
    return pltpu.CompilerParams(dimension_semantics=("arbitrary",) * n_axes,
                                vmem_limit_bytes=VMEM_LIMIT)


def _in_proj_kernel(x_ref, g_ref, w_ref, u_ref, q_ref, k_ref, v_ref, gl_ref):
    h = _rmsnorm(x_ref[...], g_ref[...]).astype(BF16)
    o_q = POOL_WIDTH
    o_k = o_q + ATTN_WIDTH
    o_gl = o_k + 2 * KV_WIDTH
    u_ref[...] = _dot(h, w_ref[:, 0:o_q])
    q_ref[...] = _dot(h, w_ref[:, o_q:o_k])
    kv = _dot(h, w_ref[:, o_k:o_gl])
    k_ref[...] = kv[:, :KV_WIDTH]
    v_ref[...] = kv[:, KV_WIDTH:]
    gl_ref[...] = _dot(h, w_ref[:, o_gl:])


def _in_proj(x, g_mix, w_in_bf, tm):
    n = x.shape[0]
    in_width = w_in_bf.shape[1]
    row = lambda c: pl.BlockSpec((tm, c), lambda i: (i, 0))
    return pl.pallas_call(
        _in_proj_kernel,
        grid=(n // tm,),
        in_specs=[row(D_MODEL), _const_spec((1, D_MODEL)), _const_spec((D_MODEL, in_width))],
        out_specs=[row(POOL_WIDTH), row(ATTN_WIDTH), row(KV_WIDTH), row(KV_WIDTH), row(2 * D_MODEL)],
        out_shape=[jax.ShapeDtypeStruct((n, POOL_WIDTH), F32),
                   jax.ShapeDtypeStruct((n, ATTN_WIDTH), F32),
                   jax.ShapeDtypeStruct((n, KV_WIDTH), F32),
                   jax.ShapeDtypeStruct((n, KV_WIDTH), F32),
                   jax.ShapeDtypeStruct((n, 2 * D_MODEL), F32)],
        compiler_params=_params(1),
        name="in_proj",
    )(x, g_mix, w_in_bf)


def _dup_kv_heads(x):
    lo = lax.broadcasted_iota(jnp.int32, x.shape, 1) < HEAD_DIM
    r = pltpu.roll(x, HEAD_DIM, axis=1)
    return [jnp.where(lo, x, r).astype(BF16), jnp.where(lo, r, x).astype(BF16)]


def _stack_heads(q, g):
    lo = lax.broadcasted_iota(jnp.int32, (q.shape[0], LANES), 1) < HEAD_DIM
    scale_lo = jnp.where(lo, LOGIT_SCALE, 0.0).astype(F32)
    scale_hi = jnp.where(lo, 0.0, LOGIT_SCALE).astype(F32)
    base = g * GQA_GROUP * HEAD_DIM
    p0 = q[:, base:base + LANES]
    p1 = q[:, base + LANES:base + 2 * LANES]
    return jnp.concatenate([p0 * scale_lo, p0 * scale_hi, p1 * scale_lo, p1 * scale_hi], axis=0)


def _attn_scores(q, kdup, g, bias4):
    return lax.dot_general(_stack_heads(q, g).astype(BF16), kdup, (((1,), (1,)), ((), ())),
                           preferred_element_type=F32) + bias4


def _sink_column(r, g, sinks_ref):
    row4 = lax.broadcasted_iota(jnp.int32, (GQA_GROUP * r, 1), 0)
    sk = jnp.full((GQA_GROUP * r, 1), sinks_ref[g * GQA_GROUP + GQA_GROUP - 1], F32)
    for hh in range(GQA_GROUP - 2, -1, -1):
        sk = jnp.where(row4 < (hh + 1) * r, sinks_ref[g * GQA_GROUP + hh], sk)
    return sk * LOG2_E


def _sink_softmax(s, sk):
    m = jnp.maximum(jnp.max(s, axis=-1, keepdims=True), sk)
    e = jnp.exp2(s - m)
    den = jnp.sum(e, axis=-1, keepdims=True) + jnp.exp2(sk - m)
    return e, 1.0 / den


def _unstack_heads(o):
    r = o.shape[0] // GQA_GROUP
    lo = lax.broadcasted_iota(jnp.int32, (r, LANES), 1) < HEAD_DIM
    return jnp.concatenate([jnp.where(lo, o[0:r], o[r:2 * r]),
                            jnp.where(lo, o[2 * r:3 * r], o[3 * r:4 * r])], axis=1)


def _software_pipeline(n_units, stages, after_tick=()):
    vals = {}
    for tick in range(n_units + len(stages) - 1):
        for k in reversed(range(len(stages))):
            u = tick - k
            if 0 <= u < n_units:
                vals[u] = stages[k](u, vals.get(u))
        if tick < len(after_tick) and after_tick[tick] is not None:
            after_tick[tick]()


def _band_bias4(r, s, first_key=0):
    i = lax.broadcasted_iota(jnp.int32, (GQA_GROUP * r, s), 0) & (r - 1)
    j = lax.broadcasted_iota(jnp.int32, (GQA_GROUP * r, s), 1)
    valid = (j >= jnp.maximum(i, first_key)) & (j <= i + WINDOW)
    return jnp.where(valid, 0.0, -jnp.inf).astype(F32)


def _window_sums(e, w):
    step = 1
    while step < w:
        e = e + pltpu.roll(e, step, axis=0)
        step *= 2
    return e


def _sigmoid(x):
    return 0.5 * jnp.tanh(0.5 * x) + 0.5


def _gate_merge(pa, po, gl):
    ga = _sigmoid(gl[:, :D_MODEL])
    gb = _sigmoid(gl[:, D_MODEL:])
    return (ga * pa + gb * po).astype(BF16)


def _merge(x, pool_a, attn_o, gl, wbp_ref, wba_ref, wout_ref):
    pa = _dot(pool_a.astype(BF16), wbp_ref[...])
    po = _dot(attn_o.astype(BF16), wba_ref[...])
    return x + _dot(_gate_merge(pa, po, gl), wout_ref[...])


def _mix_prompt_kernel(sinks_ref, u_ref, q_ref, k_ref, v_ref, gl_ref, wpool_ref, pscale_ref,
                       wbp_ref, wba_ref, wout_ref, xo_ref, o_ref, kt_ref, vt_ref, uext, kdup, vdup,
                       oscr, bias, pa_scr, mixed, *, tq, nt):
    s = pl.program_id(0)
    i = s % nt
    pre = POOL_STATE + 1

    @pl.when(s == 0)
    def _():
        mixed[...] = jnp.zeros(mixed.shape, BF16)

    @pl.when(i == 0)
    def _():
        uext[0:pre, :] = jnp.zeros((pre, POOL_WIDTH), F32)
        kdup[:, 0:WINDOW, :] = jnp.zeros((N_KV_HEADS, WINDOW, LANES), BF16)
        vdup[:, 0:WINDOW, :] = jnp.zeros((N_KV_HEADS, WINDOW, LANES), BF16)
        bias[0] = _band_bias4(WINDOW, 2 * WINDOW, WINDOW)
        bias[1] = _band_bias4(WINDOW, 2 * WINDOW)

    uext[pre:pre + tq, :] = u_ref[...]
    for dst, src in ((kdup, k_ref), (vdup, v_ref)):
        for g, d in enumerate(_dup_kv_heads(src[...])):
            dst[g, WINDOW:WINDOW + tq, :] = d
    sink_cols = [_sink_column(WINDOW, g, sinks_ref) for g in range(N_KV_HEADS)]

    piece = D_MODEL // 4

    def out_prev(n):
        cols = slice(n * piece, (n + 1) * piece)
        o_ref[:, cols] = xo_ref[:, cols] + _dot(mixed[...], wout_ref[:, cols])

    pos1 = i * tq + lax.broadcasted_iota(jnp.int32, (tq, 1), 0) + 1
    a_parts = []

    def pool_group(g):
        w = POOL_WINDOWS[g]
        cols = slice(g * POOL_GROUP, (g + 1) * POOL_GROUP)
        e = uext[:, cols]
        inv_cnt = 1.0 / jnp.minimum(pos1, w).astype(F32)
        d = _window_sums(e, w)[pre:] * inv_cnt - e[pre:]
        y = _dot(d.astype(BF16), wpool_ref[g])
        a_parts.append((y * pscale_ref[:, cols]).astype(BF16))

    n_units = N_KV_HEADS * tq // WINDOW

    def pooled_branch(n):
        cols = slice(n * 2 * piece, (n + 1) * 2 * piece)
        pa_scr[:, cols] = _dot(jnp.concatenate(a_parts, axis=1), wbp_ref[:, cols])

    def early(n):
        out_prev(n)
        pool_group(n)

    def merge_half(half):
        rows = slice(half * tq // 2, (half + 1) * tq // 2)
        po = _dot(oscr[rows, :].astype(BF16), wba_ref[...])
        mixed[rows, :] = _gate_merge(pa_scr[rows, :], po, gl_ref[rows, :])

    half_done = n_units // 2 + 1
    after_tick = [functools.partial(early, n) for n in range(4)]
    after_tick += [functools.partial(pooled_branch, n) for n in range(2)]
    assert len(after_tick) <= half_done + 1
    after_tick += [None] * (half_done + 1 - len(after_tick)) + [functools.partial(merge_half, 0)]

    def unit(u):
        nb, g = divmod(u, N_KV_HEADS)
        return nb, g, slice(nb * WINDOW, (nb + 1) * WINDOW), slice(nb * WINDOW, (nb + 2) * WINDOW)

    def scores(u, _):
        nb, g, rows, keys = unit(u)
        slot = jnp.minimum(i * (tq // WINDOW) + nb, 1)
        return _attn_scores(q_ref[rows, :], kdup[g, keys, :], g, bias[slot])

    def softmax(u, s):
        e, inv = _sink_softmax(s, sink_cols[unit(u)[1]])
        return e.astype(BF16), inv

    def values(u, weights):
        nb, g, rows, keys = unit(u)
        e, inv = weights
        oscr[rows, g * 2 * LANES:(g + 1) * 2 * LANES] = _unstack_heads(
            _dot(e, vdup[g, keys, :]) * inv)

    _software_pipeline(n_units, [scores, softmax, values], after_tick)
    merge_half(1)

    @pl.when(i == nt - 1)
    def _():
        kt_ref[0] = k_ref[tq - WINDOW:tq, :].T
        vt_ref[0] = v_ref[tq - WINDOW:tq, :].T

    uext[0:pre, :] = uext[tq:tq + pre, :]
    kdup[:, 0:WINDOW, :] = kdup[:, tq:tq + WINDOW, :]
    vdup[:, 0:WINDOW, :] = vdup[:, tq:tq + WINDOW, :]


def _mix_prompt(x, u, q, k, v, gl, sinks, wpool_bf, pscale, wbp_bf, wba_bf, wout_bf, batch, tq):
    n = x.shape[0]
    n_tiles = n // tq
    nt = n_tiles // batch
    row = lambda c: pl.BlockSpec((tq, c), lambda s: (jnp.minimum(s, n_tiles - 1), 0))
    row_prev = pl.BlockSpec((tq, D_MODEL), lambda s: (jnp.maximum(s - 1, 0), 0))
    last_t = pl.BlockSpec((1, KV_WIDTH, WINDOW),
                          lambda s: (jnp.minimum(s, n_tiles - 1) // nt, 0, 0))
    return pl.pallas_call(
        functools.partial(_mix_prompt_kernel, tq=tq, nt=nt),
        grid=(n_tiles + 1,),
        in_specs=[pl.BlockSpec(memory_space=pltpu.SMEM),
                  row(POOL_WIDTH), row(ATTN_WIDTH), row(KV_WIDTH), row(KV_WIDTH), row(2 * D_MODEL),
                  _const_spec(wpool_bf.shape), _const_spec(pscale.shape),
                  _const_spec(wbp_bf.shape), _const_spec(wba_bf.shape), _const_spec(wout_bf.shape),
                  row_prev],
        out_specs=[row_prev, last_t, last_t],
        out_shape=[jax.ShapeDtypeStruct((n, D_MODEL), F32),
                   jax.ShapeDtypeStruct((batch, KV_WIDTH, WINDOW), F32),
                   jax.ShapeDtypeStruct((batch, KV_WIDTH, WINDOW), F32)],
        scratch_shapes=[pltpu.VMEM((POOL_STATE + 1 + tq, POOL_WIDTH), F32),
                        pltpu.VMEM((N_KV_HEADS, WINDOW + tq, LANES), BF16),
                        pltpu.VMEM((N_KV_HEADS, WINDOW + tq, LANES), BF16),
                        pltpu.VMEM((tq, ATTN_WIDTH), F32),
                        pltpu.VMEM((2, GQA_GROUP * WINDOW, 2 * WINDOW), F32),
                        pltpu.VMEM((tq, D_MODEL), F32),
                        pltpu.VMEM((tq, D_MODEL), BF16)],
        compiler_params=_params(1),
        name="mix_prompt",
    )(sinks, u, q, k, v, gl, wpool_bf, pscale, wbp_bf, wba_bf, wout_bf, x)


def _tile_shift(cur, prev, j, row):
    return jnp.where(row < j, pltpu.roll(prev, j, axis=1), pltpu.roll(cur, j, axis=1))


SEQ_BLOCK = WINDOW // SUBLANES


def _sample_bias4():
    shape = (GQA_GROUP * WINDOW, 2 * WINDOW)
    i = lax.broadcasted_iota(jnp.int32, shape, 0) & (WINDOW - 1)
    j = lax.broadcasted_iota(jnp.int32, shape, 1)
    b, t = i >> 3, i & (SUBLANES - 1)
    new = j - WINDOW
    valid = ((j < WINDOW) & (j >= t)) | ((new >> 3 == b) & ((new & (SUBLANES - 1)) <= t) & (new >= 0))
    return jnp.where(valid, 0.0, -jnp.inf).astype(F32)


def _attend_cached(q, knew, vnew, ckt_ref, cvt_ref, b0, bias4, sinks_ref):
    r, t = WINDOW, SUBLANES
    nt_dims = (((1,), (1,)), ((), ()))
    kn, vn = _dup_kv_heads(knew), _dup_kv_heads(vnew)

    def per_seq(x):
        return [jnp.concatenate([x[j * r + b * t:j * r + (b + 1) * t] for j in range(GQA_GROUP)],
                                axis=0) for b in range(SEQ_BLOCK)]

    def stacked(xs):
        return jnp.concatenate([xs[b][j * t:(j + 1) * t] for j in range(GQA_GROUP)
                                for b in range(SEQ_BLOCK)], axis=0)

    outs = []
    for g in range(N_KV_HEADS):
        kv_rows = slice(g * HEAD_DIM, (g + 1) * HEAD_DIM)
        qs = _stack_heads(q, g)
        s_new = lax.dot_general(qs.astype(BF16), kn[g], nt_dims, preferred_element_type=F32)
        s_cache = []
        for b, qb in enumerate(per_seq(qs)):
            kt = ckt_ref[b0 + b, kv_rows, :]
            s_cache.append(_dot(qb.astype(BF16), jnp.concatenate([kt, kt], axis=0).astype(BF16)))
        s = jnp.concatenate([stacked(s_cache), s_new], axis=1) + bias4
        p, inv = _sink_softmax(s, _sink_column(r, g, sinks_ref))
        o_cache = []
        for b, pb in enumerate(per_seq(p[:, :r])):
            vt = cvt_ref[b0 + b, kv_rows, :]
            o_cache.append(lax.dot_general(pb.astype(BF16),
                                           jnp.concatenate([vt, vt], axis=0).astype(BF16),
                                           nt_dims, preferred_element_type=F32))
        o = stacked(o_cache) + _dot(p[:, r:].astype(BF16), vn[g])
        outs.append(_unstack_heads(o * inv))
    return jnp.concatenate(outs, axis=1)


def _slide_cache(new_rows, ct_ref, out_ref, b0):
    t = SUBLANES
    new_t = new_rows.T
    is_new = lax.broadcasted_iota(jnp.int32, (KV_WIDTH, WINDOW), 1) >= WINDOW - t
    for b in range(SEQ_BLOCK):
        kept = pltpu.roll(ct_ref[b0 + b], WINDOW - t, axis=1)
        fresh = pltpu.roll(new_t, (WINDOW - t - b * t) % WINDOW, axis=1)
        out_ref[b0 + b] = jnp.where(is_new, fresh, kept)


def _mix_sample_kernel(sinks_ref, x_ref, u_ref, st_ref, q_ref, k_ref, v_ref, ckt_ref, cvt_ref, gl_ref,
                       wpool_ref, pscale_ref, wbp_ref, wba_ref, wout_ref, o_ref, kst_ref, vst_ref,
                       oscr, *, gs, pos0):
    t = SUBLANES
    rows = gs * t

    row = lax.broadcasted_iota(jnp.int32, (gs, t, POOL_GROUP), 1)
    cnt_pos = pos0 + lax.broadcasted_iota(jnp.int32, (gs, t, 1), 1) + 1
    a_parts = []
    for g, w in enumerate(POOL_WINDOWS):
        cols = slice(g * POOL_GROUP, (g + 1) * POOL_GROUP)
        tiles = [st_ref[:, 0:t, cols], st_ref[:, t:2 * t, cols], u_ref[:, :, cols]]
        xt = tiles[2]
        step = 1
        while step < w and step < t:
            tiles = [tiles[n] + _tile_shift(tiles[n], tiles[max(n - 1, 0)], step, row)
                     for n in range(3)]
            step *= 2
        acc = tiles[2] + tiles[1] if w == 2 * t else tiles[2]
        cnt = jnp.minimum(cnt_pos, w).astype(F32)
        d = (acc / cnt - xt).reshape(rows, POOL_GROUP)
        y = _dot(d.astype(BF16), wpool_ref[g])
        a_parts.append(y * pscale_ref[:, cols])
    pool_a = jnp.concatenate(a_parts, axis=1)

    bias = _sample_bias4()
    for blk in range(gs // SEQ_BLOCK):
        rs = slice(blk * WINDOW, (blk + 1) * WINDOW)
        oscr[rs, :] = _attend_cached(q_ref[rs, :], k_ref[rs, :], v_ref[rs, :], ckt_ref, cvt_ref,
                                     blk * SEQ_BLOCK, bias, sinks_ref)
        _slide_cache(k_ref[rs, :], ckt_ref, kst_ref, blk * SEQ_BLOCK)
        _slide_cache(v_ref[rs, :], cvt_ref, vst_ref, blk * SEQ_BLOCK)

    o_ref[...] = _merge(x_ref[...], pool_a, oscr[...], gl_ref[...], wbp_ref, wba_ref, wout_ref)


def _mix_sample(x, u3, stpad, q, k, v, ck, cv, gl, sinks, wpool_bf, pscale, wbp_bf, wba_bf, wout_bf,
                gs, pos0):
    nseq, t = u3.shape[0], u3.shape[1]
    rows = gs * t
    row = lambda c: pl.BlockSpec((rows, c), lambda i: (i, 0))
    seq = lambda a, b: pl.BlockSpec((gs, a, b), lambda i: (i, 0, 0))
    return pl.pallas_call(
        functools.partial(_mix_sample_kernel, gs=gs, pos0=pos0),
        grid=(nseq // gs,),
        in_specs=[pl.BlockSpec(memory_space=pltpu.SMEM),
                  row(D_MODEL), seq(t, POOL_WIDTH), seq(2 * t, POOL_WIDTH),
                  row(ATTN_WIDTH), row(KV_WIDTH), row(KV_WIDTH),
                  seq(WINDOW, KV_WIDTH), seq(WINDOW, KV_WIDTH), row(2 * D_MODEL),
                  _const_spec(wpool_bf.shape), _const_spec(pscale.shape),
                  _const_spec(wbp_bf.shape), _const_spec(wba_bf.shape), _const_spec(wout_bf.shape)],
        out_specs=[row(D_MODEL), seq(KV_WIDTH, WINDOW), seq(KV_WIDTH, WINDOW)],
        out_shape=[jax.ShapeDtypeStruct((nseq * t, D_MODEL), F32),
                   jax.ShapeDtypeStruct((nseq, KV_WIDTH, WINDOW), F32),
                   jax.ShapeDtypeStruct((nseq, KV_WIDTH, WINDOW), F32)],
        scratch_shapes=[pltpu.VMEM((rows, ATTN_WIDTH), F32)],
        compiler_params=_params(1),
        name="mix_sample",
    )(sinks, x, u3, stpad, q, k, v, ck, cv, gl, wpool_bf, pscale, wbp_bf, wba_bf, wout_bf)


FF_CHUNK = 256


def _ff_cols(c, part):
    start = part * D_FF + c * FF_CHUNK
    return slice(start, start + FF_CHUNK)


def _gated_act(taps, cw_ref, cb_ref, c):
    def conv(part):
        x0, x1, x2 = taps(c, part)
        cols = _ff_cols(c, part)
        return (x0 * cw_ref[0:1, cols] + x1 * cw_ref[1:2, cols] + x2 * cw_ref[2:3, cols]
                + cb_ref[:, cols])
    h = conv(0)
    return ((h * (1.0 + jnp.tanh(h))) * conv(1)).astype(BF16)


def _ffn_prompt_kernel(x_ref, g_ref, wup_ref, cw_ref, cb_ref, wdn_ref, gf_ref, o_ref, cs_ref,
                       upx, act, *, tm):
    i = pl.program_id(1)
    pre = SUBLANES

    @pl.when(i == 0)
    def _():
        upx[0:pre, :] = jnp.zeros((pre, 2 * D_FF), F32)

    x = x_ref[...]
    h = _rmsnorm(x, g_ref[...]).astype(BF16)
    upx[pre:pre + tm, :] = _dot(h, wup_ref[...])

    def taps(c, part):
        e = upx[:, _ff_cols(c, part)]
        return pltpu.roll(e, 2, axis=0)[pre:], pltpu.roll(e, 1, axis=0)[pre:], e[pre:]

    for c in range(D_FF // FF_CHUNK):
        act[:, c * FF_CHUNK:(c + 1) * FF_CHUNK] = _gated_act(taps, cw_ref, cb_ref, c)

    y = x + _dot(act[...], wdn_ref[...])
    o_ref[...] = _rmsnorm(y, gf_ref[...])
    cs_ref[0] = upx[pre + tm - CONV_STATE:pre + tm, :]
    upx[0:pre, :] = upx[tm:tm + pre, :]


def _ffn_prompt(x, g_ffn, wup_bf, conv_w, conv_b, wdn_bf, g_final, batch, tm):
    n = x.shape[0]
    nt = n // batch // tm
    row = pl.BlockSpec((tm, D_MODEL), lambda b, i: (b * nt + i, 0))
    return pl.pallas_call(
        functools.partial(_ffn_prompt_kernel, tm=tm),
        grid=(batch, nt),
        in_specs=[row, _const_spec((1, D_MODEL)), _const_spec(wup_bf.shape),
                  _const_spec(conv_w.shape), _const_spec(conv_b.shape), _const_spec(wdn_bf.shape),
                  _const_spec((1, D_MODEL))],
        out_specs=[row, pl.BlockSpec((1, CONV_STATE, 2 * D_FF), lambda b, i: (b, 0, 0))],
        out_shape=[jax.ShapeDtypeStruct((n, D_MODEL), F32),
                   jax.ShapeDtypeStruct((batch, CONV_STATE, 2 * D_FF), F32)],
        scratch_shapes=[pltpu.VMEM((SUBLANES + tm, 2 * D_FF), F32),
                        pltpu.VMEM((tm, D_FF), BF16)],
        compiler_params=_params(2),
        name="ffn_prompt",
    )(x, g_ffn, wup_bf, conv_w, conv_b, wdn_bf, g_final)


def _ffn_sample_kernel(x_ref, st_ref, g_ref, wup_ref, cw_ref, cb_ref, wdn_ref, gf_ref,
                       o_ref, cs_ref, up, act, *, gs):
    t = SUBLANES
    rows = gs * t
    x = x_ref[...]
    h = _rmsnorm(x, g_ref[...]).astype(BF16)
    up[...] = _dot(h, wup_ref[...])
    row = lax.broadcasted_iota(jnp.int32, (gs, t, FF_CHUNK), 1)

    def taps(c, part):
        cols = _ff_cols(c, part)
        x2 = up[:, cols].reshape(gs, t, FF_CHUNK)
        s0 = jnp.broadcast_to(st_ref[:, 0:1, cols], x2.shape)
        s1 = jnp.broadcast_to(st_ref[:, 1:2, cols], x2.shape)
        x1 = jnp.where(row < 1, s1, pltpu.roll(x2, 1, axis=1))
        x0 = jnp.where(row < 1, s0, jnp.where(row < 2, s1, pltpu.roll(x2, 2, axis=1)))
        flat = lambda a: a.reshape(rows, FF_CHUNK)
        return flat(x0), flat(x1), flat(x2)

    for c in range(D_FF // FF_CHUNK):
        act[:, c * FF_CHUNK:(c + 1) * FF_CHUNK] = _gated_act(taps, cw_ref, cb_ref, c)

    y = x + _dot(act[...], wdn_ref[...])
    o_ref[...] = _rmsnorm(y, gf_ref[...])
    for c0 in range(0, 2 * D_FF, 4 * FF_CHUNK):
        cols = slice(c0, min(c0 + 4 * FF_CHUNK, 2 * D_FF))
        cs_ref[:, :, cols] = up[:, cols].reshape(gs, t, -1)[:, t - CONV_STATE:, :]


def _ffn_sample(x, st, g_ffn, wup_bf, conv_w, conv_b, wdn_bf, g_final, gs):
    nseq = st.shape[0]
    t = x.shape[0] // nseq
    rows = gs * t
    row = pl.BlockSpec((rows, D_MODEL), lambda i: (i, 0))
    st_spec = pl.BlockSpec((gs, CONV_STATE, 2 * D_FF), lambda i: (i, 0, 0))
    return pl.pallas_call(
        functools.partial(_ffn_sample_kernel, gs=gs),
        grid=(nseq // gs,),
        in_specs=[row, st_spec,
                  _const_spec((1, D_MODEL)), _const_spec(wup_bf.shape),
                  _const_spec(conv_w.shape), _const_spec(conv_b.shape), _const_spec(wdn_bf.shape),
                  _const_spec((1, D_MODEL))],
        out_specs=[row, st_spec],
        out_shape=[jax.ShapeDtypeStruct((nseq * t, D_MODEL), F32),
                   jax.ShapeDtypeStruct((nseq, CONV_STATE, 2 * D_FF), F32)],
        scratch_shapes=[pltpu.VMEM((rows, 2 * D_FF), F32), pltpu.VMEM((rows, D_FF), BF16)],
        compiler_params=_params(1),
        name="ffn_sample",
    )(x, st, g_ffn, wup_bf, conv_w, conv_b, wdn_bf, g_final)


TM_IN = 1024
TM_IN_SAMPLE = 256
TQ_MIX = 512
TM_FFN = 512
GS_MIX = 32
GS_FFN = 64


def kernel(x_prompt, x_sample, state_pool, cache_k, cache_v, state_conv, g_mix, w_in, w_pool_grp,
           pool_scale, sinks, w_branch_pool, w_branch_attn, w_out, g_ffn, w_up, conv_w, conv_b,
           w_down, g_final):
    depth = g_mix.shape[0]
    batch, seq, _ = x_prompt.shape
    dec_batch, dec_seq, _ = x_sample.shape
    assert dec_seq == SUBLANES and seq % TQ_MIX == 0 and seq % TM_FFN == 0
    assert depth == 1

    yp = x_prompt.reshape(batch * seq, D_MODEL)
    ys = x_sample.reshape(dec_batch * dec_seq, D_MODEL)
    outs = [[] for _ in range(8)]
    gfin = g_final.reshape(1, D_MODEL)
    for l in range(depth):
        gmix = g_mix[l].reshape(1, D_MODEL)
        gffn = g_ffn[l].reshape(1, D_MODEL)
        w_in_bf = w_in[l].astype(BF16)
        wpool_bf = w_pool_grp[l].astype(BF16)
        pscale = pool_scale[l].reshape(1, POOL_WIDTH)
        wbp_bf = w_branch_pool[l].astype(BF16)
        wba_bf = w_branch_attn[l].astype(BF16)
        wout_bf = w_out[l].astype(BF16)
        wup_bf = w_up[l].astype(BF16)
        wdn_bf = w_down[l].astype(BF16)
        half_gate = jnp.where(jnp.arange(2 * D_FF) < D_FF, 0.5, 1.0).astype(F32)
        cw = conv_w[l] * half_gate
        cb = (conv_b[l] * half_gate).reshape(1, 2 * D_FF)

        u, q, k, v, gl = _in_proj(yp, gmix, w_in_bf, TM_IN)
        x1, kt, vt = _mix_prompt(yp, u, q, k, v, gl, sinks[l], wpool_bf, pscale, wbp_bf, wba_bf,
                                 wout_bf, batch, TQ_MIX)
        yp, conv_p = _ffn_prompt(x1, gffn, wup_bf, cw, cb, wdn_bf, gfin, batch, TM_FFN)
        outs[0].append(u.reshape(batch, seq, POOL_WIDTH)[:, seq - POOL_STATE:])
        untranspose = lambda t: t.reshape(-1, N_KV_HEADS, HEAD_DIM, WINDOW).transpose(0, 3, 1, 2)
        outs[1].append(untranspose(kt))
        outs[2].append(untranspose(vt))
        outs[3].append(conv_p)

        us, qs, ks, vs, gls = _in_proj(ys, gmix, w_in_bf, TM_IN_SAMPLE)
        u3 = us.reshape(dec_batch, dec_seq, POOL_WIDTH)
        stpad = jnp.pad(state_pool[l], ((0, 0), (1, 0), (0, 0)))
        transposed = lambda c: c.transpose(0, 2, 3, 1).reshape(dec_batch, KV_WIDTH, WINDOW)
        ck, cv = transposed(cache_k[l]), transposed(cache_v[l])
        x1s, kst, vst = _mix_sample(ys, u3, stpad, qs, ks, vs, ck, cv, gls, sinks[l], wpool_bf,
                                    pscale, wbp_bf, wba_bf, wout_bf, GS_MIX, PAST_LEN)
        ys, conv_s = _ffn_sample(x1s, state_conv[l], gffn, wup_bf, cw, cb, wdn_bf, gfin, GS_FFN)
        outs[4].append(jnp.concatenate([state_pool[l], u3], axis=1)[:, dec_seq:])
        outs[5].append(untranspose(kst))
        outs[6].append(untranspose(vst))
        outs[7].append(conv_s)
    return (yp.reshape(batch, seq, D_MODEL), ys.reshape(dec_batch, dec_seq, D_MODEL),
            *[jnp.stack(o) for o in outs])
```

```python
import functools

import jax
import jax.numpy as jnp
from jax import lax
from jax.experimental import pallas as pl
from jax.experimental.pallas import tpu as pltpu

D_MODEL = 1024
POOL_WIDTH = 512
POOL_WINDOWS = (2, 4, 8, 16)
POOL_GROUP = 128
POOL_STATE = 15
HEAD_DIM = 64
N_HEADS = 8
N_KV_HEADS = 2
GQA_GROUP = N_HEADS // N_KV_HEADS
WINDOW = 128
ATTN_WIDTH = N_HEADS * HEAD_DIM
KV_WIDTH = N_KV_HEADS * HEAD_DIM
ATTN_SCALE = HEAD_DIM ** -0.5
LOG2_E = 1.4426950408889634
LOGIT_SCALE = ATTN_SCALE * LOG2_E
D_FF = 2816
CONV_STATE = 2
EPS = 1e-6
PAST_LEN = 16384

LANES = 128
SUBLANES = 8
VMEM_LIMIT = 56 * 1024 * 1024

F32 = jnp.float32
BF16 = jnp.bfloat16


def _rmsnorm(x, g):
    inv = lax.rsqrt(jnp.mean(x * x, axis=-1, keepdims=True) + EPS)
    return (x * inv) * g


def _dot(a, b):
    return jnp.dot(a, b, preferred_element_type=F32)


def _const_spec(shape):
    nd = len(shape)
    return pl.BlockSpec(shape, lambda *_: (0,) * nd, pipeline_mode=pl.Buffered(1))


def _params(n_axes):
    return pltpu.CompilerParams(dimension_semantics=("arbitrary",) * n_axes,
                                vmem_limit_bytes=VMEM_LIMIT)


def _in_proj_kernel(x_ref, g_ref, w_ref, u_ref, q_ref, k_ref, v_ref, gl_ref):
    h = _rmsnorm(x_ref[...], g_ref[...]).astype(BF16)
    o_q = POOL_WIDTH
    o_k = o_q + ATTN_WIDTH
    o_gl = o_k + 2 * KV_WIDTH
    u_ref[...] = _dot(h, w_ref[:, 0:o_q])
    q_ref[...] = _dot(h, w_ref[:, o_q:o_k])
    kv = _dot(h, w_ref[:, o_k:o_gl])
    k_ref[...] = kv[:, :KV_WIDTH]
    v_ref[...] = kv[:, KV_WIDTH:]
    gl_ref[...] = _dot(h, w_ref[:, o_gl:])


def _in_proj(x, g_mix, w_in_bf, tm):
    n = x.shape[0]
    in_width = w_in_bf.shape[1]
    row = lambda c: pl.BlockSpec((tm, c), lambda i: (i, 0))
    return pl.pallas_call(
        _in_proj_kernel,
        grid=(n // tm,),
        in_specs=[row(D_MODEL), _const_spec((1, D_MODEL)), _const_spec((D_MODEL, in_width))],
        out_specs=[row(POOL_WIDTH), row(ATTN_WIDTH), row(KV_WIDTH), row(KV_WIDTH), row(2 * D_MODEL)],
        out_shape=[jax.ShapeDtypeStruct((n, POOL_WIDTH), F32),
                   jax.ShapeDtypeStruct((n, ATTN_WIDTH), F32),
                   jax.ShapeDtypeStruct((n, KV_WIDTH), F32),
                   jax.ShapeDtypeStruct((n, KV_WIDTH), F32),
                   jax.ShapeDtypeStruct((n, 2 * D_MODEL), F32)],
        compiler_params=_params(1),
        name="in_proj",
    )(x, g_mix, w_in_bf)


def _dup_kv_heads(x):
    lo = lax.broadcasted_iota(jnp.int32, x.shape, 1) < HEAD_DIM
    r = pltpu.roll(x, HEAD_DIM, axis=1)
    return [jnp.where(lo, x, r).astype(BF16), jnp.where(lo, r, x).astype(BF16)]


def _stack_heads(q, g):
    lo = lax.broadcasted_iota(jnp.int32, (q.shape[0], LANES), 1) < HEAD_DIM
    scale_lo = jnp.where(lo, LOGIT_SCALE, 0.0).astype(F32)
    scale_hi = jnp.where(lo, 0.0, LOGIT_SCALE).astype(F32)
    base = g * GQA_GROUP * HEAD_DIM
    p0 = q[:, base:base + LANES]
    p1 = q[:, base + LANES:base + 2 * LANES]
    return jnp.concatenate([p0 * scale_lo, p0 * scale_hi, p1 * scale_lo, p1 * scale_hi], axis=0)


def _attn_scores(q, kdup, g, bias4):
    return lax.dot_general(_stack_heads(q, g).astype(BF16), kdup, (((1,), (1,)), ((), ())),
                           preferred_element_type=F32) + bias4


def _sink_column(r, g, sinks_ref):
    row4 = lax.broadcasted_iota(jnp.int32, (GQA_GROUP * r, 1), 0)
    sk = jnp.full((GQA_GROUP * r, 1), sinks_ref[g * GQA_GROUP + GQA_GROUP - 1], F32)
    for hh in range(GQA_GROUP - 2, -1, -1):
        sk = jnp.where(row4 < (hh + 1) * r, sinks_ref[g * GQA_GROUP + hh], sk)
    return sk * LOG2_E


def _sink_softmax(s, sk):
    m = jnp.maximum(jnp.max(s, axis=-1, keepdims=True), sk)
    e = jnp.exp2(s - m)
    den = jnp.sum(e, axis=-1, keepdims=True) + jnp.exp2(sk - m)
    return e, 1.0 / den


def _unstack_heads(o):
    r = o.shape[0] // GQA_GROUP
    lo = lax.broadcasted_iota(jnp.int32, (r, LANES), 1) < HEAD_DIM
    return jnp.concatenate([jnp.where(lo, o[0:r], o[r:2 * r]),
                            jnp.where(lo, o[2 * r:3 * r], o[3 * r:4 * r])], axis=1)


def _software_pipeline(n_units, stages, after_tick=()):
    vals = {}
    for tick in range(n_units + len(stages) - 1):
        for k in reversed(range(len(stages))):
            u = tick - k
            if 0 <= u < n_units:
                vals[u] = stages[k](u, vals.get(u))
        if tick < len(after_tick) and after_tick[tick] is not None:
            after_tick[tick]()


def _band_bias4(r, s, first_key=0):
    i = lax.broadcasted_iota(jnp.int32, (GQA_GROUP * r, s), 0) & (r - 1)
    j = lax.broadcasted_iota(jnp.int32, (GQA_GROUP * r, s), 1)
    valid = (j >= jnp.maximum(i, first_key)) & (j <= i + WINDOW)
    return jnp.where(valid, 0.0, -jnp.inf).astype(F32)


def _window_sums(e, w):
    step = 1
    while step < w:
        e = e + pltpu.roll(e, step, axis=0)
        step *= 2
    return e


def _sigmoid(x):
    return 0.5 * jnp.tanh(0.5 * x) + 0.5


def _gate_merge(pa, po, gl):
    ga = _sigmoid(gl[:, :D_MODEL])
    gb = _sigmoid(gl[:, D_MODEL:])
    return (ga * pa + gb * po).astype(BF16)


def _merge(x, pool_a, attn_o, gl, wbp_ref, wba_ref, wout_ref):
    pa = _dot(pool_a.astype(BF16), wbp_ref[...])
    po = _dot(attn_o.astype(BF16), wba_ref[...])
    return x + _dot(_gate_merge(pa, po, gl), wout_ref[...])


def _mix_prompt_kernel(sinks_ref, u_ref, q_ref, k_ref, v_ref, gl_ref, wpool_ref, pscale_ref,
                       wbp_ref, wba_ref, wout_ref, xo_ref, o_ref, kt_ref, vt_ref, uext, kdup, vdup,
                       oscr, bias, pa_scr, mixed, *, tq, nt):
    s = pl.program_id(0)
    i = s % nt
    pre = POOL_STATE + 1

    @pl.when(s == 0)
    def _():
        mixed[...] = jnp.zeros(mixed.shape, BF16)

    @pl.when(i == 0)
    def _():
        uext[0:pre, :] = jnp.zeros((pre, POOL_WIDTH), F32)
        kdup[:, 0:WINDOW, :] = jnp.zeros((N_KV_HEADS, WINDOW, LANES), BF16)
        vdup[:, 0:WINDOW, :] = jnp.zeros((N_KV_HEADS, WINDOW, LANES), BF16)
        bias[0] = _band_bias4(WINDOW, 2 * WINDOW, WINDOW)
        bias[1] = _band_bias4(WINDOW, 2 * WINDOW)

    uext[pre:pre + tq, :] = u_ref[...]
    for dst, src in ((kdup, k_ref), (vdup, v_ref)):
        for g, d in enumerate(_dup_kv_heads(src[...])):
            dst[g, WINDOW:WINDOW + tq, :] = d
    sink_cols = [_sink_column(WINDOW, g, sinks_ref) for g in range(N_KV_HEADS)]

    piece = D_MODEL // 4

    def out_prev(n):
        cols = slice(n * piece, (n + 1) * piece)
        o_ref[:, cols] = xo_ref[:, cols] + _dot(mixed[...], wout_ref[:, cols])

    pos1 = i * tq + lax.broadcasted_iota(jnp.int32, (tq, 1), 0) + 1
    a_parts = []

    def pool_group(g):
        w = POOL_WINDOWS[g]
        cols = slice(g * POOL_GROUP, (g + 1) * POOL_GROUP)
        e = uext[:, cols]
        inv_cnt = 1.0 / jnp.minimum(pos1, w).astype(F32)
        d = _window_sums(e, w)[pre:] * inv_cnt - e[pre:]
        y = _dot(d.astype(BF16), wpool_ref[g])
        a_parts.append((y * pscale_ref[:, cols]).astype(BF16))

    n_units = N_KV_HEADS * tq // WINDOW

    def pooled_branch(n):
        cols = slice(n * 2 * piece, (n + 1) * 2 * piece)
        pa_scr[:, cols] = _dot(jnp.concatenate(a_parts, axis=1), wbp_ref[:, cols])

    def early(n):
        out_prev(n)
        pool_group(n)

    def merge_half(half):
        rows = slice(half * tq // 2, (half + 1) * tq // 2)
        po = _dot(oscr[rows, :].astype(BF16), wba_ref[...])
        mixed[rows, :] = _gate_merge(pa_scr[rows, :], po, gl_ref[rows, :])

    half_done = n_units // 2 + 1
    after_tick = [functools.partial(early, n) for n in range(4)]
    after_tick += [functools.partial(pooled_branch, n) for n in range(2)]
    assert len(after_tick) <= half_done + 1
    after_tick += [None] * (half_done + 1 - len(after_tick)) + [functools.partial(merge_half, 0)]

    def unit(u):
        nb, g = divmod(u, N_KV_HEADS)
        return nb, g, slice(nb * WINDOW, (nb + 1) * WINDOW), slice(nb * WINDOW, (nb + 2) * WINDOW)

    def scores(u, _):
        nb, g, rows, keys = unit(u)
        slot = jnp.minimum(i * (tq // WINDOW) + nb, 1)
        return _attn_scores(q_ref[rows, :], kdup[g, keys, :], g, bias[slot])

    def softmax(u, s):
        e, inv = _sink_softmax(s, sink_cols[unit(u)[1]])
        return e.astype(BF16), inv

    def values(u, weights):
        nb, g, rows, keys = unit(u)
        e, inv = weights
        oscr[rows, g * 2 * LANES:(g + 1) * 2 * LANES] = _unstack_heads(
            _dot(e, vdup[g, keys, :]) * inv)

    _software_pipeline(n_units, [scores, softmax, values], after_tick)
    merge_half(1)

    @pl.when(i == nt - 1)
    def _():
        kt_ref[0] = k_ref[tq - WINDOW:tq, :].T
        vt_ref[0] = v_ref[tq - WINDOW:tq, :].T

    uext[0:pre, :] = uext[tq:tq + pre, :]
    kdup[:, 0:WINDOW, :] = kdup[:, tq:tq + WINDOW, :]
    vdup[:, 0:WINDOW, :] = vdup[:, tq:tq + WINDOW, :]


def _mix_prompt(x, u, q, k, v, gl, sinks, wpool_bf, pscale, wbp_bf, wba_bf, wout_bf, batch, tq):
    n = x.shape[0]
    n_tiles = n // tq
    nt = n_tiles // batch
    row = lambda c: pl.BlockSpec((tq, c), lambda s: (jnp.minimum(s, n_tiles - 1), 0))
    row_prev = pl.BlockSpec((tq, D_MODEL), lambda s: (jnp.maximum(s - 1, 0), 0))
    last_t = pl.BlockSpec((1, KV_WIDTH, WINDOW),
                          lambda s: (jnp.minimum(s, n_tiles - 1) // nt, 0, 0))
    return pl.pallas_call(
        functools.partial(_mix_prompt_kernel, tq=tq, nt=nt),
        grid=(n_tiles + 1,),
        in_specs=[pl.BlockSpec(memory_space=pltpu.SMEM),
                  row(POOL_WIDTH), row(ATTN_WIDTH), row(KV_WIDTH), row(KV_WIDTH), row(2 * D_MODEL),
                  _const_spec(wpool_bf.shape), _const_spec(pscale.shape),
                  _const_spec(wbp_bf.shape), _const_spec(wba_bf.shape), _const_spec(wout_bf.shape),
                  row_prev],
        out_specs=[row_prev, last_t, last_t],
        out_shape=[jax.ShapeDtypeStruct((n, D_MODEL), F32),
                   jax.ShapeDtypeStruct((batch, KV_WIDTH, WINDOW), F32),
                   jax.ShapeDtypeStruct((batch, KV_WIDTH, WINDOW), F32)],
        scratch_shapes=[pltpu.VMEM((POOL_STATE + 1 + tq, POOL_WIDTH), F32),
                        pltpu.VMEM((N_KV_HEADS, WINDOW + tq, LANES), BF16),
                        pltpu.VMEM((N_KV_HEADS, WINDOW + tq, LANES), BF16),
                        pltpu.VMEM((tq, ATTN_WIDTH), F32),
                        pltpu.VMEM((2, GQA_GROUP * WINDOW, 2 * WINDOW), F32),
                        pltpu.VMEM((tq, D_MODEL), F32),
                        pltpu.VMEM((tq, D_MODEL), BF16)],
        compiler_params=_params(1),
        name="mix_prompt",
    )(sinks, u, q, k, v, gl, wpool_bf, pscale, wbp_bf, wba_bf, wout_bf, x)


def _tile_shift(cur, prev, j, row):
    return jnp.where(row < j, pltpu.roll(prev, j, axis=1), pltpu.roll(cur, j, axis=1))


SEQ_BLOCK = WINDOW // SUBLANES


def _sample_bias4():
    shape = (GQA_GROUP * WINDOW, 2 * WINDOW)
    i = lax.broadcasted_iota(jnp.int32, shape, 0) & (WINDOW - 1)
    j = lax.broadcasted_iota(jnp.int32, shape, 1)
    b, t = i >> 3, i & (SUBLANES - 1)
    new = j - WINDOW
    valid = ((j < WINDOW) & (j >= t)) | ((new >> 3 == b) & ((new & (SUBLANES - 1)) <= t) & (new >= 0))
    return jnp.where(valid, 0.0, -jnp.inf).astype(F32)


def _attend_cached(q, knew, vnew, ckt_ref, cvt_ref, b0, bias4, sinks_ref):
    r, t = WINDOW, SUBLANES
    nt_dims = (((1,), (1,)), ((), ()))
    kn, vn = _dup_kv_heads(knew), _dup_kv_heads(vnew)

    def per_seq(x):
        return [jnp.concatenate([x[j * r + b * t:j * r + (b + 1) * t] for j in range(GQA_GROUP)],
                                axis=0) for b in range(SEQ_BLOCK)]

    def stacked(xs):
        return jnp.concatenate([xs[b][j * t:(j + 1) * t] for j in range(GQA_GROUP)
                                for b in range(SEQ_BLOCK)], axis=0)

    outs = []
    for g in range(N_KV_HEADS):
        kv_rows = slice(g * HEAD_DIM, (g + 1) * HEAD_DIM)
        qs = _stack_heads(q, g)
        s_new = lax.dot_general(qs.astype(BF16), kn[g], nt_dims, preferred_element_type=F32)
        s_cache = []
        for b, qb in enumerate(per_seq(qs)):
            kt = ckt_ref[b0 + b, kv_rows, :]
            s_cache.append(_dot(qb.astype(BF16), jnp.concatenate([kt, kt], axis=0).astype(BF16)))
        s = jnp.concatenate([stacked(s_cache), s_new], axis=1) + bias4
        p, inv = _sink_softmax(s, _sink_column(r, g, sinks_ref))
        o_cache = []
        for b, pb in enumerate(per_seq(p[:, :r])):
            vt = cvt_ref[b0 + b, kv_rows, :]
            o_cache.append(lax.dot_general(pb.astype(BF16),
                                           jnp.concatenate([vt, vt], axis=0).astype(BF16),
                                           nt_dims, preferred_element_type=F32))
        o = stacked(o_cache) + _dot(p[:, r:].astype(BF16), vn[g])
        outs.append(_unstack_heads(o * inv))
    return jnp.concatenate(outs, axis=1)


def _slide_cache(new_rows, ct_ref, out_ref, b0):
    t = SUBLANES
    new_t = new_rows.T
    is_new = lax.broadcasted_iota(jnp.int32, (KV_WIDTH, WINDOW), 1) >= WINDOW - t
    for b in range(SEQ_BLOCK):
        kept = pltpu.roll(ct_ref[b0 + b], WINDOW - t, axis=1)
        fresh = pltpu.roll(new_t, (WINDOW - t - b * t) % WINDOW, axis=1)
        out_ref[b0 + b] = jnp.where(is_new, fresh, kept)


def _mix_sample_kernel(sinks_ref, x_ref, u_ref, st_ref, q_ref, k_ref, v_ref, ckt_ref, cvt_ref, gl_ref,
                       wpool_ref, pscale_ref, wbp_ref, wba_ref, wout_ref, o_ref, kst_ref, vst_ref,
                       oscr, *, gs, pos0):
    t = SUBLANES
    rows = gs * t

    row = lax.broadcasted_iota(jnp.int32, (gs, t, POOL_GROUP), 1)
    cnt_pos = pos0 + lax.broadcasted_iota(jnp.int32, (gs, t, 1), 1) + 1
    a_parts = []
    for g, w in enumerate(POOL_WINDOWS):
        cols = slice(g * POOL_GROUP, (g + 1) * POOL_GROUP)
        tiles = [st_ref[:, 0:t, cols], st_ref[:, t:2 * t, cols], u_ref[:, :, cols]]
        xt = tiles[2]
        step = 1
        while step < w and step < t:
            tiles = [tiles[n] + _tile_shift(tiles[n], tiles[max(n - 1, 0)], step, row)
                     for n in range(3)]
            step *= 2
        acc = tiles[2] + tiles[1] if w == 2 * t else tiles[2]
        cnt = jnp.minimum(cnt_pos, w).astype(F32)
        d = (acc / cnt - xt).reshape(rows, POOL_GROUP)
        y = _dot(d.astype(BF16), wpool_ref[g])
        a_parts.append(y * pscale_ref[:, cols])
    pool_a = jnp.concatenate(a_parts, axis=1)

    bias = _sample_bias4()
    for blk in range(gs // SEQ_BLOCK):
        rs = slice(blk * WINDOW, (blk + 1) * WINDOW)
        oscr[rs, :] = _attend_cached(q_ref[rs, :], k_ref[rs, :], v_ref[rs, :], ckt_ref, cvt_ref,
                                     blk * SEQ_BLOCK, bias, sinks_ref)
        _slide_cache(k_ref[rs, :], ckt_ref, kst_ref, blk * SEQ_BLOCK)
        _slide_cache(v_ref[rs, :], cvt_ref, vst_ref, blk * SEQ_BLOCK)

    o_ref[...] = _merge(x_ref[...], pool_a, oscr[...], gl_ref[...], wbp_ref, wba_ref, wout_ref)


def _mix_sample(x, u3, stpad, q, k, v, ck, cv, gl, sinks, wpool_bf, pscale, wbp_bf, wba_bf, wout_bf,
                gs, pos0):
    nseq, t = u3.shape[0], u3.shape[1]
    rows = gs * t
    row = lambda c: pl.BlockSpec((rows, c), lambda i: (i, 0))
    seq = lambda a, b: pl.BlockSpec((gs, a, b), lambda i: (i, 0, 0))
    return pl.pallas_call(
        functools.partial(_mix_sample_kernel, gs=gs, pos0=pos0),
        grid=(nseq // gs,),
        in_specs=[pl.BlockSpec(memory_space=pltpu.SMEM),
                  row(D_MODEL), seq(t, POOL_WIDTH), seq(2 * t, POOL_WIDTH),
                  row(ATTN_WIDTH), row(KV_WIDTH), row(KV_WIDTH),
                  seq(WINDOW, KV_WIDTH), seq(WINDOW, KV_WIDTH), row(2 * D_MODEL),
                  _const_spec(wpool_bf.shape), _const_spec(pscale.shape),
                  _const_spec(wbp_bf.shape), _const_spec(wba_bf.shape), _const_spec(wout_bf.shape)],
        out_specs=[row(D_MODEL), seq(KV_WIDTH, WINDOW), seq(KV_WIDTH, WINDOW)],
        out_shape=[jax.ShapeDtypeStruct((nseq * t, D_MODEL), F32),
                   jax.ShapeDtypeStruct((nseq, KV_WIDTH, WINDOW), F32),
                   jax.ShapeDtypeStruct((nseq, KV_WIDTH, WINDOW), F32)],
        scratch_shapes=[pltpu.VMEM((rows, ATTN_WIDTH), F32)],
        compiler_params=_params(1),
        name="mix_sample",
    )(sinks, x, u3, stpad, q, k, v, ck, cv, gl, wpool_bf, pscale, wbp_bf, wba_bf, wout_bf)


FF_CHUNK = 256


def _ff_cols(c, part):
    start = part * D_FF + c * FF_CHUNK
    return slice(start, start + FF_CHUNK)


def _gated_act(taps, cw_ref, cb_ref, c):
    def conv(part):
        x0, x1, x2 = taps(c, part)
        cols = _ff_cols(c, part)
        return (x0 * cw_ref[0:1, cols] + x1 * cw_ref[1:2, cols] + x2 * cw_ref[2:3, cols]
                + cb_ref[:, cols])
    h = conv(0)
    return ((h * (1.0 + jnp.tanh(h))) * conv(1)).astype(BF16)


def _ffn_prompt_kernel(x_ref, g_ref, wup_ref, cw_ref, cb_ref, wdn_ref, gf_ref, o_ref, cs_ref,
                       upx, act, *, tm):
    i = pl.program_id(1)
    pre = SUBLANES

    @pl.when(i == 0)
    def _():
        upx[0:pre, :] = jnp.zeros((pre, 2 * D_FF), F32)

    x = x_ref[...]
    h = _rmsnorm(x, g_ref[...]).astype(BF16)
    upx[pre:pre + tm, :] = _dot(h, wup_ref[...])

    def taps(c, part):
        e = upx[:, _ff_cols(c, part)]
        return pltpu.roll(e, 2, axis=0)[pre:], pltpu.roll(e, 1, axis=0)[pre:], e[pre:]

    y = x
    for c in range(D_FF // FF_CHUNK):
        a = _gated_act(taps, cw_ref, cb_ref, c)
        y = y + _dot(a, wdn_ref[c * FF_CHUNK:(c + 1) * FF_CHUNK, :])
    o_ref[...] = _rmsnorm(y, gf_ref[...])
    cs_ref[0] = upx[pre + tm - CONV_STATE:pre + tm, :]
    upx[0:pre, :] = upx[tm:tm + pre, :]


def _ffn_prompt(x, g_ffn, wup_bf, conv_w, conv_b, wdn_bf, g_final, batch, tm):
    n = x.shape[0]
    nt = n // batch // tm
    row = pl.BlockSpec((tm, D_MODEL), lambda b, i: (b * nt + i, 0))
    return pl.pallas_call(
        functools.partial(_ffn_prompt_kernel, tm=tm),
        grid=(batch, nt),
        in_specs=[row, _const_spec((1, D_MODEL)), _const_spec(wup_bf.shape),
                  _const_spec(conv_w.shape), _const_spec(conv_b.shape), _const_spec(wdn_bf.shape),
                  _const_spec((1, D_MODEL))],
        out_specs=[row, pl.BlockSpec((1, CONV_STATE, 2 * D_FF), lambda b, i: (b, 0, 0))],
        out_shape=[jax.ShapeDtypeStruct((n, D_MODEL), F32),
                   jax.ShapeDtypeStruct((batch, CONV_STATE, 2 * D_FF), F32)],
        scratch_shapes=[pltpu.VMEM((SUBLANES + tm, 2 * D_FF), F32),
                        pltpu.VMEM((tm, D_FF), BF16)],
        compiler_params=_params(2),
        name="ffn_prompt",
    )(x, g_ffn, wup_bf, conv_w, conv_b, wdn_bf, g_final)


def _ffn_sample_kernel(x_ref, st_ref, g_ref, wup_ref, cw_ref, cb_ref, wdn_ref, gf_ref,
                       o_ref, cs_ref, up, act, *, gs):
    t = SUBLANES
    rows = gs * t
    x = x_ref[...]
    h = _rmsnorm(x, g_ref[...]).astype(BF16)
    up[...] = _dot(h, wup_ref[...])
    row = lax.broadcasted_iota(jnp.int32, (gs, t, FF_CHUNK), 1)

    def taps(c, part):
        cols = _ff_cols(c, part)
        x2 = up[:, cols].reshape(gs, t, FF_CHUNK)
        s0 = jnp.broadcast_to(st_ref[:, 0:1, cols], x2.shape)
        s1 = jnp.broadcast_to(st_ref[:, 1:2, cols], x2.shape)
        x1 = jnp.where(row < 1, s1, pltpu.roll(x2, 1, axis=1))
        x0 = jnp.where(row < 1, s0, jnp.where(row < 2, s1, pltpu.roll(x2, 2, axis=1)))
        flat = lambda a: a.reshape(rows, FF_CHUNK)
        return flat(x0), flat(x1), flat(x2)

    for c in range(D_FF // FF_CHUNK):
        act[:, c * FF_CHUNK:(c + 1) * FF_CHUNK] = _gated_act(taps, cw_ref, cb_ref, c)

    y = x + _dot(act[...], wdn_ref[...])
    o_ref[...] = _rmsnorm(y, gf_ref[...])
    for c0 in range(0, 2 * D_FF, 4 * FF_CHUNK):
        cols = slice(c0, min(c0 + 4 * FF_CHUNK, 2 * D_FF))
        cs_ref[:, :, cols] = up[:, cols].reshape(gs, t, -1)[:, t - CONV_STATE:, :]


def _ffn_sample(x, st, g_ffn, wup_bf, conv_w, conv_b, wdn_bf, g_final, gs):
    nseq = st.shape[0]
    t = x.shape[0] // nseq
    rows = gs * t
    row = pl.BlockSpec((rows, D_MODEL), lambda i: (i, 0))
    st_spec = pl.BlockSpec((gs, CONV_STATE, 2 * D_FF), lambda i: (i, 0, 0))
    return pl.pallas_call(
        functools.partial(_ffn_sample_kernel, gs=gs),
        grid=(nseq // gs,),
        in_specs=[row, st_spec,
                  _const_spec((1, D_MODEL)), _const_spec(wup_bf.shape),
                  _const_spec(conv_w.shape), _const_spec(conv_b.shape), _const_spec(wdn_bf.shape),
                  _const_spec((1, D_MODEL))],
        out_specs=[row, st_spec],
        out_shape=[jax.ShapeDtypeStruct((nseq * t, D_MODEL), F32),
                   jax.ShapeDtypeStruct((nseq, CONV_STATE, 2 * D_FF), F32)],
        scratch_shapes=[pltpu.VMEM((rows, 2 * D_FF), F32), pltpu.VMEM((rows, D_FF), BF16)],
        compiler_params=_params(1),
        name="ffn_sample",
    )(x, st, g_ffn, wup_bf, conv_w, conv_b, wdn_bf, g_final)


TM_IN = 1024
TQ_MIX = 512
TM_FFN = 512
GS_MIX = 32
GS_FFN = 64


def kernel(x_prompt, x_sample, state_pool, cache_k, cache_v, state_conv, g_mix, w_in, w_pool_grp,
           pool_scale, sinks, w_branch_pool, w_branch_attn, w_out, g_ffn, w_up, conv_w, conv_b,
           w_down, g_final):
    depth = g_mix.shape[0]
    batch, seq, _ = x_prompt.shape
    dec_batch, dec_seq, _ = x_sample.shape
    assert dec_seq == SUBLANES and seq % TQ_MIX == 0 and seq % TM_FFN == 0
    assert depth == 1

    yp = x_prompt.reshape(batch * seq, D_MODEL)
    ys = x_sample.reshape(dec_batch * dec_seq, D_MODEL)
    outs = [[] for _ in range(8)]
    gfin = g_final.reshape(1, D_MODEL)
    for l in range(depth):
        gmix = g_mix[l].reshape(1, D_MODEL)
        gffn = g_ffn[l].reshape(1, D_MODEL)
        w_in_bf = w_in[l].astype(BF16)
        wpool_bf = w_pool_grp[l].astype(BF16)
        pscale = pool_scale[l].reshape(1, POOL_WIDTH)
        wbp_bf = w_branch_pool[l].astype(BF16)
        wba_bf = w_branch_attn[l].astype(BF16)
        wout_bf = w_out[l].astype(BF16)
        wup_bf = w_up[l].astype(BF16)
        wdn_bf = w_down[l].astype(BF16)
        half_gate = jnp.where(jnp.arange(2 * D_FF) < D_FF, 0.5, 1.0).astype(F32)
        cw = conv_w[l] * half_gate
        cb = (conv_b[l] * half_gate).reshape(1, 2 * D_FF)

        u, q, k, v, gl = _in_proj(yp, gmix, w_in_bf, TM_IN)
        x1, kt, vt = _mix_prompt(yp, u, q, k, v, gl, sinks[l], wpool_bf, pscale, wbp_bf, wba_bf,
                                 wout_bf, batch, TQ_MIX)
        yp, conv_p = _ffn_prompt(x1, gffn, wup_bf, cw, cb, wdn_bf, gfin, batch, TM_FFN)
        outs[0].append(u.reshape(batch, seq, POOL_WIDTH)[:, seq - POOL_STATE:])
        untranspose = lambda t: t.reshape(-1, N_KV_HEADS, HEAD_DIM, WINDOW).transpose(0, 3, 1, 2)
        outs[1].append(untranspose(kt))
        outs[2].append(untranspose(vt))
        outs[3].append(conv_p)

        us, qs, ks, vs, gls = _in_proj(ys, gmix, w_in_bf, TM_IN)
        u3 = us.reshape(dec_batch, dec_seq, POOL_WIDTH)
        stpad = jnp.pad(state_pool[l], ((0, 0), (1, 0), (0, 0)))
        transposed = lambda c: c.transpose(0, 2, 3, 1).reshape(dec_batch, KV_WIDTH, WINDOW)
        ck, cv = transposed(cache_k[l]), transposed(cache_v[l])
        x1s, kst, vst = _mix_sample(ys, u3, stpad, qs, ks, vs, ck, cv, gls, sinks[l], wpool_bf,
                                    pscale, wbp_bf, wba_bf, wout_bf, GS_MIX, PAST_LEN)
        ys, conv_s = _ffn_sample(x1s, state_conv[l], gffn, wup_bf, cw, cb, wdn_bf, gfin, GS_FFN)
        outs[4].append(jnp.concatenate([state_pool[l], u3], axis=1)[:, dec_seq:])
        outs[5].append(untranspose(kst))
        outs[6].append(untranspose(vst))
        outs[7].append(conv_s)
    return (yp.reshape(batch, seq, D_MODEL), ys.reshape(dec_batch, dec_seq, D_MODEL),
            *[jnp.stack(o) for o in outs])
```

```python
import functools

import jax
import jax.numpy as jnp
from jax import lax
from jax.experimental import pallas as pl
from jax.experimental.pallas import tpu as pltpu

D_MODEL = 1024
POOL_WIDTH = 512
POOL_WINDOWS = (2, 4, 8, 16)
POOL_GROUP = 128
POOL_STATE = 15
HEAD_DIM = 64
N_HEADS = 8
N_KV_HEADS = 2
GQA_GROUP = N_HEADS // N_KV_HEADS
WINDOW = 128
ATTN_WIDTH = N_HEADS * HEAD_DIM
KV_WIDTH = N_KV_HEADS * HEAD_DIM
ATTN_SCALE = HEAD_DIM ** -0.5
LOG2_E = 1.4426950408889634
LOGIT_SCALE = ATTN_SCALE * LOG2_E
D_FF = 2816
CONV_STATE = 2
EPS = 1e-6
PAST_LEN = 16384

LANES = 128
SUBLANES = 8
VMEM_LIMIT = 56 * 1024 * 1024

F32 = jnp.float32
BF16 = jnp.bfloat16


def _rmsnorm(x, g):
    inv = lax.rsqrt(jnp.mean(x * x, axis=-1, keepdims=True) + EPS)
    return (x * inv) * g


def _dot(a, b):
    return jnp.dot(a, b, preferred_element_type=F32)


def _const_spec(shape):
    nd = len(shape)
    return pl.BlockSpec(shape, lambda *_: (0,) * nd, pipeline_mode=pl.Buffered(1))


def _params(n_axes):
    return pltpu.CompilerParams(dimension_semantics=("arbitrary",) * n_axes,
                                vmem_limit_bytes=VMEM_LIMIT)


def _in_proj_kernel(x_ref, g_ref, w_ref, u_ref, q_ref, k_ref, v_ref, gl_ref):
    h = _rmsnorm(x_ref[...], g_ref[...]).astype(BF16)
    o_q = POOL_WIDTH
    o_k = o_q + ATTN_WIDTH
    o_gl = o_k + 2 * KV_WIDTH
    u_ref[...] = _dot(h, w_ref[:, 0:o_q])
    q_ref[...] = _dot(h, w_ref[:, o_q:o_k])
    kv = _dot(h, w_ref[:, o_k:o_gl])
    k_ref[...] = kv[:, :KV_WIDTH]
    v_ref[...] = kv[:, KV_WIDTH:]
    gl_ref[...] = _dot(h, w_ref[:, o_gl:])


def _in_proj(x, g_mix, w_in_bf, tm):
    n = x.shape[0]
    in_width = w_in_bf.shape[1]
    row = lambda c: pl.BlockSpec((tm, c), lambda i: (i, 0))
    return pl.pallas_call(
        _in_proj_kernel,
        grid=(n // tm,),
        in_specs=[row(D_MODEL), _const_spec((1, D_MODEL)), _const_spec((D_MODEL, in_width))],
        out_specs=[row(POOL_WIDTH), row(ATTN_WIDTH), row(KV_WIDTH), row(KV_WIDTH), row(2 * D_MODEL)],
        out_shape=[jax.ShapeDtypeStruct((n, POOL_WIDTH), F32),
                   jax.ShapeDtypeStruct((n, ATTN_WIDTH), F32),
                   jax.ShapeDtypeStruct((n, KV_WIDTH), F32),
                   jax.ShapeDtypeStruct((n, KV_WIDTH), F32),
                   jax.ShapeDtypeStruct((n, 2 * D_MODEL), F32)],
        compiler_params=_params(1),
        name="in_proj",
    )(x, g_mix, w_in_bf)


def _dup_kv_heads(x):
    lo = lax.broadcasted_iota(jnp.int32, x.shape, 1) < HEAD_DIM
    r = pltpu.roll(x, HEAD_DIM, axis=1)
    return [jnp.where(lo, x, r).astype(BF16), jnp.where(lo, r, x).astype(BF16)]


def _stack_heads(q, g):
    lo = lax.broadcasted_iota(jnp.int32, (q.shape[0], LANES), 1) < HEAD_DIM
    scale_lo = jnp.where(lo, LOGIT_SCALE, 0.0).astype(F32)
    scale_hi = jnp.where(lo, 0.0, LOGIT_SCALE).astype(F32)
    base = g * GQA_GROUP * HEAD_DIM
    p0 = q[:, base:base + LANES]
    p1 = q[:, base + LANES:base + 2 * LANES]
    return jnp.concatenate([p0 * scale_lo, p0 * scale_hi, p1 * scale_lo, p1 * scale_hi], axis=0)


def _attn_scores(q, kdup, g, bias4):
    return lax.dot_general(_stack_heads(q, g).astype(BF16), kdup, (((1,), (1,)), ((), ())),
                           preferred_element_type=F32) + bias4


def _sink_column(r, g, sinks_ref):
    row4 = lax.broadcasted_iota(jnp.int32, (GQA_GROUP * r, 1), 0)
    sk = jnp.full((GQA_GROUP * r, 1), sinks_ref[g * GQA_GROUP + GQA_GROUP - 1], F32)
    for hh in range(GQA_GROUP - 2, -1, -1):
        sk = jnp.where(row4 < (hh + 1) * r, sinks_ref[g * GQA_GROUP + hh], sk)
    return sk * LOG2_E


def _sink_softmax(s, sk):
    m = jnp.maximum(jnp.max(s, axis=-1, keepdims=True), sk)
    e = jnp.exp2(s - m)
    den = jnp.sum(e, axis=-1, keepdims=True) + jnp.exp2(sk - m)
    return e, 1.0 / den


def _unstack_heads(o):
    r = o.shape[0] // GQA_GROUP
    lo = lax.broadcasted_iota(jnp.int32, (r, LANES), 1) < HEAD_DIM
    return jnp.concatenate([jnp.where(lo, o[0:r], o[r:2 * r]),
                            jnp.where(lo, o[2 * r:3 * r], o[3 * r:4 * r])], axis=1)


def _software_pipeline(n_units, stages, after_tick=()):
    vals = {}
    for tick in range(n_units + len(stages) - 1):
        for k in reversed(range(len(stages))):
            u = tick - k
            if 0 <= u < n_units:
                vals[u] = stages[k](u, vals.get(u))
        if tick < len(after_tick) and after_tick[tick] is not None:
            after_tick[tick]()


def _band_bias4(r, s, first_key=0):
    i = lax.broadcasted_iota(jnp.int32, (GQA_GROUP * r, s), 0) & (r - 1)
    j = lax.broadcasted_iota(jnp.int32, (GQA_GROUP * r, s), 1)
    valid = (j >= jnp.maximum(i, first_key)) & (j <= i + WINDOW)
    return jnp.where(valid, 0.0, -jnp.inf).astype(F32)


def _window_sums(e, w):
    step = 1
    while step < w:
        e = e + pltpu.roll(e, step, axis=0)
        step *= 2
    return e


def _sigmoid(x):
    return 0.5 * jnp.tanh(0.5 * x) + 0.5


def _gate_merge(pa, po, gl):
    ga = _sigmoid(gl[:, :D_MODEL])
    gb = _sigmoid(gl[:, D_MODEL:])
    return (ga * pa + gb * po).astype(BF16)


def _merge(x, pool_a, attn_o, gl, wbp_ref, wba_ref, wout_ref):
    pa = _dot(pool_a.astype(BF16), wbp_ref[...])
    po = _dot(attn_o.astype(BF16), wba_ref[...])
    return x + _dot(_gate_merge(pa, po, gl), wout_ref[...])


def _mix_prompt_kernel(sinks_ref, u_ref, q_ref, k_ref, v_ref, gl_ref, wpool_ref, pscale_ref,
                       wbp_ref, wba_ref, wout_ref, xo_ref, o_ref, kt_ref, vt_ref, uext, kdup, vdup,
                       oscr, bias, pa_scr, mixed, *, tq, nt):
    s = pl.program_id(0)
    i = s % nt
    pre = POOL_STATE + 1

    @pl.when(s == 0)
    def _():
        mixed[...] = jnp.zeros(mixed.shape, BF16)

    @pl.when(i == 0)
    def _():
        uext[0:pre, :] = jnp.zeros((pre, POOL_WIDTH), F32)
        kdup[:, 0:WINDOW, :] = jnp.zeros((N_KV_HEADS, WINDOW, LANES), BF16)
        vdup[:, 0:WINDOW, :] = jnp.zeros((N_KV_HEADS, WINDOW, LANES), BF16)
        bias[0] = _band_bias4(WINDOW, 2 * WINDOW, WINDOW)
        bias[1] = _band_bias4(WINDOW, 2 * WINDOW)

    uext[pre:pre + tq, :] = u_ref[...]
    for dst, src in ((kdup, k_ref), (vdup, v_ref)):
        for g, d in enumerate(_dup_kv_heads(src[...])):
            dst[g, WINDOW:WINDOW + tq, :] = d
    sink_cols = [_sink_column(WINDOW, g, sinks_ref) for g in range(N_KV_HEADS)]

    piece = D_MODEL // 4

    def out_prev(n):
        cols = slice(n * piece, (n + 1) * piece)
        o_ref[:, cols] = xo_ref[:, cols] + _dot(mixed[...], wout_ref[:, cols])

    pos1 = i * tq + lax.broadcasted_iota(jnp.int32, (tq, 1), 0) + 1
    a_parts = []

    def pool_group(g):
        w = POOL_WINDOWS[g]
        cols = slice(g * POOL_GROUP, (g + 1) * POOL_GROUP)
        e = uext[:, cols]
        inv_cnt = 1.0 / jnp.minimum(pos1, w).astype(F32)
        d = _window_sums(e, w)[pre:] * inv_cnt - e[pre:]
        y = _dot(d.astype(BF16), wpool_ref[g])
        a_parts.append((y * pscale_ref[:, cols]).astype(BF16))

    n_units = N_KV_HEADS * tq // WINDOW

    def pooled_branch(n):
        cols = slice(n * 2 * piece, (n + 1) * 2 * piece)
        pa_scr[:, cols] = _dot(jnp.concatenate(a_parts, axis=1), wbp_ref[:, cols])

    def early(n):
        out_prev(n)
        pool_group(n)

    def merge_half(half):
        rows = slice(half * tq // 2, (half + 1) * tq // 2)
        po = _dot(oscr[rows, :].astype(BF16), wba_ref[...])
        mixed[rows, :] = _gate_merge(pa_scr[rows, :], po, gl_ref[rows, :])

    half_done = n_units // 2 + 1
    after_tick = [functools.partial(early, n) for n in range(4)]
    after_tick += [functools.partial(pooled_branch, n) for n in range(2)]
    assert len(after_tick) <= half_done + 1
    after_tick += [None] * (half_done + 1 - len(after_tick)) + [functools.partial(merge_half, 0)]

    def unit(u):
        nb, g = divmod(u, N_KV_HEADS)
        return nb, g, slice(nb * WINDOW, (nb + 1) * WINDOW), slice(nb * WINDOW, (nb + 2) * WINDOW)

    def scores(u, _):
        nb, g, rows, keys = unit(u)
        slot = jnp.minimum(i * (tq // WINDOW) + nb, 1)
        return _attn_scores(q_ref[rows, :], kdup[g, keys, :], g, bias[slot])

    def softmax(u, s):
        e, inv = _sink_softmax(s, sink_cols[unit(u)[1]])
        return e.astype(BF16), inv

    def values(u, weights):
        nb, g, rows, keys = unit(u)
        e, inv = weights
        oscr[rows, g * 2 * LANES:(g + 1) * 2 * LANES] = _unstack_heads(
            _dot(e, vdup[g, keys, :]) * inv)

    _software_pipeline(n_units, [scores, softmax, values], after_tick)
    merge_half(1)

    @pl.when(i == nt - 1)
    def _():
        kt_ref[0] = k_ref[tq - WINDOW:tq, :].T
        vt_ref[0] = v_ref[tq - WINDOW:tq, :].T

    uext[0:pre, :] = uext[tq:tq + pre, :]
    kdup[:, 0:WINDOW, :] = kdup[:, tq:tq + WINDOW, :]
    vdup[:, 0:WINDOW, :] = vdup[:, tq:tq + WINDOW, :]


def _mix_prompt(x, u, q, k, v, gl, sinks, wpool_bf, pscale, wbp_bf, wba_bf, wout_bf, batch, tq):
    n = x.shape[0]
    n_tiles = n // tq
    nt = n_tiles // batch
    row = lambda c: pl.BlockSpec((tq, c), lambda s: (jnp.minimum(s, n_tiles - 1), 0))
    row_prev = pl.BlockSpec((tq, D_MODEL), lambda s: (jnp.maximum(s - 1, 0), 0))
    last_t = pl.BlockSpec((1, KV_WIDTH, WINDOW),
                          lambda s: (jnp.minimum(s, n_tiles - 1) // nt, 0, 0))
    return pl.pallas_call(
        functools.partial(_mix_prompt_kernel, tq=tq, nt=nt),
        grid=(n_tiles + 1,),
        in_specs=[pl.BlockSpec(memory_space=pltpu.SMEM),
                  row(POOL_WIDTH), row(ATTN_WIDTH), row(KV_WIDTH), row(KV_WIDTH), row(2 * D_MODEL),
                  _const_spec(wpool_bf.shape), _const_spec(pscale.shape),
                  _const_spec(wbp_bf.shape), _const_spec(wba_bf.shape), _const_spec(wout_bf.shape),
                  row_prev],
        out_specs=[row_prev, last_t, last_t],
        out_shape=[jax.ShapeDtypeStruct((n, D_MODEL), F32),
                   jax.ShapeDtypeStruct((batch, KV_WIDTH, WINDOW), F32),
                   jax.ShapeDtypeStruct((batch, KV_WIDTH, WINDOW), F32)],
        scratch_shapes=[pltpu.VMEM((POOL_STATE + 1 + tq, POOL_WIDTH), F32),
                        pltpu.VMEM((N_KV_HEADS, WINDOW + tq, LANES), BF16),
                        pltpu.VMEM((N_KV_HEADS, WINDOW + tq, LANES), BF16),
                        pltpu.VMEM((tq, ATTN_WIDTH), F32),
                        pltpu.VMEM((2, GQA_GROUP * WINDOW, 2 * WINDOW), F32),
                        pltpu.VMEM((tq, D_MODEL), F32),
                        pltpu.VMEM((tq, D_MODEL), BF16)],
        compiler_params=_params(1),
        name="mix_prompt",
    )(sinks, u, q, k, v, gl, wpool_bf, pscale, wbp_bf, wba_bf, wout_bf, x)


def _tile_shift(cur, prev, j, row):
    return jnp.where(row < j, pltpu.roll(prev, j, axis=1), pltpu.roll(cur, j, axis=1))


SEQ_BLOCK = WINDOW // SUBLANES


def _sample_bias4():
    shape = (GQA_GROUP * WINDOW, 2 * WINDOW)
    i = lax.broadcasted_iota(jnp.int32, shape, 0) & (WINDOW - 1)
    j = lax.broadcasted_iota(jnp.int32, shape, 1)
    b, t = i >> 3, i & (SUBLANES - 1)
    new = j - WINDOW
    valid = ((j < WINDOW) & (j >= t)) | ((new >> 3 == b) & ((new & (SUBLANES - 1)) <= t) & (new >= 0))
    return jnp.where(valid, 0.0, -jnp.inf).astype(F32)


def _attend_cached(q, knew, vnew, ckt_ref, cvt_ref, b0, bias4, sinks_ref):
    r, t = WINDOW, SUBLANES
    nt_dims = (((1,), (1,)), ((), ()))
    kn, vn = _dup_kv_heads(knew), _dup_kv_heads(vnew)

    def per_seq(x):
        return [jnp.concatenate([x[j * r + b * t:j * r + (b + 1) * t] for j in range(GQA_GROUP)],
                                axis=0) for b in range(SEQ_BLOCK)]

    def stacked(xs):
        return jnp.concatenate([xs[b][j * t:(j + 1) * t] for j in range(GQA_GROUP)
                                for b in range(SEQ_BLOCK)], axis=0)

    outs = []
    for g in range(N_KV_HEADS):
        kv_rows = slice(g * HEAD_DIM, (g + 1) * HEAD_DIM)
        qs = _stack_heads(q, g)
        s_new = lax.dot_general(qs.astype(BF16), kn[g], nt_dims, preferred_element_type=F32)
        s_cache = []
        for b, qb in enumerate(per_seq(qs)):
            kt = ckt_ref[b0 + b, kv_rows, :]
            s_cache.append(_dot(qb.astype(BF16), jnp.concatenate([kt, kt], axis=0).astype(BF16)))
        s = jnp.concatenate([stacked(s_cache), s_new], axis=1) + bias4
        p, inv = _sink_softmax(s, _sink_column(r, g, sinks_ref))
        o_cache = []
        for b, pb in enumerate(per_seq(p[:, :r])):
            vt = cvt_ref[b0 + b, kv_rows, :]
            o_cache.append(lax.dot_general(pb.astype(BF16),
                                           jnp.concatenate([vt, vt], axis=0).astype(BF16),
                                           nt_dims, preferred_element_type=F32))
        o = stacked(o_cache) + _dot(p[:, r:].astype(BF16), vn[g])
        outs.append(_unstack_heads(o * inv))
    return jnp.concatenate(outs, axis=1)


def _slide_cache(new_rows, ct_ref, out_ref, b0):
    t = SUBLANES
    new_t = new_rows.T
    is_new = lax.broadcasted_iota(jnp.int32, (KV_WIDTH, WINDOW), 1) >= WINDOW - t
    for b in range(SEQ_BLOCK):
        kept = pltpu.roll(ct_ref[b0 + b], WINDOW - t, axis=1)
        fresh = pltpu.roll(new_t, (WINDOW - t - b * t) % WINDOW, axis=1)
        out_ref[b0 + b] = jnp.where(is_new, fresh, kept)


def _mix_sample_kernel(sinks_ref, x_ref, u_ref, st_ref, q_ref, k_ref, v_ref, ckt_ref, cvt_ref, gl_ref,
                       wpool_ref, pscale_ref, wbp_ref, wba_ref, wout_ref, o_ref, kst_ref, vst_ref,
                       oscr, *, gs, pos0):
    t = SUBLANES
    rows = gs * t

    row = lax.broadcasted_iota(jnp.int32, (gs, t, POOL_GROUP), 1)
    cnt_pos = pos0 + lax.broadcasted_iota(jnp.int32, (gs, t, 1), 1) + 1
    a_parts = []
    for g, w in enumerate(POOL_WINDOWS):
        cols = slice(g * POOL_GROUP, (g + 1) * POOL_GROUP)
        tiles = [st_ref[:, 0:t, cols], st_ref[:, t:2 * t, cols], u_ref[:, :, cols]]
        xt = tiles[2]
        step = 1
        while step < w and step < t:
            tiles = [tiles[n] + _tile_shift(tiles[n], tiles[max(n - 1, 0)], step, row)
                     for n in range(3)]
            step *= 2
        acc = tiles[2] + tiles[1] if w == 2 * t else tiles[2]
        cnt = jnp.minimum(cnt_pos, w).astype(F32)
        d = (acc / cnt - xt).reshape(rows, POOL_GROUP)
        y = _dot(d.astype(BF16), wpool_ref[g])
        a_parts.append(y * pscale_ref[:, cols])
    pool_a = jnp.concatenate(a_parts, axis=1)

    bias = _sample_bias4()
    for blk in range(gs // SEQ_BLOCK):
        rs = slice(blk * WINDOW, (blk + 1) * WINDOW)
        oscr[rs, :] = _attend_cached(q_ref[rs, :], k_ref[rs, :], v_ref[rs, :], ckt_ref, cvt_ref,
                                     blk * SEQ_BLOCK, bias, sinks_ref)
        _slide_cache(k_ref[rs, :], ckt_ref, kst_ref, blk * SEQ_BLOCK)
        _slide_cache(v_ref[rs, :], cvt_ref, vst_ref, blk * SEQ_BLOCK)

    o_ref[...] = _merge(x_ref[...], pool_a, oscr[...], gl_ref[...], wbp_ref, wba_ref, wout_ref)


def _mix_sample(x, u3, stpad, q, k, v, ck, cv, gl, sinks, wpool_bf, pscale, wbp_bf, wba_bf, wout_bf,
                gs, pos0):
    nseq, t = u3.shape[0], u3.shape[1]
    rows = gs * t
    row = lambda c: pl.BlockSpec((rows, c), lambda i: (i, 0))
    seq = lambda a, b: pl.BlockSpec((gs, a, b), lambda i: (i, 0, 0))
    return pl.pallas_call(
        functools.partial(_mix_sample_kernel, gs=gs, pos0=pos0),
        grid=(nseq // gs,),
        in_specs=[pl.BlockSpec(memory_space=pltpu.SMEM),
                  row(D_MODEL), seq(t, POOL_WIDTH), seq(2 * t, POOL_WIDTH),
                  row(ATTN_WIDTH), row(KV_WIDTH), row(KV_WIDTH),
                  seq(WINDOW, KV_WIDTH), seq(WINDOW, KV_WIDTH), row(2 * D_MODEL),
                  _const_spec(wpool_bf.shape), _const_spec(pscale.shape),
                  _const_spec(wbp_bf.shape), _const_spec(wba_bf.shape), _const_spec(wout_bf.shape)],
        out_specs=[row(D_MODEL), seq(KV_WIDTH, WINDOW), seq(KV_WIDTH, WINDOW)],
        out_shape=[jax.ShapeDtypeStruct((nseq * t, D_MODEL), F32),
                   jax.ShapeDtypeStruct((nseq, KV_WIDTH, WINDOW), F32),
                   jax.ShapeDtypeStruct((nseq, KV_WIDTH, WINDOW), F32)],
        scratch_shapes=[pltpu.VMEM((rows, ATTN_WIDTH), F32)],
        compiler_params=_params(1),
        name="mix_sample",
    )(sinks, x, u3, stpad, q, k, v, ck, cv, gl, wpool_bf, pscale, wbp_bf, wba_bf, wout_bf)


FF_CHUNK = 256


def _ff_cols(c, part):
    start = part * D_FF + c * FF_CHUNK
    return slice(start, start + FF_CHUNK)


def _gated_act(taps, cw_ref, cb_ref, c):
    def conv(part):
        x0, x1, x2 = taps(c, part)
        cols = _ff_cols(c, part)
        return (x0 * cw_ref[0:1, cols] + x1 * cw_ref[1:2, cols] + x2 * cw_ref[2:3, cols]
                + cb_ref[:, cols])
    h = conv(0)
    return ((h * (1.0 + jnp.tanh(h))) * conv(1)).astype(BF16)


def _ffn_prompt_kernel(x_ref, g_ref, wup_ref, cw_ref, cb_ref, wdn_ref, gf_ref, o_ref, cs_ref,
                       upx, act, *, tm):
    i = pl.program_id(1)
    pre = SUBLANES

    @pl.when(i == 0)
    def _():
        upx[0:pre, :] = jnp.zeros((pre, 2 * D_FF), F32)

    x = x_ref[...]
    h = _rmsnorm(x, g_ref[...]).astype(BF16)
    upx[pre:pre + tm, :] = _dot(h, wup_ref[...])

    def taps(c, part):
        e = upx[:, _ff_cols(c, part)]
        return pltpu.roll(e, 2, axis=0)[pre:], pltpu.roll(e, 1, axis=0)[pre:], e[pre:]

    y = x
    for c in range(D_FF // FF_CHUNK):
        a = _gated_act(taps, cw_ref, cb_ref, c)
        y = y + _dot(a, wdn_ref[c * FF_CHUNK:(c + 1) * FF_CHUNK, :])
    o_ref[...] = _rmsnorm(y, gf_ref[...])
    cs_ref[0] = upx[pre + tm - CONV_STATE:pre + tm, :]
    upx[0:pre, :] = upx[tm:tm + pre, :]


def _ffn_prompt(x, g_ffn, wup_bf, conv_w, conv_b, wdn_bf, g_final, batch, tm):
    n = x.shape[0]
    nt = n // batch // tm
    row = pl.BlockSpec((tm, D_MODEL), lambda b, i: (b * nt + i, 0))
    return pl.pallas_call(
        functools.partial(_ffn_prompt_kernel, tm=tm),
        grid=(batch, nt),
        in_specs=[row, _const_spec((1, D_MODEL)), _const_spec(wup_bf.shape),
                  _const_spec(conv_w.shape), _const_spec(conv_b.shape), _const_spec(wdn_bf.shape),
                  _const_spec((1, D_MODEL))],
        out_specs=[row, pl.BlockSpec((1, CONV_STATE, 2 * D_FF), lambda b, i: (b, 0, 0))],
        out_shape=[jax.ShapeDtypeStruct((n, D_MODEL), F32),
                   jax.ShapeDtypeStruct((batch, CONV_STATE, 2 * D_FF), F32)],
        scratch_shapes=[pltpu.VMEM((SUBLANES + tm, 2 * D_FF), F32),
                        pltpu.VMEM((tm, D_FF), BF16)],
        compiler_params=_params(2),
        name="ffn_prompt",
    )(x, g_ffn, wup_bf, conv_w, conv_b, wdn_bf, g_final)


def _ffn_sample_kernel(x_ref, st_ref, g_ref, wup_ref, cw_ref, cb_ref, wdn_ref, gf_ref,
                       o_ref, cs_ref, up, act, *, gs):
    t = SUBLANES
    rows = gs * t
    x = x_ref[...]
    h = _rmsnorm(x, g_ref[...]).astype(BF16)
    up[...] = _dot(h, wup_ref[...])
    row = lax.broadcasted_iota(jnp.int32, (gs, t, FF_CHUNK), 1)

    def taps(c, part):
        cols = _ff_cols(c, part)
        x2 = up[:, cols].reshape(gs, t, FF_CHUNK)
        s0 = jnp.broadcast_to(st_ref[:, 0:1, cols], x2.shape)
        s1 = jnp.broadcast_to(st_ref[:, 1:2, cols], x2.shape)
        x1 = jnp.where(row < 1, s1, pltpu.roll(x2, 1, axis=1))
        x0 = jnp.where(row < 1, s0, jnp.where(row < 2, s1, pltpu.roll(x2, 2, axis=1)))
        flat = lambda a: a.reshape(rows, FF_CHUNK)
        return flat(x0), flat(x1), flat(x2)

    y = x
    for c in range(D_FF // FF_CHUNK):
        a = _gated_act(taps, cw_ref, cb_ref, c)
        y = y + _dot(a, wdn_ref[c * FF_CHUNK:(c + 1) * FF_CHUNK, :])
    o_ref[...] = _rmsnorm(y, gf_ref[...])
    for c0 in range(0, 2 * D_FF, 4 * FF_CHUNK):
        cols = slice(c0, min(c0 + 4 * FF_CHUNK, 2 * D_FF))
        cs_ref[:, :, cols] = up[:, cols].reshape(gs, t, -1)[:, t - CONV_STATE:, :]


def _ffn_sample(x, st, g_ffn, wup_bf, conv_w, conv_b, wdn_bf, g_final, gs):
    nseq = st.shape[0]
    t = x.shape[0] // nseq
    rows = gs * t
    row = pl.BlockSpec((rows, D_MODEL), lambda i: (i, 0))
    st_spec = pl.BlockSpec((gs, CONV_STATE, 2 * D_FF), lambda i: (i, 0, 0))
    return pl.pallas_call(
        functools.partial(_ffn_sample_kernel, gs=gs),
        grid=(nseq // gs,),
        in_specs=[row, st_spec,
                  _const_spec((1, D_MODEL)), _const_spec(wup_bf.shape),
                  _const_spec(conv_w.shape), _const_spec(conv_b.shape), _const_spec(wdn_bf.shape),
                  _const_spec((1, D_MODEL))],
        out_specs=[row, st_spec],
        out_shape=[jax.ShapeDtypeStruct((nseq * t, D_MODEL), F32),
                   jax.ShapeDtypeStruct((nseq, CONV_STATE, 2 * D_FF), F32)],
        scratch_shapes=[pltpu.VMEM((rows, 2 * D_FF), F32), pltpu.VMEM((rows, D_FF), BF16)],
        compiler_params=_params(1),
        name="ffn_sample",
    )(x, st, g_ffn, wup_bf, conv_w, conv_b, wdn_bf, g_final)


TM_IN = 1024
TQ_MIX = 512
TM_FFN = 512
GS_MIX = 32
GS_FFN = 64


def kernel(x_prompt, x_sample, state_pool, cache_k, cache_v, state_conv, g_mix, w_in, w_pool_grp,
           pool_scale, sinks, w_branch_pool, w_branch_attn, w_out, g_ffn, w_up, conv_w, conv_b,
           w_down, g_final):
    depth = g_mix.shape[0]
    batch, seq, _ = x_prompt.shape
    dec_batch, dec_seq, _ = x_sample.shape
    assert dec_seq == SUBLANES and seq % TQ_MIX == 0 and seq % TM_FFN == 0
    assert depth == 1

    yp = x_prompt.reshape(batch * seq, D_MODEL)
    ys = x_sample.reshape(dec_batch * dec_seq, D_MODEL)
    outs = [[] for _ in range(8)]
    gfin = g_final.reshape(1, D_MODEL)
    for l in range(depth):
        gmix = g_mix[l].reshape(1, D_MODEL)
        gffn = g_ffn[l].reshape(1, D_MODEL)
        w_in_bf = w_in[l].astype(BF16)
        wpool_bf = w_pool_grp[l].astype(BF16)
        pscale = pool_scale[l].reshape(1, POOL_WIDTH)
        wbp_bf = w_branch_pool[l].astype(BF16)
        wba_bf = w_branch_attn[l].astype(BF16)
        wout_bf = w_out[l].astype(BF16)
        wup_bf = w_up[l].astype(BF16)
        wdn_bf = w_down[l].astype(BF16)
        half_gate = jnp.where(jnp.arange(2 * D_FF) < D_FF, 0.5, 1.0).astype(F32)
        cw = conv_w[l] * half_gate
        cb = (conv_b[l] * half_gate).reshape(1, 2 * D_FF)

        u, q, k, v, gl = _in_proj(yp, gmix, w_in_bf, TM_IN)
        x1, kt, vt = _mix_prompt(yp, u, q, k, v, gl, sinks[l], wpool_bf, pscale, wbp_bf, wba_bf,
                                 wout_bf, batch, TQ_MIX)
        yp, conv_p = _ffn_prompt(x1, gffn, wup_bf, cw, cb, wdn_bf, gfin, batch, TM_FFN)
        outs[0].append(u.reshape(batch, seq, POOL_WIDTH)[:, seq - POOL_STATE:])
        untranspose = lambda t: t.reshape(-1, N_KV_HEADS, HEAD_DIM, WINDOW).transpose(0, 3, 1, 2)
        outs[1].append(untranspose(kt))
        outs[2].append(untranspose(vt))
        outs[3].append(conv_p)

        us, qs, ks, vs, gls = _in_proj(ys, gmix, w_in_bf, TM_IN)
        u3 = us.reshape(dec_batch, dec_seq, POOL_WIDTH)
        stpad = jnp.pad(state_pool[l], ((0, 0), (1, 0), (0, 0)))
        transposed = lambda c: c.transpose(0, 2, 3, 1).reshape(dec_batch, KV_WIDTH, WINDOW)
        ck, cv = transposed(cache_k[l]), transposed(cache_v[l])
        x1s, kst, vst = _mix_sample(ys, u3, stpad, qs, ks, vs, ck, cv, gls, sinks[l], wpool_bf,
                                    pscale, wbp_bf, wba_bf, wout_bf, GS_MIX, PAST_LEN)
        ys, conv_s = _ffn_sample(x1s, state_conv[l], gffn, wup_bf, cw, cb, wdn_bf, gfin, GS_FFN)
        outs[4].append(jnp.concatenate([state_pool[l], u3], axis=1)[:, dec_seq:])
        outs[5].append(untranspose(kst))
        outs[6].append(untranspose(vst))
        outs[7].append(conv_s)
    return (yp.reshape(batch, seq, D_MODEL), ys.reshape(dec_batch, dec_seq, D_MODEL),
            *[jnp.stack(o) for o in outs])
```
